```python
import math
import jax, jax.numpy as jnp
from jax import lax
import numpy as np

D_MODEL = 1024
BATCH = 32
SEQ = 2048
DEPTH = 1

DA_HEADS = 8
DA_DK = 64
DA_DV = 2 * DA_DK
Q_BLOCK = 128
ML_HEADS = 4
ML_DK = 128
ML_DV = 256
ML_CHUNK = 128
CONV_K = 4
PEER_HEADS = 8
PEER_TOPK = 16
N_KEYS = 128
N_EXPERTS = N_KEYS * N_KEYS
PEER_DKEY = 128
PEER_CHUNK = 128

DA_QK_W = DA_HEADS * 2 * DA_DK
DA_V_W = DA_HEADS * DA_DV
ML_QK_W = ML_HEADS * ML_DK
ML_V_W = ML_HEADS * ML_DV
IN_SIZES = (DA_QK_W, DA_QK_W, DA_V_W, ML_QK_W, ML_QK_W, ML_V_W, ML_V_W,
            ML_HEADS, ML_HEADS, D_MODEL, D_MODEL)
IN_W = sum(IN_SIZES)

ALPHA = (2 * DEPTH) ** 0.25
BETA = (8 * DEPTH) ** -0.25
LN_EPS = 1e-5

kernel_name = "hybrid_diffattn_mlstm_peer_deepnorm_adaln"


def _split_cols(z, sizes):
    idx = np.cumsum(np.array(sizes))[:-1].tolist()
    return jnp.split(z, idx, axis=-1)


def _layer_norm(x):
    xf = x.astype(jnp.float32)
    mu = jnp.mean(xf, axis=-1, keepdims=True)
    var = jnp.mean(jnp.square(xf - mu), axis=-1, keepdims=True)
    return ((xf - mu) * lax.rsqrt(var + LN_EPS)).astype(x.dtype)


def _rms_norm(x, g):
    xf = x.astype(jnp.float32)
    y = xf * lax.rsqrt(jnp.mean(jnp.square(xf), axis=-1, keepdims=True) + LN_EPS)
    return y * g.astype(jnp.float32)


def _causal_depthwise_conv(x, w, b):
    C = x.shape[-1]
    y = lax.conv_general_dilated(x, w[:, None, :].astype(x.dtype), window_strides=(1,),
                                 padding=[(CONV_K - 1, 0)],
                                 dimension_numbers=('NWC', 'WIO', 'NWC'),
                                 feature_group_count=C)
    return y + b


def _diff_attention(q, k, v, lam, subln_g, lambda_init):
    dtype = v.dtype
    q = jnp.transpose(q, (0, 2, 3, 1, 4))
    k = jnp.transpose(k, (0, 2, 3, 1, 4))
    v = jnp.transpose(v, (0, 2, 1, 3)).astype(jnp.float32)
    S = q.shape[3]
    lamf = lam.astype(jnp.float32)
    lam_val = (jnp.exp(jnp.sum(lamf[0] * lamf[1])) - jnp.exp(jnp.sum(lamf[2] * lamf[3]))
               + lambda_init)
    scale = DA_DK ** -0.5
    outs = []
    for blk in range(S // Q_BLOCK):
        q0 = blk * Q_BLOCK
        kend = q0 + Q_BLOCK
        qb = q[:, :, :, q0:kend]
        kb = k[:, :, :, :kend]
        s = jnp.einsum('bhmqd,bhmkd->bhmqk', qb, kb).astype(jnp.float32) * scale
        mask = jnp.arange(kend)[None, :] <= (q0 + jnp.arange(Q_BLOCK))[:, None]
        p = jax.nn.softmax(jnp.where(mask, s, -jnp.inf), axis=-1)
        a = p[:, :, 0] - lam_val * p[:, :, 1]
        outs.append(jnp.einsum('bhqk,bhkd->bhqd', a, v[:, :, :kend]))
    o = jnp.concatenate(outs, axis=2)
    o = _rms_norm(o, subln_g) * (1.0 - lambda_init)
    B = o.shape[0]
    return jnp.transpose(o, (0, 2, 1, 3)).reshape(B, S, DA_HEADS * DA_DV).astype(dtype)


def _mlstm(q, k, v, i_pre, f_pre, norm_g):
    dtype = v.dtype
    B, S, H, _ = q.shape
    L = ML_CHUNK
    NC = S // L
    q = q.astype(jnp.float32)
    k = k.astype(jnp.float32) * (ML_DK ** -0.5)
    v = v.astype(jnp.float32)
    ig = i_pre.astype(jnp.float32)
    lf = jax.nn.log_sigmoid(f_pre.astype(jnp.float32))

    def to_chunks(t):
        return jnp.transpose(t.reshape(B, NC, L, H, t.shape[-1]), (1, 0, 3, 2, 4))

    def g_chunks(t):
        return jnp.transpose(t.reshape(B, NC, L, H), (1, 0, 3, 2))

    causal = jnp.tril(jnp.ones((L, L), dtype=bool))

    def step(carry, inp):
        C, n, m = carry
        qc, kc, vc, ic, fc = inp
        b = jnp.cumsum(fc, axis=-1)
        Dm = b[..., :, None] - b[..., None, :] + ic[..., None, :]
        Dm = jnp.where(causal, Dm, -jnp.inf)
        m_inter = b + m[..., None]
        m_t = jnp.maximum(m_inter, jnp.max(Dm, axis=-1))
        W = jnp.exp(Dm - m_t[..., None])
        P = W * jnp.einsum('bhtd,bhsd->bhts', qc, kc)
        inter = jnp.exp(m_inter - m_t)
        num = (jnp.einsum('bhts,bhsv->bhtv', P, vc)
               + inter[..., None] * jnp.einsum('bhvd,bhtd->bhtv', C, qc))
        nq = jnp.sum(P, axis=-1) + inter * jnp.einsum('bhd,bhtd->bht', n, qc)
        h = num / jnp.maximum(jnp.abs(nq), jnp.exp(-m_t))[..., None]
        m_new = m_t[..., -1]
        decay = jnp.exp(b[..., -1] + m - m_new)
        w_s = jnp.exp(b[..., -1:] - b + ic - m_new[..., None])
        C_new = decay[..., None, None] * C + jnp.einsum('bhsv,bhsd->bhvd', vc * w_s[..., None], kc)
        n_new = decay[..., None] * n + jnp.einsum('bhs,bhsd->bhd', w_s, kc)
        return (C_new, n_new, m_new), h

    init = (jnp.zeros((B, H, ML_DV, ML_DK), jnp.float32),
            jnp.zeros((B, H, ML_DK), jnp.float32),
            jnp.zeros((B, H), jnp.float32))
    _, hs = lax.scan(step, init, (to_chunks(q), to_chunks(k), to_chunks(v), g_chunks(ig), g_chunks(lf)))
    h = jnp.transpose(hs, (1, 0, 3, 2, 4)).reshape(B, S, H, ML_DV)
    h = _rms_norm(h, norm_g.reshape(H, ML_DV))
    return h.reshape(B, S, H * ML_DV).astype(dtype)


def _peer(h, w_q, sub_keys, u_tab, v_tab):
    B, S, D = h.shape
    T = B * S
    ht = h.reshape(T, D)
    q = (ht @ w_q).reshape(T, PEER_HEADS, 2, PEER_DKEY // 2)
    s = jnp.einsum('thpd,pnd->thpn', q, sub_keys).astype(jnp.float32)
    sv, si = lax.top_k(s, PEER_TOPK)
    cand = (sv[:, :, 0, :, None] + sv[:, :, 1, None, :]).reshape(T, PEER_HEADS, PEER_TOPK * PEER_TOPK)
    cv, ci = lax.top_k(cand, PEER_TOPK)
    i1 = jnp.take_along_axis(si[:, :, 0], ci // PEER_TOPK, axis=-1)
    i2 = jnp.take_along_axis(si[:, :, 1], ci % PEER_TOPK, axis=-1)
    eidx = i1 * N_KEYS + i2
    g = jax.nn.softmax(cv, axis=-1)
    nch = T // PEER_CHUNK

    def expert_block(args):
        hc, ec, gc = args
        uc = u_tab[ec]
        act = jax.nn.gelu(jnp.einsum('cd,chkd->chk', hc, uc), approximate=False)
        vc = v_tab[ec]
        return jnp.einsum('chk,chkd->cd', (gc * act).astype(hc.dtype), vc)

    out = lax.map(expert_block, (ht.reshape(nch, PEER_CHUNK, D),
                                 eidx.reshape(nch, PEER_CHUNK, PEER_HEADS, PEER_TOPK),
                                 g.reshape(nch, PEER_CHUNK, PEER_HEADS, PEER_TOPK)))
    return out.reshape(B, S, D)


def setup_inputs(seed: int = 0) -> dict:
    key = jax.random.key(seed)
    ks = jax.random.split(key, 26)
    f32 = jnp.float32
    nrm = lambda k, shape, s: jax.random.normal(k, shape, f32) * s
    gain = lambda k, shape: 1.0 + 0.02 * jax.random.normal(k, shape, f32)
    b_if = jnp.stack([0.1 * jax.random.normal(ks[4], (DEPTH, ML_HEADS), f32),
                      jnp.linspace(3.0, 6.0, ML_HEADS, dtype=f32)[None, :]
                      + 0.1 * jax.random.normal(ks[5], (DEPTH, ML_HEADS), f32)], axis=1)
    return {
        "x": nrm(ks[0], (BATCH, SEQ, D_MODEL), 1.0),
        "c": nrm(ks[1], (BATCH, D_MODEL), 1.0),
        "w_ada": nrm(ks[2], (DEPTH, D_MODEL, 6 * D_MODEL), 0.5 * D_MODEL ** -0.5),
        "b_ada": nrm(ks[3], (DEPTH, 6 * D_MODEL), 0.02),
        "w_in": nrm(ks[6], (DEPTH, D_MODEL, IN_W), D_MODEL ** -0.5),
        "b_if": b_if,
        "conv_w": nrm(ks[7], (DEPTH, CONV_K, 2 * ML_QK_W), CONV_K ** -0.5),
        "conv_b": nrm(ks[8], (DEPTH, 2 * ML_QK_W), 0.02),
        "da_lambda": nrm(ks[9], (DEPTH, 4, DA_DK), 0.1),
        "da_subln_g": gain(ks[10], (DEPTH, DA_DV)),
        "ml_norm_g": gain(ks[11], (DEPTH, ML_V_W)),
        "w_br_attn": nrm(ks[12], (DEPTH, DA_V_W, D_MODEL), BETA * DA_V_W ** -0.5),
        "w_br_mlstm": nrm(ks[13], (DEPTH, ML_V_W, D_MODEL), BETA * ML_V_W ** -0.5),
        "w_out": nrm(ks[14], (DEPTH, D_MODEL, D_MODEL), BETA * D_MODEL ** -0.5),
        "ln1_g": gain(ks[15], (DEPTH, D_MODEL)),
        "ln1_b": nrm(ks[16], (DEPTH, D_MODEL), 0.02),
        "peer_wq": nrm(ks[17], (DEPTH, D_MODEL, PEER_HEADS * PEER_DKEY), D_MODEL ** -0.5),
        "peer_keys": nrm(ks[18], (DEPTH, 2, N_KEYS, PEER_DKEY // 2), (PEER_DKEY // 2) ** -0.5),
        "peer_u": nrm(ks[19], (DEPTH, N_EXPERTS, D_MODEL), D_MODEL ** -0.5),
        "peer_v": nrm(ks[20], (DEPTH, N_EXPERTS, D_MODEL), BETA * PEER_HEADS ** -0.5),
        "ln2_g": gain(ks[21], (DEPTH, D_MODEL)),
        "ln2_b": nrm(ks[22], (DEPTH, D_MODEL), 0.02),
    }


def reference(x, c, w_ada, b_ada, w_in, b_if, conv_w, conv_b, da_lambda, da_subln_g,
              ml_norm_g, w_br_attn, w_br_mlstm, w_out, ln1_g, ln1_b, peer_wq, peer_keys,
              peer_u, peer_v, ln2_g, ln2_b):
    B, S, D = x.shape
    for l in range(DEPTH):
        mod = jax.nn.silu(c) @ w_ada[l] + b_ada[l]
        sh1, sc1, gt1, sh2, sc2, gt2 = jnp.split(mod[:, None, :], 6, axis=-1)

        h = _layer_norm(x) * (1.0 + sc1) + sh1
        z = h @ w_in[l]
        (da_q, da_k, da_v, ml_q, ml_k, ml_v, ml_o, ml_i, ml_f,
         g_attn, g_ml) = _split_cols(z, IN_SIZES)
        qk = jax.nn.silu(_causal_depthwise_conv(jnp.concatenate([ml_q, ml_k], axis=-1),
                                                conv_w[l], conv_b[l]))
        ml_q, ml_k = jnp.split(qk, 2, axis=-1)
        lambda_init = 0.8 - 0.6 * math.exp(-0.3 * l)
        ya = _diff_attention(da_q.reshape(B, S, DA_HEADS, 2, DA_DK),
                             da_k.reshape(B, S, DA_HEADS, 2, DA_DK),
                             da_v.reshape(B, S, DA_HEADS, DA_DV),
                             da_lambda[l], da_subln_g[l], lambda_init)
        ym = _mlstm(ml_q.reshape(B, S, ML_HEADS, ML_DK),
                    ml_k.reshape(B, S, ML_HEADS, ML_DK),
                    ml_v.reshape(B, S, ML_HEADS, ML_DV),
                    ml_i + b_if[l, 0], ml_f + b_if[l, 1], ml_norm_g[l])
        ym = ym * jax.nn.sigmoid(ml_o)
        y = (jax.nn.sigmoid(g_attn) * (ya @ w_br_attn[l])
             + jax.nn.sigmoid(g_ml) * (ym @ w_br_mlstm[l]))
        x = _layer_norm(ALPHA * x + gt1 * (y @ w_out[l])) * ln1_g[l] + ln1_b[l]

        h = _layer_norm(x) * (1.0 + sc2) + sh2
        yf = _peer(h, peer_wq[l], peer_keys[l], peer_u[l], peer_v[l])
        x = _layer_norm(ALPHA * x + gt2 * yf) * ln2_g[l] + ln2_b[l]
    return x
```

```python
import functools
import math

import jax
import jax.numpy as jnp
from jax import lax
from jax.experimental import pallas as pl
from jax.experimental.pallas import tpu as pltpu

D_MODEL = 1024
DA_HEADS = 8
DA_DK = 64
DA_DV = 2 * DA_DK
ML_HEADS = 4
ML_DK = 128
ML_DV = 256
ML_CHUNK = 128
CONV_K = 4
PEER_HEADS = 8
PEER_TOPK = 16
N_KEYS = 128
N_EXPERTS = N_KEYS * N_KEYS
PEER_DKEY = 128
DEPTH = 1
ALPHA = (2 * DEPTH) ** 0.25
LN_EPS = 1e-5

F32 = jnp.float32
BF16 = jnp.bfloat16
NEG_INF = float("-inf")

VMEM_LIMIT_BYTES = 56 * 1024 * 1024


def _cparams(sem):
    return pltpu.CompilerParams(dimension_semantics=sem, vmem_limit_bytes=VMEM_LIMIT_BYTES)


def _layer_norm_rows(x):
    mu = jnp.mean(x, axis=-1, keepdims=True)
    xc = x - mu
    var = jnp.mean(xc * xc, axis=-1, keepdims=True)
    return xc * lax.rsqrt(var + LN_EPS)


def _dot(a, b):
    return jnp.dot(a, b, preferred_element_type=F32)


def _dot_nt(a, b):
    return lax.dot_general(a, b, (((1,), (1,)), ((), ())), preferred_element_type=F32)


def _dot_tn(a, b):
    return lax.dot_general(a, b, (((0,), (0,)), ((), ())), preferred_element_type=F32)


def _mod_kernel(c_ref, w_ref, b_ref, o_ref):
    c = c_ref[...]
    a = c * jax.nn.sigmoid(c)
    o_ref[...] = jnp.dot(a, w_ref[...], preferred_element_type=F32,
                         precision=lax.Precision.HIGHEST) + b_ref[...]


def _modulation(c, w_ada, b_ada):
    B, D = c.shape
    N = w_ada.shape[1]
    tn = 1024
    return pl.pallas_call(
        _mod_kernel,
        grid=(N // tn,),
        in_specs=[pl.BlockSpec((B, D), lambda n: (0, 0)),
                  pl.BlockSpec((D, tn), lambda n: (0, n)),
                  pl.BlockSpec((1, tn), lambda n: (0, n))],
        out_specs=pl.BlockSpec((B, tn), lambda n: (0, n)),
        out_shape=jax.ShapeDtypeStruct((B, N), F32),
        compiler_params=_cparams(("arbitrary",)),
        name="modulation",
    )(c, w_ada, b_ada.reshape(1, N))


def _inproj_kernel(x_ref, mod_ref, w_ref, wg_ref, wgt_ref, bcol_ref, brow_ref,
                   z_ref, gcol_ref, grow_ref, h_scr):
    n = pl.program_id(1)

    @pl.when(n == 0)
    def _():
        hn = _layer_norm_rows(x_ref[...])
        sh1 = mod_ref[0, 0:1, :]
        sc1 = mod_ref[0, 1:2, :]
        hb = (hn * (1.0 + sc1) + sh1).astype(BF16)
        h_scr[...] = hb
        gcol_ref[...] = _dot(hb, wg_ref[...]) + bcol_ref[...]
        grow_ref[...] = _dot_nt(wgt_ref[...], hb) + brow_ref[...]

    z_ref[0] = _dot(h_scr[...], w_ref[0]).astype(BF16)


def _in_proj(x2, mod3, w8, wg, wgt, bcol, brow, S):
    T, D = x2.shape
    tm = 1024
    npiece = w8.shape[0]
    return pl.pallas_call(
        _inproj_kernel,
        grid=(T // tm, npiece),
        in_specs=[pl.BlockSpec((tm, D), lambda i, n: (i, 0)),
                  pl.BlockSpec((1, 6, D), lambda i, n: ((i * tm) // S, 0, 0)),
                  pl.BlockSpec((1, D, D), lambda i, n: (n, 0, 0)),
                  pl.BlockSpec((D, 128), lambda i, n: (0, 0)),
                  pl.BlockSpec((8, D), lambda i, n: (0, 0)),
                  pl.BlockSpec((1, 128), lambda i, n: (0, 0)),
                  pl.BlockSpec((8, 1), lambda i, n: (0, 0))],
        out_specs=[pl.BlockSpec((1, tm, D), lambda i, n: (n, i, 0)),
                   pl.BlockSpec((tm, 128), lambda i, n: (i, 0)),
                   pl.BlockSpec((8, tm), lambda i, n: (0, i))],
        out_shape=[jax.ShapeDtypeStruct((npiece, T, D), BF16),
                   jax.ShapeDtypeStruct((T, 128), F32),
                   jax.ShapeDtypeStruct((8, T), F32)],
        scratch_shapes=[pltpu.VMEM((tm, D), BF16)],
        compiler_params=_cparams(("parallel", "arbitrary")),
        name="in_proj",
    )(x2, mod3, w8, wg, wgt, bcol, brow)


DA_TQ = 256


def _diffattn_kernel(lam_ref, g_ref, q_ref, k_ref, v_ref, o_ref, *, S, lambda_init):
    tq = DA_TQ
    lam = lam_ref[...]
    t1 = jnp.sum(lam[0:1] * lam[1:2], axis=-1, keepdims=True)
    t2 = jnp.sum(lam[2:3] * lam[3:4], axis=-1, keepdims=True)
    lam_val = jnp.exp(t1) - jnp.exp(t2) + lambda_init
    first_map = lax.broadcasted_iota(jnp.int32, (1, DA_DV), 1) < DA_DK
    gain = g_ref[...] * (1.0 - lambda_init)
    row = lax.broadcasted_iota(jnp.int32, (tq, tq), 0)
    col = lax.broadcasted_iota(jnp.int32, (tq, tq), 1)
    causal = col <= row

    def q_block(qi, carry):
        q0 = pl.multiple_of(qi * tq, tq)
        qs = q_ref[0, 0, pl.ds(q0, tq), :] * (DA_DK ** -0.5)
        zero = jnp.zeros_like(qs)
        q1 = jnp.where(first_map, qs, zero)
        q2 = jnp.where(first_map, zero, qs)

        def kv_step(k0, state, masked):
            m1, l1, a1, m2, l2, a2 = state
            k = k_ref[0, 0, pl.ds(k0, tq), :]
            v = v_ref[0, 0, pl.ds(k0, tq), :]
            out = []
            for qm, m, l, a in ((q1, m1, l1, a1), (q2, m2, l2, a2)):
                s = _dot_nt(qm, k)
                if masked:
                    s = jnp.where(causal, s, NEG_INF)
                mn = jnp.maximum(m, jnp.max(s, axis=-1, keepdims=True))
                p = jnp.exp(s - mn)
                alpha = jnp.exp(m - mn)
                l = alpha * l + jnp.sum(p, axis=-1, keepdims=True)
                a = alpha * a + _dot(p.astype(BF16), v)
                out += [mn, l, a]
            return tuple(out)

        init = (jnp.full((tq, 1), NEG_INF, F32), jnp.zeros((tq, 1), F32), jnp.zeros((tq, DA_DV), F32),
                jnp.full((tq, 1), NEG_INF, F32), jnp.zeros((tq, 1), F32), jnp.zeros((tq, DA_DV), F32))
        state = kv_step(q0, init, True)
        state = lax.fori_loop(0, qi, lambda kj, st: kv_step(pl.multiple_of(kj * tq, tq), st, False), state)
        m1, l1, a1, m2, l2, a2 = state
        o = a1 / l1 - lam_val * (a2 / l2)
        o = o * lax.rsqrt(jnp.mean(o * o, axis=-1, keepdims=True) + LN_EPS) * gain
        o_ref[0, pl.ds(q0, tq), :] = o.astype(BF16)
        return carry

    lax.fori_loop(0, S // tq, q_block, 0)


def _diff_attention(z4, da_lambda, subln_g, B, S, lambda_init):
    kern = functools.partial(_diffattn_kernel, S=S, lambda_init=lambda_init)
    return pl.pallas_call(
        kern,
        grid=(B, DA_HEADS),
        in_specs=[pl.BlockSpec((4, DA_DK), lambda b, h: (0, 0)),
                  pl.BlockSpec((1, DA_DV), lambda b, h: (0, 0)),
                  pl.BlockSpec((1, 1, S, DA_DV), lambda b, h: (0, b, 0, h)),
                  pl.BlockSpec((1, 1, S, DA_DV), lambda b, h: (1, b, 0, h)),
                  pl.BlockSpec((1, 1, S, DA_DV), lambda b, h: (2, b, 0, h))],
        out_specs=pl.BlockSpec((1, S, DA_DV), lambda b, h: (b, 0, h)),
        out_shape=jax.ShapeDtypeStruct((B, S, DA_HEADS * DA_DV), BF16),
        compiler_params=_cparams(("parallel", "parallel")),
        name="diff_attention",
    )(da_lambda, subln_g.reshape(1, DA_DV), z4, z4, z4)


def _mlstm_kernel(q_ref, k_ref, v_ref, og_ref, gcol_ref, grow_ref, cwq_ref, cwk_ref, cbq_ref, cbk_ref,
                  ng_ref, o_ref, qc_scr, kc_scr, *, S):
    L = ML_CHUNK
    hh = pl.program_id(1)
    srow = lax.broadcasted_iota(jnp.int32, (S, ML_DK), 0)

    def conv_silu(x_ref, w_ref, b_ref):
        x = x_ref[0, 0].astype(F32)
        y = x * w_ref[CONV_K - 1:CONV_K, :] + b_ref[...]
        for j in range(1, CONV_K):
            xs = jnp.where(srow >= j, pltpu.roll(x, j, 0), 0.0)
            y = y + xs * w_ref[CONV_K - 1 - j:CONV_K - j, :]
        return y * jax.nn.sigmoid(y)

    qc_scr[...] = conv_silu(q_ref, cwq_ref, cbq_ref)
    kc_scr[...] = conv_silu(k_ref, cwk_ref, cbk_ref) * (ML_DK ** -0.5)

    r_i = lax.broadcasted_iota(jnp.int32, (L, L), 0)
    c_i = lax.broadcasted_iota(jnp.int32, (L, L), 1)
    causal = c_i <= r_i
    tril = causal.astype(F32)
    triu = (r_i <= c_i).astype(F32)
    lane = lax.broadcasted_iota(jnp.int32, (1, 128), 1)
    sel_i = (lane == hh).astype(F32)
    sel_f = (lane == hh + ML_HEADS).astype(F32)
    sub = lax.broadcasted_iota(jnp.int32, (8, 1), 0)
    rsel_i = (sub == hh).astype(F32)
    rsel_f = (sub == hh + ML_HEADS).astype(F32)
    ngain = ng_ref[...]

    def chunk(ci, state):
        Ct, n_row, m = state
        t0 = pl.multiple_of(ci * L, L)
        q = qc_scr[pl.ds(t0, L), :]
        k = kc_scr[pl.ds(t0, L), :]
        v = v_ref[0, 0, pl.ds(t0, L), :]
        gc = gcol_ref[0, pl.ds(t0, L), :]
        gr = grow_ref[:, pl.ds(t0, L)]
        ig_col = jnp.sum(gc * sel_i, axis=-1, keepdims=True)
        lf_col = jax.nn.log_sigmoid(jnp.sum(gc * sel_f, axis=-1, keepdims=True))
        ig_row = jnp.sum(gr * rsel_i, axis=0, keepdims=True)
        lf_row = jax.nn.log_sigmoid(jnp.sum(gr * rsel_f, axis=0, keepdims=True))
        b_col = jnp.dot(tril, jnp.broadcast_to(lf_col, (L, 128)), preferred_element_type=F32,
                        precision=lax.Precision.HIGHEST)[:, 0:1]
        b_row = jnp.dot(jnp.broadcast_to(lf_row, (8, L)), triu, preferred_element_type=F32,
                        precision=lax.Precision.HIGHEST)[0:1, :]
        dm = jnp.where(causal, b_col - b_row + ig_row, NEG_INF)
        m_inter = b_col + m
        m_t = jnp.maximum(m_inter, jnp.max(dm, axis=-1, keepdims=True))
        w = jnp.exp(dm - m_t)
        qb = q.astype(BF16)
        kb = k.astype(BF16)
        p = w * _dot_nt(qb, kb)
        inter = jnp.exp(m_inter - m_t)
        num = _dot(p.astype(BF16), v) + inter * _dot(qb, Ct.astype(BF16))
        nq = jnp.sum(p, axis=-1, keepdims=True) + inter * jnp.sum(q * n_row, axis=-1, keepdims=True)
        hout = num / jnp.maximum(jnp.abs(nq), jnp.exp(-m_t))
        hout = hout * lax.rsqrt(jnp.mean(hout * hout, axis=-1, keepdims=True) + LN_EPS) * ngain
        og = og_ref[0, 0, pl.ds(t0, L), :].astype(F32)
        o_ref[0, pl.ds(t0, L), :] = (hout * jax.nn.sigmoid(og)).astype(BF16)
        m_new = m_t[L - 1:L, :]
        b_last = b_col[L - 1:L, :]
        decay = jnp.exp(b_last + m - m_new)
        w_s = jnp.exp(b_last - b_col + ig_col - m_new)
        Ct = decay * Ct + _dot_tn(kb, (v.astype(F32) * w_s).astype(BF16))
        n_row = decay * n_row + jnp.sum(k * w_s, axis=0, keepdims=True)
        return Ct, n_row, m_new

    init = (jnp.zeros((ML_DK, ML_DV), F32), jnp.zeros((1, ML_DK), F32), jnp.zeros((1, 1), F32))
    lax.fori_loop(0, S // L, chunk, init)


def _mlstm(z4, gcol3, grow, conv_w, conv_b, norm_g, B, S):
    kern = functools.partial(_mlstm_kernel, S=S)
    H = ML_HEADS
    return pl.pallas_call(
        kern,
        grid=(B, H),
        in_specs=[pl.BlockSpec((1, 1, S, ML_DK), lambda b, h: (3, b, 0, h)),
                  pl.BlockSpec((1, 1, S, ML_DK), lambda b, h: (3, b, 0, H + h)),
                  pl.BlockSpec((1, 1, S, ML_DV), lambda b, h: (4, b, 0, h)),
                  pl.BlockSpec((1, 1, S, ML_DV), lambda b, h: (5, b, 0, h)),
                  pl.BlockSpec((1, S, 128), lambda b, h: (b, 0, 0)),
                  pl.BlockSpec((8, S), lambda b, h: (0, b)),
                  pl.BlockSpec((CONV_K, ML_DK), lambda b, h: (0, h)),
                  pl.BlockSpec((CONV_K, ML_DK), lambda b, h: (0, H + h)),
                  pl.BlockSpec((1, ML_DK), lambda b, h: (0, h)),
                  pl.BlockSpec((1, ML_DK), lambda b, h: (0, H + h)),
                  pl.BlockSpec((1, ML_DV), lambda b, h: (0, h))],
        out_specs=pl.BlockSpec((1, S, ML_DV), lambda b, h: (b, 0, h)),
        out_shape=jax.ShapeDtypeStruct((B, S, H * ML_DV), BF16),
        scratch_shapes=[pltpu.VMEM((S, ML_DK), F32), pltpu.VMEM((S, ML_DK), F32)],
        compiler_params=_cparams(("parallel", "parallel")),
        name="mlstm",
    )(z4, z4, z4, z4, gcol3, grow, conv_w, conv_w, conv_b, conv_b, norm_g)


def _merge_kernel(ya_ref, ym_ref, ga_ref, gm_ref, x_ref, mod_ref, wa_ref, wm_ref, wo_ref, g1_ref, b1_ref,
                  x1_ref, h2_ref):
    ya = _dot(ya_ref[...], wa_ref[...])
    ym = _dot(ym_ref[...], wm_ref[...])
    y = (jax.nn.sigmoid(ga_ref[0].astype(F32)) * ya + jax.nn.sigmoid(gm_ref[0].astype(F32)) * ym)
    y2 = _dot(y.astype(BF16), wo_ref[...])
    gt1 = mod_ref[0, 2:3, :]
    sh2 = mod_ref[0, 3:4, :]
    sc2 = mod_ref[0, 4:5, :]
    x1 = _layer_norm_rows(ALPHA * x_ref[...] + gt1 * y2) * g1_ref[...] + b1_ref[...]
    x1_ref[...] = x1
    h2_ref[...] = (_layer_norm_rows(x1) * (1.0 + sc2) + sh2).astype(BF16)


def _merge(ya2, ym2, z3, x2, mod3, wa, wm, wo, ln_g, ln_b, S):
    T, D = x2.shape
    tm = 512
    tok = lambda i: (i, 0)
    const = lambda i: (0, 0)
    return pl.pallas_call(
        _merge_kernel,
        grid=(T // tm,),
        in_specs=[pl.BlockSpec((tm, D), tok), pl.BlockSpec((tm, D), tok),
                  pl.BlockSpec((1, tm, D), lambda i: (6, i, 0)),
                  pl.BlockSpec((1, tm, D), lambda i: (7, i, 0)),
                  pl.BlockSpec((tm, D), tok),
                  pl.BlockSpec((1, 6, D), lambda i: ((i * tm) // S, 0, 0)),
                  pl.BlockSpec((D, D), const), pl.BlockSpec((D, D), const), pl.BlockSpec((D, D), const),
                  pl.BlockSpec((1, D), const), pl.BlockSpec((1, D), const)],
        out_specs=[pl.BlockSpec((tm, D), tok), pl.BlockSpec((tm, D), tok)],
        out_shape=[jax.ShapeDtypeStruct((T, D), F32), jax.ShapeDtypeStruct((T, D), BF16)],
        compiler_params=_cparams(("parallel",)),
        name="merge",
    )(ya2, ym2, z3, z3, x2, mod3, wa, wm, wo, ln_g, ln_b)


PEER_TB = 256
_CAND = [(a, b) for a in range(PEER_TOPK) for b in range(PEER_TOPK) if (a + 1) * (b + 1) <= PEER_TOPK]


def _extract_top(s, n, iota):
    nrow = s.shape[0]
    vals = []
    for _ in range(n):
        m = jnp.max(s, axis=0, keepdims=True)
        first = jnp.min(jnp.where(s == m, iota, nrow), axis=0, keepdims=True)
        vals.append(m)
        s = jnp.where(iota == first, NEG_INF, s)
    return vals


def _route_kernel(h_ref, wqt_ref, keys_ref, s1_ref, s2_ref, st_ref, q_scr):
    tb = PEER_TB
    half = PEER_DKEY // 2
    q_scr[...] = _dot_nt(wqt_ref[...], h_ref[...]).astype(BF16)
    iota_k = lax.broadcasted_iota(jnp.int32, (N_KEYS, tb), 0)
    ncand = len(_CAND)
    npad = -ncand % 8
    iota_c = lax.broadcasted_iota(jnp.int32, (ncand + npad, tb), 0)

    def head(h, carry):
        base = pl.multiple_of(h * PEER_DKEY, PEER_DKEY)
        tops = []
        for p, s_ref in ((0, s1_ref), (1, s2_ref)):
            qp = q_scr[pl.ds(base + p * half, half), :]
            s = _dot(keys_ref[p], qp)
            s_ref[h] = s
            tops.append(_extract_top(s, PEER_TOPK, iota_k))
        a1, a2 = tops
        rows = [a1[i] + a2[j] for (i, j) in _CAND] + [jnp.full((1, tb), NEG_INF, F32)] * npad
        cand = jnp.concatenate(rows, axis=0)
        tau = _extract_top(cand, PEER_TOPK, iota_c)[-1]
        cmax = a1[0] + a2[0]
        zsum = jnp.sum(jnp.where(cand >= tau, jnp.exp(cand - cmax), 0.0), axis=0, keepdims=True)
        st_ref[0, pl.ds(h, 1), :] = tau
        st_ref[1, pl.ds(h, 1), :] = a1[0]
        st_ref[2, pl.ds(h, 1), :] = a2[0]
        st_ref[3, pl.ds(h, 1), :] = zsum
        return carry

    lax.fori_loop(0, PEER_HEADS, head, 0)


def _peer_route(h2, wqt, keys):
    T, D = h2.shape
    tb = PEER_TB
    PH = PEER_HEADS
    return pl.pallas_call(
        _route_kernel,
        grid=(T // tb,),
        in_specs=[pl.BlockSpec((tb, D), lambda i: (i, 0)),
                  pl.BlockSpec((PH * PEER_DKEY, D), lambda i: (0, 0)),
                  pl.BlockSpec((2, N_KEYS, PEER_DKEY // 2), lambda i: (0, 0, 0))],
        out_specs=[pl.BlockSpec((PH, N_KEYS, tb), lambda i: (0, 0, i)),
                   pl.BlockSpec((PH, N_KEYS, tb), lambda i: (0, 0, i)),
                   pl.BlockSpec((4, PH, tb), lambda i: (0, 0, i))],
        out_shape=[jax.ShapeDtypeStruct((PH, N_KEYS, T), F32),
                   jax.ShapeDtypeStruct((PH, N_KEYS, T), F32),
                   jax.ShapeDtypeStruct((4, PH, T), F32)],
        scratch_shapes=[pltpu.VMEM((PH * PEER_DKEY, tb), BF16)],
        compiler_params=_cparams(("parallel",)),
        name="peer_route",
    )(h2, wqt, keys)


EXP_TB = 512
EXP_EB = 512


def _experts_kernel(h_ref, u_ref, vt_ref, s1_ref, s2_ref, st_ref, x1_ref, mod_ref, g2_ref, b2_ref,
                    o_ref, acc_scr, e1_scr, e2_scr):
    e = pl.program_id(1)
    ne = pl.num_programs(1)
    nsub = EXP_EB // N_KEYS

    @pl.when(e == 0)
    def _():
        acc_scr[...] = jnp.zeros_like(acc_scr)
        for h in range(PEER_HEADS):
            m1 = st_ref[1, h:h + 1, :]
            m2 = st_ref[2, h:h + 1, :]
            z = st_ref[3, h:h + 1, :]
            e1_scr[h] = jnp.exp(s1_ref[h] - m1) / z
            e2_scr[h] = jnp.exp(s2_ref[h] - m2)

    at = _dot_nt(u_ref[...], h_ref[...])
    ws = []
    for j in range(nsub):
        i1 = e * nsub + j
        a = at[j * N_KEYS:(j + 1) * N_KEYS, :]
        gate = jnp.zeros_like(a)
        for h in range(PEER_HEADS):
            pair = s1_ref[h, pl.ds(i1, 1), :] + s2_ref[h]
            sel = pair >= st_ref[0, h:h + 1, :]
            gate = gate + jnp.where(sel, e2_scr[h], 0.0) * e1_scr[h, pl.ds(i1, 1), :]
        act = 0.5 * a * (1.0 + lax.erf(a * (2.0 ** -0.5)))
        ws.append((gate * act).astype(BF16))
    w = jnp.concatenate(ws, axis=0)
    acc_scr[...] += _dot(vt_ref[...], w)

    @pl.when(e == ne - 1)
    def _():
        yf = acc_scr[...].T
        gt2 = mod_ref[0, 5:6, :]
        r = ALPHA * x1_ref[...] + gt2 * yf
        o_ref[...] = _layer_norm_rows(r) * g2_ref[...] + b2_ref[...]


def _peer_experts(h2, u_b, vt_b, s1, s2, st, x1, mod3, ln_g, ln_b, S):
    T, D = h2.shape
    tb, eb = EXP_TB, EXP_EB
    PH = PEER_HEADS
    return pl.pallas_call(
        _experts_kernel,
        grid=(T // tb, N_EXPERTS // eb),
        in_specs=[pl.BlockSpec((tb, D), lambda i, e: (i, 0)),
                  pl.BlockSpec((eb, D), lambda i, e: (e, 0)),
                  pl.BlockSpec((D, eb), lambda i, e: (0, e)),
                  pl.BlockSpec((PH, N_KEYS, tb), lambda i, e: (0, 0, i)),
                  pl.BlockSpec((PH, N_KEYS, tb), lambda i, e: (0, 0, i)),
                  pl.BlockSpec((4, PH, tb), lambda i, e: (0, 0, i)),
                  pl.BlockSpec((tb, D), lambda i, e: (i, 0)),
                  pl.BlockSpec((1, 6, D), lambda i, e: ((i * tb) // S, 0, 0)),
                  pl.BlockSpec((1, D), lambda i, e: (0, 0)),
                  pl.BlockSpec((1, D), lambda i, e: (0, 0))],
        out_specs=pl.BlockSpec((tb, D), lambda i, e: (i, 0)),
        out_shape=jax.ShapeDtypeStruct((T, D), F32),
        scratch_shapes=[pltpu.VMEM((D, tb), F32),
                        pltpu.VMEM((PH, N_KEYS, tb), F32),
                        pltpu.VMEM((PH, N_KEYS, tb), F32)],
        compiler_params=_cparams(("parallel", "arbitrary")),
        name="peer_experts",
    )(h2, u_b, vt_b, s1, s2, st, x1, mod3, ln_g, ln_b)


def kernel(x, c, w_ada, b_ada, w_in, b_if, conv_w, conv_b, da_lambda, da_subln_g, ml_norm_g, w_br_attn,
           w_br_mlstm, w_out, ln1_g, ln1_b, peer_wq, peer_keys, peer_u, peer_v, ln2_g, ln2_b):
    B, S, D = x.shape
    T = B * S
    assert D == D_MODEL and S % DA_TQ == 0 and S % 1024 == 0
    l = 0
    lambda_init = 0.8 - 0.6 * math.exp(-0.3 * l)

    mod3 = _modulation(c, w_ada[l], b_ada[l]).reshape(B, 6, D)

    w = w_in[l]
    o_mq = 3 * D
    o_mv = o_mq + 2 * ML_HEADS * ML_DK
    o_mo = o_mv + D
    o_if = o_mo + D
    o_ga = o_if + 2 * ML_HEADS
    o_gm = o_ga + D
    starts = (0, D, 2 * D, o_mq, o_mv, o_mo, o_ga, o_gm)
    w8 = jnp.stack([w[:, s0:s0 + D] for s0 in starts]).astype(BF16)
    w_if = w[:, o_if:o_if + 2 * ML_HEADS]
    wg = jnp.pad(w_if, ((0, 0), (0, 128 - 2 * ML_HEADS))).astype(BF16)
    wgt = w_if.T.astype(BF16)
    bias8 = b_if[l].reshape(2 * ML_HEADS)
    bcol = jnp.pad(bias8, (0, 128 - 2 * ML_HEADS)).reshape(1, 128)
    brow = bias8.reshape(2 * ML_HEADS, 1)

    x2 = x.reshape(T, D)
    z, gcol, grow = _in_proj(x2, mod3, w8, wg, wgt, bcol, brow, S)
    z4 = z.reshape(8, B, S, D)

    ya = _diff_attention(z4, da_lambda[l], da_subln_g[l], B, S, lambda_init)
    ym = _mlstm(z4, gcol.reshape(B, S, 128), grow, conv_w[l], conv_b[l].reshape(1, -1),
                ml_norm_g[l].reshape(1, -1), B, S)

    x1, h2 = _merge(ya.reshape(T, D), ym.reshape(T, D), z, x2, mod3,
                    w_br_attn[l].astype(BF16), w_br_mlstm[l].astype(BF16), w_out[l].astype(BF16),
                    ln1_g[l].reshape(1, D), ln1_b[l].reshape(1, D), S)

    s1, s2, st = _peer_route(h2, peer_wq[l].T.astype(BF16), peer_keys[l].astype(BF16))
    out = _peer_experts(h2, peer_u[l].astype(BF16), peer_v[l].T.astype(BF16), s1, s2, st, x1, mod3,
                        ln2_g[l].reshape(1, D), ln2_b[l].reshape(1, D), S)
    return out.reshape(B, S, D)
```

```python
import functools
import math

import jax
import jax.numpy as jnp
from jax import lax
from jax.experimental import pallas as pl
from jax.experimental.pallas import tpu as pltpu

D_MODEL = 1024
DA_HEADS = 8
DA_DK = 64
DA_DV = 2 * DA_DK
ML_HEADS = 4
ML_DK = 128
ML_DV = 256
ML_CHUNK = 128
CONV_K = 4
PEER_HEADS = 8
PEER_TOPK = 16
N_KEYS = 128
N_EXPERTS = N_KEYS * N_KEYS
PEER_DKEY = 128
DEPTH = 1
ALPHA = (2 * DEPTH) ** 0.25
LN_EPS = 1e-5

F32 = jnp.float32
BF16 = jnp.bfloat16
NEG_INF = float("-inf")

VMEM_LIMIT_BYTES = 56 * 1024 * 1024


def _cparams(sem):
    return pltpu.CompilerParams(dimension_semantics=sem, vmem_limit_bytes=VMEM_LIMIT_BYTES)


def _layer_norm_rows(x):
    mu = jnp.mean(x, axis=-1, keepdims=True)
    xc = x - mu
    var = jnp.mean(xc * xc, axis=-1, keepdims=True)
    return xc * lax.rsqrt(var + LN_EPS)


def _dot(a, b):
    return jnp.dot(a, b, preferred_element_type=F32)


def _dot_nt(a, b):
    return lax.dot_general(a, b, (((1,), (1,)), ((), ())), preferred_element_type=F32)


def _dot_tn(a, b):
    return lax.dot_general(a, b, (((0,), (0,)), ((), ())), preferred_element_type=F32)


def _mod_kernel(c_ref, w_ref, b_ref, o_ref):
    c = c_ref[...]
    a = c * jax.nn.sigmoid(c)
    o_ref[...] = jnp.dot(a, w_ref[...], preferred_element_type=F32,
                         precision=lax.Precision.HIGHEST) + b_ref[...]


def _modulation(c, w_ada, b_ada):
    B, D = c.shape
    N = w_ada.shape[1]
    tn = 1024
    return pl.pallas_call(
        _mod_kernel,
        grid=(N // tn,),
        in_specs=[pl.BlockSpec((B, D), lambda n: (0, 0)),
                  pl.BlockSpec((D, tn), lambda n: (0, n)),
                  pl.BlockSpec((1, tn), lambda n: (0, n))],
        out_specs=pl.BlockSpec((B, tn), lambda n: (0, n)),
        out_shape=jax.ShapeDtypeStruct((B, N), F32),
        compiler_params=_cparams(("arbitrary",)),
        name="modulation",
    )(c, w_ada, b_ada.reshape(1, N))


def _inproj_kernel(x_ref, mod_ref, w_ref, wg_ref, wgt_ref, bcol_ref, brow_ref,
                   z_ref, gcol_ref, grow_ref, h_scr):
    n = pl.program_id(1)

    @pl.when(n == 0)
    def _():
        hn = _layer_norm_rows(x_ref[...])
        sh1 = mod_ref[0, 0:1, :]
        sc1 = mod_ref[0, 1:2, :]
        hb = (hn * (1.0 + sc1) + sh1).astype(BF16)
        h_scr[...] = hb
        gcol_ref[...] = _dot(hb, wg_ref[...]) + bcol_ref[...]
        grow_ref[...] = _dot_nt(wgt_ref[...], hb) + brow_ref[...]

    z_ref[0] = _dot(h_scr[...], w_ref[0]).astype(BF16)


def _in_proj(x2, mod3, w8, wg, wgt, bcol, brow, S):
    T, D = x2.shape
    tm = 1024
    npiece = w8.shape[0]
    return pl.pallas_call(
        _inproj_kernel,
        grid=(T // tm, npiece),
        in_specs=[pl.BlockSpec((tm, D), lambda i, n: (i, 0)),
                  pl.BlockSpec((1, 6, D), lambda i, n: ((i * tm) // S, 0, 0)),
                  pl.BlockSpec((1, D, D), lambda i, n: (n, 0, 0)),
                  pl.BlockSpec((D, 128), lambda i, n: (0, 0)),
                  pl.BlockSpec((8, D), lambda i, n: (0, 0)),
                  pl.BlockSpec((1, 128), lambda i, n: (0, 0)),
                  pl.BlockSpec((8, 1), lambda i, n: (0, 0))],
        out_specs=[pl.BlockSpec((1, tm, D), lambda i, n: (n, i, 0)),
                   pl.BlockSpec((tm, 128), lambda i, n: (i, 0)),
                   pl.BlockSpec((8, tm), lambda i, n: (0, i))],
        out_shape=[jax.ShapeDtypeStruct((npiece, T, D), BF16),
                   jax.ShapeDtypeStruct((T, 128), F32),
                   jax.ShapeDtypeStruct((8, T), F32)],
        scratch_shapes=[pltpu.VMEM((tm, D), BF16)],
        compiler_params=_cparams(("parallel", "arbitrary")),
        name="in_proj",
    )(x2, mod3, w8, wg, wgt, bcol, brow)


DA_TQ = 256


def _diffattn_kernel(lam_ref, g_ref, q_ref, k_ref, v_ref, o_ref, *, S, lambda_init):
    tq = DA_TQ
    lam = lam_ref[...]
    t1 = jnp.sum(lam[0:1] * lam[1:2], axis=-1, keepdims=True)
    t2 = jnp.sum(lam[2:3] * lam[3:4], axis=-1, keepdims=True)
    lam_val = jnp.exp(t1) - jnp.exp(t2) + lambda_init
    first_map = lax.broadcasted_iota(jnp.int32, (1, DA_DV), 1) < DA_DK
    gain = g_ref[...] * (1.0 - lambda_init)
    row = lax.broadcasted_iota(jnp.int32, (tq, tq), 0)
    col = lax.broadcasted_iota(jnp.int32, (tq, tq), 1)
    causal = col <= row

    def q_block(qi, carry):
        q0 = pl.multiple_of(qi * tq, tq)
        qs = q_ref[0, 0, pl.ds(q0, tq), :] * (DA_DK ** -0.5)
        zero = jnp.zeros_like(qs)
        q1 = jnp.where(first_map, qs, zero)
        q2 = jnp.where(first_map, zero, qs)

        def kv_step(k0, state, masked):
            m1, l1, a1, m2, l2, a2 = state
            k = k_ref[0, 0, pl.ds(k0, tq), :]
            v = v_ref[0, 0, pl.ds(k0, tq), :]
            out = []
            for qm, m, l, a in ((q1, m1, l1, a1), (q2, m2, l2, a2)):
                s = _dot_nt(qm, k)
                if masked:
                    s = jnp.where(causal, s, NEG_INF)
                mn = jnp.maximum(m, jnp.max(s, axis=-1, keepdims=True))
                p = jnp.exp(s - mn)
                alpha = jnp.exp(m - mn)
                l = alpha * l + jnp.sum(p, axis=-1, keepdims=True)
                a = alpha * a + _dot(p.astype(BF16), v)
                out += [mn, l, a]
            return tuple(out)

        init = (jnp.full((tq, 1), NEG_INF, F32), jnp.zeros((tq, 1), F32), jnp.zeros((tq, DA_DV), F32),
                jnp.full((tq, 1), NEG_INF, F32), jnp.zeros((tq, 1), F32), jnp.zeros((tq, DA_DV), F32))
        state = kv_step(q0, init, True)
        state = lax.fori_loop(0, qi, lambda kj, st: kv_step(pl.multiple_of(kj * tq, tq), st, False), state)
        m1, l1, a1, m2, l2, a2 = state
        o = a1 / l1 - lam_val * (a2 / l2)
        o = o * lax.rsqrt(jnp.mean(o * o, axis=-1, keepdims=True) + LN_EPS) * gain
        o_ref[0, pl.ds(q0, tq), :] = o.astype(BF16)
        return carry

    lax.fori_loop(0, S // tq, q_block, 0)


def _diff_attention(z4, da_lambda, subln_g, B, S, lambda_init):
    kern = functools.partial(_diffattn_kernel, S=S, lambda_init=lambda_init)
    return pl.pallas_call(
        kern,
        grid=(B, DA_HEADS),
        in_specs=[pl.BlockSpec((4, DA_DK), lambda b, h: (0, 0)),
                  pl.BlockSpec((1, DA_DV), lambda b, h: (0, 0)),
                  pl.BlockSpec((1, 1, S, DA_DV), lambda b, h: (0, b, 0, h)),
                  pl.BlockSpec((1, 1, S, DA_DV), lambda b, h: (1, b, 0, h)),
                  pl.BlockSpec((1, 1, S, DA_DV), lambda b, h: (2, b, 0, h))],
        out_specs=pl.BlockSpec((1, S, DA_DV), lambda b, h: (b, 0, h)),
        out_shape=jax.ShapeDtypeStruct((B, S, DA_HEADS * DA_DV), BF16),
        compiler_params=_cparams(("parallel", "parallel")),
        name="diff_attention",
    )(da_lambda, subln_g.reshape(1, DA_DV), z4, z4, z4)


def _mlstm_kernel(q_ref, k_ref, v_ref, og_ref, gcol_ref, grow_ref, cwq_ref, cwk_ref, cbq_ref, cbk_ref,
                  ng_ref, o_ref, qc_scr, kc_scr, *, S):
    L = ML_CHUNK
    hh = pl.program_id(1)
    srow = lax.broadcasted_iota(jnp.int32, (S, ML_DK), 0)

    def conv_silu(x_ref, w_ref, b_ref):
        x = x_ref[0, 0].astype(F32)
        y = x * w_ref[CONV_K - 1:CONV_K, :] + b_ref[...]
        for j in range(1, CONV_K):
            xs = jnp.where(srow >= j, pltpu.roll(x, j, 0), 0.0)
            y = y + xs * w_ref[CONV_K - 1 - j:CONV_K - j, :]
        return y * jax.nn.sigmoid(y)

    qc_scr[...] = conv_silu(q_ref, cwq_ref, cbq_ref)
    kc_scr[...] = conv_silu(k_ref, cwk_ref, cbk_ref) * (ML_DK ** -0.5)

    r_i = lax.broadcasted_iota(jnp.int32, (L, L), 0)
    c_i = lax.broadcasted_iota(jnp.int32, (L, L), 1)
    causal = c_i <= r_i
    tril = causal.astype(F32)
    triu = (r_i <= c_i).astype(F32)
    lane = lax.broadcasted_iota(jnp.int32, (1, 128), 1)
    sel_i = (lane == hh).astype(F32)
    sel_f = (lane == hh + ML_HEADS).astype(F32)
    sub = lax.broadcasted_iota(jnp.int32, (8, 1), 0)
    rsel_i = (sub == hh).astype(F32)
    rsel_f = (sub == hh + ML_HEADS).astype(F32)
    ngain = ng_ref[...]

    def chunk(ci, state):
        Ct, n_row, m = state
        t0 = pl.multiple_of(ci * L, L)
        q = qc_scr[pl.ds(t0, L), :]
        k = kc_scr[pl.ds(t0, L), :]
        v = v_ref[0, 0, pl.ds(t0, L), :]
        gc = gcol_ref[0, pl.ds(t0, L), :]
        gr = grow_ref[:, pl.ds(t0, L)]
        ig_col = jnp.sum(gc * sel_i, axis=-1, keepdims=True)
        lf_col = jax.nn.log_sigmoid(jnp.sum(gc * sel_f, axis=-1, keepdims=True))
        ig_row = jnp.sum(gr * rsel_i, axis=0, keepdims=True)
        lf_row = jax.nn.log_sigmoid(jnp.sum(gr * rsel_f, axis=0, keepdims=True))
        b_col = jnp.dot(tril, jnp.broadcast_to(lf_col, (L, 128)), preferred_element_type=F32,
                        precision=lax.Precision.HIGHEST)[:, 0:1]
        b_row = jnp.dot(jnp.broadcast_to(lf_row, (8, L)), triu, preferred_element_type=F32,
                        precision=lax.Precision.HIGHEST)[0:1, :]
        dm = jnp.where(causal, b_col - b_row + ig_row, NEG_INF)
        m_inter = b_col + m
        m_t = jnp.maximum(m_inter, jnp.max(dm, axis=-1, keepdims=True))
        w = jnp.exp(dm - m_t)
        qb = q.astype(BF16)
        kb = k.astype(BF16)
        p = w * _dot_nt(qb, kb)
        inter = jnp.exp(m_inter - m_t)
        num = _dot(p.astype(BF16), v) + inter * _dot(qb, Ct.astype(BF16))
        nq = jnp.sum(p, axis=-1, keepdims=True) + inter * jnp.sum(q * n_row, axis=-1, keepdims=True)
        hout = num / jnp.maximum(jnp.abs(nq), jnp.exp(-m_t))
        hout = hout * lax.rsqrt(jnp.mean(hout * hout, axis=-1, keepdims=True) + LN_EPS) * ngain
        og = og_ref[0, 0, pl.ds(t0, L), :].astype(F32)
        o_ref[0, pl.ds(t0, L), :] = (hout * jax.nn.sigmoid(og)).astype(BF16)
        m_new = m_t[L - 1:L, :]
        b_last = b_col[L - 1:L, :]
        decay = jnp.exp(b_last + m - m_new)
        w_s = jnp.exp(b_last - b_col + ig_col - m_new)
        Ct = decay * Ct + _dot_tn(kb, (v.astype(F32) * w_s).astype(BF16))
        n_row = decay * n_row + jnp.sum(k * w_s, axis=0, keepdims=True)
        return Ct, n_row, m_new

    init = (jnp.zeros((ML_DK, ML_DV), F32), jnp.zeros((1, ML_DK), F32), jnp.zeros((1, 1), F32))
    lax.fori_loop(0, S // L, chunk, init)


def _mlstm(z4, gcol3, grow, conv_w, conv_b, norm_g, B, S):
    kern = functools.partial(_mlstm_kernel, S=S)
    H = ML_HEADS
    return pl.pallas_call(
        kern,
        grid=(B, H),
        in_specs=[pl.BlockSpec((1, 1, S, ML_DK), lambda b, h: (3, b, 0, h)),
                  pl.BlockSpec((1, 1, S, ML_DK), lambda b, h: (3, b, 0, H + h)),
                  pl.BlockSpec((1, 1, S, ML_DV), lambda b, h: (4, b, 0, h)),
                  pl.BlockSpec((1, 1, S, ML_DV), lambda b, h: (5, b, 0, h)),
                  pl.BlockSpec((1, S, 128), lambda b, h: (b, 0, 0)),
                  pl.BlockSpec((8, S), lambda b, h: (0, b)),
                  pl.BlockSpec((CONV_K, ML_DK), lambda b, h: (0, h)),
                  pl.BlockSpec((CONV_K, ML_DK), lambda b, h: (0, H + h)),
                  pl.BlockSpec((1, ML_DK), lambda b, h: (0, h)),
                  pl.BlockSpec((1, ML_DK), lambda b, h: (0, H + h)),
                  pl.BlockSpec((1, ML_DV), lambda b, h: (0, h))],
        out_specs=pl.BlockSpec((1, S, ML_DV), lambda b, h: (b, 0, h)),
        out_shape=jax.ShapeDtypeStruct((B, S, H * ML_DV), BF16),
        scratch_shapes=[pltpu.VMEM((S, ML_DK), F32), pltpu.VMEM((S, ML_DK), F32)],
        compiler_params=_cparams(("parallel", "parallel")),
        name="mlstm",
    )(z4, z4, z4, z4, gcol3, grow, conv_w, conv_w, conv_b, conv_b, norm_g)


def _merge_kernel(ya_ref, ym_ref, ga_ref, gm_ref, x_ref, mod_ref, wa_ref, wm_ref, wo_ref, g1_ref, b1_ref,
                  x1_ref, h2_ref):
    ya = _dot(ya_ref[...], wa_ref[...])
    ym = _dot(ym_ref[...], wm_ref[...])
    y = (jax.nn.sigmoid(ga_ref[0].astype(F32)) * ya + jax.nn.sigmoid(gm_ref[0].astype(F32)) * ym)
    y2 = _dot(y.astype(BF16), wo_ref[...])
    gt1 = mod_ref[0, 2:3, :]
    sh2 = mod_ref[0, 3:4, :]
    sc2 = mod_ref[0, 4:5, :]
    x1 = _layer_norm_rows(ALPHA * x_ref[...] + gt1 * y2) * g1_ref[...] + b1_ref[...]
    x1_ref[...] = x1
    h2_ref[...] = (_layer_norm_rows(x1) * (1.0 + sc2) + sh2).astype(BF16)


def _merge(ya2, ym2, z3, x2, mod3, wa, wm, wo, ln_g, ln_b, S):
    T, D = x2.shape
    tm = 512
    tok = lambda i: (i, 0)
    const = lambda i: (0, 0)
    return pl.pallas_call(
        _merge_kernel,
        grid=(T // tm,),
        in_specs=[pl.BlockSpec((tm, D), tok), pl.BlockSpec((tm, D), tok),
                  pl.BlockSpec((1, tm, D), lambda i: (6, i, 0)),
                  pl.BlockSpec((1, tm, D), lambda i: (7, i, 0)),
                  pl.BlockSpec((tm, D), tok),
                  pl.BlockSpec((1, 6, D), lambda i: ((i * tm) // S, 0, 0)),
                  pl.BlockSpec((D, D), const), pl.BlockSpec((D, D), const), pl.BlockSpec((D, D), const),
                  pl.BlockSpec((1, D), const), pl.BlockSpec((1, D), const)],
        out_specs=[pl.BlockSpec((tm, D), tok), pl.BlockSpec((tm, D), tok)],
        out_shape=[jax.ShapeDtypeStruct((T, D), F32), jax.ShapeDtypeStruct((T, D), BF16)],
        compiler_params=_cparams(("parallel",)),
        name="merge",
    )(ya2, ym2, z3, z3, x2, mod3, wa, wm, wo, ln_g, ln_b)


PEER_TB = 256
_CAND_COLS = [[k1 for k1 in range(PEER_TOPK) if (k1 + 1) * (k2 + 1) <= PEER_TOPK] for k2 in range(PEER_TOPK)]


def _batcher_pairs(n):
    pairs = []
    p = 1
    while p < n:
        k = p
        while k >= 1:
            for j in range(k % p, n - k, 2 * k):
                for i in range(min(k, n - j - k)):
                    if (i + j) // (2 * p) == (i + j + k) // (2 * p):
                        pairs.append((i + j, i + j + k))
            k //= 2
        p *= 2
    return pairs


_SORT16 = _batcher_pairs(PEER_TOPK)


def _cmpx(v, i, j):
    a, b = v[i], v[j]
    if b is None:
        return
    if a is None:
        v[i], v[j] = b, None
        return
    v[i], v[j] = jnp.maximum(a, b), jnp.minimum(a, b)


def _sort16_desc(v):
    v = list(v)
    for i, j in _SORT16:
        _cmpx(v, i, j)
    return v


def _bitonic_merge_desc(v):
    v = list(v)
    d = PEER_TOPK // 2
    while d >= 1:
        for i in range(PEER_TOPK):
            if i & d == 0:
                _cmpx(v, i, i + d)
        d //= 2
    return v


def _half_clean(a, b):
    out = []
    for g in range(PEER_TOPK):
        x, y = a[g], b[PEER_TOPK - 1 - g]
        out.append(x if y is None else (y if x is None else jnp.maximum(x, y)))
    return out


def _top16_all_sublanes(s):
    rows = _sort16_desc([s[g * 8:(g + 1) * 8, :] for g in range(PEER_TOPK)])
    for shift in (4, 2, 1):
        partner = [pltpu.roll(r, shift, 0) for r in rows]
        rows = _bitonic_merge_desc(_half_clean(rows, partner))
    return rows


def _route_kernel(h_ref, wqt_ref, keys_ref, rk2_ref, e2_ref, c1_ref, e1_ref, q_scr, s_scr, a_scr, f_scr):
    half = PEER_DKEY // 2
    K = PEER_TOPK
    q_scr[...] = _dot_nt(wqt_ref[...], h_ref[...]).astype(BF16)

    def stage_a(h, carry):
        base = pl.multiple_of(h * PEER_DKEY, PEER_DKEY)
        for p in range(2):
            qp = q_scr[pl.ds(base + p * half, half), :]
            s = _dot(keys_ref[p], qp)
            s_scr[p, h] = s
            top = _top16_all_sublanes(s)
            for k in range(K):
                a_scr[p, k, pl.ds(h, 1), :] = top[k][0:1, :]
        return carry

    lax.fori_loop(0, PEER_HEADS, stage_a, 0)

    a1 = [a_scr[0, k] for k in range(K)]
    a2 = [a_scr[1, k] for k in range(K)]
    cand = [[a1[k1] + a2[k2] for k1 in col] for k2, col in enumerate(_CAND_COLS)]
    g0 = [cand[k2][0] for k2 in range(K)]
    rest = [cand[k2][i] for k2 in range(K) for i in range(1, len(_CAND_COLS[k2]))]
    rest += [None] * (-len(rest) % K)
    groups = [g0] + [_sort16_desc(rest[i:i + K]) for i in range(0, len(rest), K)]
    while len(groups) > 2:
        merged = [_bitonic_merge_desc(_half_clean(groups[i], groups[i + 1])) for i in range(0, len(groups) - 1, 2)]
        groups = merged + ([groups[-1]] if len(groups) % 2 else [])
    last = [x for x in _half_clean(groups[0], groups[1]) if x is not None]
    tau = functools.reduce(jnp.minimum, last)
    cmax = a1[0] + a2[0]
    zsum = None
    for k2, col in enumerate(_CAND_COLS):
        phi = None
        for i, k1 in enumerate(col):
            c = cand[k2][i]
            hit = c >= tau
            term = jnp.where(hit, jnp.exp(c - cmax), 0.0)
            zsum = term if zsum is None else zsum + term
            lo = jnp.where(hit, a1[k1], jnp.inf)
            phi = lo if phi is None else jnp.minimum(phi, lo)
        f_scr[k2] = phi
    f_scr[K] = 1.0 / zsum

    def stage_c(h, carry):
        s1 = s_scr[0, h]
        s2 = s_scr[1, h]
        rank = jnp.zeros_like(s2)
        cnt = jnp.zeros_like(s1)
        for k in range(K):
            rank = rank + jnp.where(a_scr[1, k, pl.ds(h, 1), :] > s2, 1.0, 0.0)
            cnt = cnt + jnp.where(s1 >= f_scr[k, pl.ds(h, 1), :], 1.0, 0.0)
        rk2_ref[h] = rank.astype(BF16)
        c1_ref[h] = cnt
        e2_ref[h] = jnp.exp(s2 - a_scr[1, 0, pl.ds(h, 1), :]).astype(BF16)
        e1_ref[h] = jnp.exp(s1 - a_scr[0, 0, pl.ds(h, 1), :]) * f_scr[K, pl.ds(h, 1), :]
        return carry

    lax.fori_loop(0, PEER_HEADS, stage_c, 0)


def _peer_route(h2, wqt, keys):
    T, D = h2.shape
    tb = PEER_TB
    PH = PEER_HEADS
    blk = pl.BlockSpec((PH, N_KEYS, tb), lambda i: (0, 0, i))
    return pl.pallas_call(
        _route_kernel,
        grid=(T // tb,),
        in_specs=[pl.BlockSpec((tb, D), lambda i: (i, 0)),
                  pl.BlockSpec((PH * PEER_DKEY, D), lambda i: (0, 0)),
                  pl.BlockSpec((2, N_KEYS, PEER_DKEY // 2), lambda i: (0, 0, 0))],
        out_specs=[blk, blk, blk, blk],
        out_shape=[jax.ShapeDtypeStruct((PH, N_KEYS, T), BF16),
                   jax.ShapeDtypeStruct((PH, N_KEYS, T), BF16),
                   jax.ShapeDtypeStruct((PH, N_KEYS, T), F32),
                   jax.ShapeDtypeStruct((PH, N_KEYS, T), F32)],
        scratch_shapes=[pltpu.VMEM((PH * PEER_DKEY, tb), BF16),
                        pltpu.VMEM((2, PH, N_KEYS, tb), F32),
                        pltpu.VMEM((2, PEER_TOPK, PH, tb), F32),
                        pltpu.VMEM((PEER_TOPK + 1, PH, tb), F32)],
        compiler_params=_cparams(("parallel",)),
        name="peer_route",
    )(h2, wqt, keys)


EXP_TB = 512
EXP_EB = 512


BF16_ROWS = 16
LANES = 256


def _experts_kernel(h_ref, u_ref, vt_ref, rk2_ref, e2_ref, c1_ref, e1_ref, x1_ref, mod_ref, g2_ref, b2_ref,
                    o_ref, acc_scr, at_scr, w_scr):
    e = pl.program_id(1)
    ne = pl.num_programs(1) - 1
    nsub = EXP_EB // N_KEYS
    ngrp = N_KEYS // BF16_ROWS
    tb = h_ref.shape[0]
    slot = lax.rem(e, 2)

    @pl.when(e == 0)
    def _():
        acc_scr[...] = jnp.zeros_like(acc_scr)
        w_scr[1] = jnp.zeros((EXP_EB, tb), BF16)

    at_scr[...] = _dot_nt(u_ref[...], h_ref[...])
    acc_scr[...] += _dot(vt_ref[...], w_scr[1 - slot])
    for j in range(nsub):
        i1 = jnp.minimum(e * nsub + j, N_KEYS - 1)
        cnt_rows = [c1_ref[h, pl.ds(i1, 1), :] for h in range(PEER_HEADS)]
        e1_rows = [e1_ref[h, pl.ds(i1, 1), :] for h in range(PEER_HEADS)]
        for lt in range(tb // LANES):
            cols = slice(lt * LANES, (lt + 1) * LANES)
            gates = [None] * ngrp
            for h in range(PEER_HEADS):
                cnt = jnp.broadcast_to(cnt_rows[h][:, cols], (BF16_ROWS, LANES)).astype(BF16)
                e1 = jnp.broadcast_to(e1_rows[h][:, cols], (BF16_ROWS, LANES)).astype(BF16)
                for r in range(ngrp):
                    rows = slice(r * BF16_ROWS, (r + 1) * BF16_ROWS)
                    term = jnp.where(rk2_ref[h, rows, cols] < cnt, e2_ref[h, rows, cols],
                                     jnp.zeros((), BF16)) * e1
                    gates[r] = term if gates[r] is None else gates[r] + term
            for r in range(ngrp):
                rows = slice(j * N_KEYS + r * BF16_ROWS, j * N_KEYS + (r + 1) * BF16_ROWS)
                a = at_scr[rows, cols]
                act = (0.5 * a * (1.0 + lax.erf(a * (2.0 ** -0.5)))).astype(BF16)
                w_scr[slot, rows, cols] = gates[r] * act

    @pl.when(e == ne)
    def _():
        yf = acc_scr[...].T
        gt2 = mod_ref[0, 5:6, :]
        r = ALPHA * x1_ref[...] + gt2 * yf
        o_ref[...] = _layer_norm_rows(r) * g2_ref[...] + b2_ref[...]


def _peer_experts(h2, u_b, vt_b, rk2, e2, c1, e1, x1, mod3, ln_g, ln_b, S):
    T, D = h2.shape
    tb, eb = EXP_TB, EXP_EB
    PH = PEER_HEADS
    route = pl.BlockSpec((PH, N_KEYS, tb), lambda i, e: (0, 0, i))
    ne = N_EXPERTS // eb
    return pl.pallas_call(
        _experts_kernel,
        grid=(T // tb, ne + 1),
        in_specs=[pl.BlockSpec((tb, D), lambda i, e: (i, 0)),
                  pl.BlockSpec((eb, D), lambda i, e: (jnp.minimum(e, ne - 1), 0)),
                  pl.BlockSpec((D, eb), lambda i, e: (0, jnp.maximum(e - 1, 0))),
                  route, route, route, route,
                  pl.BlockSpec((tb, D), lambda i, e: (i, 0)),
                  pl.BlockSpec((1, 6, D), lambda i, e: ((i * tb) // S, 0, 0)),
                  pl.BlockSpec((1, D), lambda i, e: (0, 0)),
                  pl.BlockSpec((1, D), lambda i, e: (0, 0))],
        out_specs=pl.BlockSpec((tb, D), lambda i, e: (i, 0)),
        out_shape=jax.ShapeDtypeStruct((T, D), F32),
        scratch_shapes=[pltpu.VMEM((D, tb), F32), pltpu.VMEM((eb, tb), F32), pltpu.VMEM((2, eb, tb), BF16)],
        compiler_params=_cparams(("parallel", "arbitrary")),
        name="peer_experts",
    )(h2, u_b, vt_b, rk2, e2, c1, e1, x1, mod3, ln_g, ln_b)


def kernel(x, c, w_ada, b_ada, w_in, b_if, conv_w, conv_b, da_lambda, da_subln_g, ml_norm_g, w_br_attn,
           w_br_mlstm, w_out, ln1_g, ln1_b, peer_wq, peer_keys, peer_u, peer_v, ln2_g, ln2_b):
    B, S, D = x.shape
    T = B * S
    assert D == D_MODEL and S % DA_TQ == 0 and S % 1024 == 0
    l = 0
    lambda_init = 0.8 - 0.6 * math.exp(-0.3 * l)

    mod3 = _modulation(c, w_ada[l], b_ada[l]).reshape(B, 6, D)

    w = w_in[l]
    o_mq = 3 * D
    o_mv = o_mq + 2 * ML_HEADS * ML_DK
    o_mo = o_mv + D
    o_if = o_mo + D
    o_ga = o_if + 2 * ML_HEADS
    o_gm = o_ga + D
    starts = (0, D, 2 * D, o_mq, o_mv, o_mo, o_ga, o_gm)
    w8 = jnp.stack([w[:, s0:s0 + D] for s0 in starts]).astype(BF16)
    w_if = w[:, o_if:o_if + 2 * ML_HEADS]
    wg = jnp.pad(w_if, ((0, 0), (0, 128 - 2 * ML_HEADS))).astype(BF16)
    wgt = w_if.T.astype(BF16)
    bias8 = b_if[l].reshape(2 * ML_HEADS)
    bcol = jnp.pad(bias8, (0, 128 - 2 * ML_HEADS)).reshape(1, 128)
    brow = bias8.reshape(2 * ML_HEADS, 1)

    x2 = x.reshape(T, D)
    z, gcol, grow = _in_proj(x2, mod3, w8, wg, wgt, bcol, brow, S)
    z4 = z.reshape(8, B, S, D)

    ya = _diff_attention(z4, da_lambda[l], da_subln_g[l], B, S, lambda_init)
    ym = _mlstm(z4, gcol.reshape(B, S, 128), grow, conv_w[l], conv_b[l].reshape(1, -1),
                ml_norm_g[l].reshape(1, -1), B, S)

    x1, h2 = _merge(ya.reshape(T, D), ym.reshape(T, D), z, x2, mod3,
                    w_br_attn[l].astype(BF16), w_br_mlstm[l].astype(BF16), w_out[l].astype(BF16),
                    ln1_g[l].reshape(1, D), ln1_b[l].reshape(1, D), S)

    rk2, e2, c1, e1 = _peer_route(h2, peer_wq[l].T.astype(BF16), peer_keys[l].astype(BF16))
    out = _peer_experts(h2, peer_u[l].astype(BF16), peer_v[l].T.astype(BF16), rk2, e2, c1, e1, x1, mod3,
                        ln2_g[l].reshape(1, D), ln2_b[l].reshape(1, D), S)
    return out.reshape(B, S, D)
```

```python
import functools
import math

import jax
import jax.numpy as jnp
from jax import lax
from jax.experimental import pallas as pl
from jax.experimental.pallas import tpu as pltpu

D_MODEL = 1024
DA_HEADS = 8
DA_DK = 64
DA_DV = 2 * DA_DK
ML_HEADS = 4
ML_DK = 128
ML_DV = 256
ML_CHUNK = 128
CONV_K = 4
PEER_HEADS = 8
PEER_TOPK = 16
N_KEYS = 128
N_EXPERTS = N_KEYS * N_KEYS
PEER_DKEY = 128
DEPTH = 1
ALPHA = (2 * DEPTH) ** 0.25
LN_EPS = 1e-5

F32 = jnp.float32
BF16 = jnp.bfloat16
NEG_INF = float("-inf")

VMEM_LIMIT_BYTES = 56 * 1024 * 1024


def _cparams(sem):
    return pltpu.CompilerParams(dimension_semantics=sem, vmem_limit_bytes=VMEM_LIMIT_BYTES)


def _layer_norm_rows(x):
    mu = jnp.mean(x, axis=-1, keepdims=True)
    xc = x - mu
    var = jnp.mean(xc * xc, axis=-1, keepdims=True)
    return xc * lax.rsqrt(var + LN_EPS)


def _dot(a, b):
    return jnp.dot(a, b, preferred_element_type=F32)


def _dot_nt(a, b):
    return lax.dot_general(a, b, (((1,), (1,)), ((), ())), preferred_element_type=F32)


def _dot_tn(a, b):
    return lax.dot_general(a, b, (((0,), (0,)), ((), ())), preferred_element_type=F32)


def _mod_kernel(c_ref, w_ref, b_ref, o_ref):
    c = c_ref[...]
    a = c * jax.nn.sigmoid(c)
    o_ref[...] = jnp.dot(a, w_ref[...], preferred_element_type=F32,
                         precision=lax.Precision.HIGHEST) + b_ref[...]


def _modulation(c, w_ada, b_ada):
    B, D = c.shape
    N = w_ada.shape[1]
    tn = 1024
    return pl.pallas_call(
        _mod_kernel,
        grid=(N // tn,),
        in_specs=[pl.BlockSpec((B, D), lambda n: (0, 0)),
                  pl.BlockSpec((D, tn), lambda n: (0, n)),
                  pl.BlockSpec((1, tn), lambda n: (0, n))],
        out_specs=pl.BlockSpec((B, tn), lambda n: (0, n)),
        out_shape=jax.ShapeDtypeStruct((B, N), F32),
        compiler_params=_cparams(("arbitrary",)),
        name="modulation",
    )(c, w_ada, b_ada.reshape(1, N))


def _inproj_kernel(x_ref, mod_ref, w_ref, wg_ref, wgt_ref, bcol_ref, brow_ref,
                   z_ref, gcol_ref, grow_ref, h_scr):
    n = pl.program_id(1)

    @pl.when(n == 0)
    def _():
        hn = _layer_norm_rows(x_ref[...])
        sh1 = mod_ref[0, 0:1, :]
        sc1 = mod_ref[0, 1:2, :]
        hb = (hn * (1.0 + sc1) + sh1).astype(BF16)
        h_scr[...] = hb
        gcol_ref[...] = _dot(hb, wg_ref[...]) + bcol_ref[...]
        grow_ref[...] = _dot_nt(wgt_ref[...], hb) + brow_ref[...]

    z_ref[0] = _dot(h_scr[...], w_ref[0]).astype(BF16)


def _in_proj(x2, mod3, w8, wg, wgt, bcol, brow, S):
    T, D = x2.shape
    tm = 1024
    npiece = w8.shape[0]
    return pl.pallas_call(
        _inproj_kernel,
        grid=(T // tm, npiece),
        in_specs=[pl.BlockSpec((tm, D), lambda i, n: (i, 0)),
                  pl.BlockSpec((1, 6, D), lambda i, n: ((i * tm) // S, 0, 0)),
                  pl.BlockSpec((1, D, D), lambda i, n: (n, 0, 0)),
                  pl.BlockSpec((D, 128), lambda i, n: (0, 0)),
                  pl.BlockSpec((8, D), lambda i, n: (0, 0)),
                  pl.BlockSpec((1, 128), lambda i, n: (0, 0)),
                  pl.BlockSpec((8, 1), lambda i, n: (0, 0))],
        out_specs=[pl.BlockSpec((1, tm, D), lambda i, n: (n, i, 0)),
                   pl.BlockSpec((tm, 128), lambda i, n: (i, 0)),
                   pl.BlockSpec((8, tm), lambda i, n: (0, i))],
        out_shape=[jax.ShapeDtypeStruct((npiece, T, D), BF16),
                   jax.ShapeDtypeStruct((T, 128), F32),
                   jax.ShapeDtypeStruct((8, T), F32)],
        scratch_shapes=[pltpu.VMEM((tm, D), BF16)],
        compiler_params=_cparams(("parallel", "arbitrary")),
        name="in_proj",
    )(x2, mod3, w8, wg, wgt, bcol, brow)


DA_TQ = 256


def _diffattn_kernel(lam_ref, g_ref, q_ref, k_ref, v_ref, o_ref, *, S, lambda_init):
    tq = DA_TQ
    lam = lam_ref[...]
    t1 = jnp.sum(lam[0:1] * lam[1:2], axis=-1, keepdims=True)
    t2 = jnp.sum(lam[2:3] * lam[3:4], axis=-1, keepdims=True)
    lam_val = jnp.exp(t1) - jnp.exp(t2) + lambda_init
    first_map = lax.broadcasted_iota(jnp.int32, (1, DA_DV), 1) < DA_DK
    gain = g_ref[...] * (1.0 - lambda_init)
    row = lax.broadcasted_iota(jnp.int32, (tq, tq), 0)
    col = lax.broadcasted_iota(jnp.int32, (tq, tq), 1)
    causal = col <= row

    for qi in range(S // tq):
        q0 = qi * tq
        qs = q_ref[0, 0, q0:q0 + tq, :] * (DA_DK ** -0.5)
        zero = jnp.zeros_like(qs)
        maps = (jnp.where(first_map, qs, zero), jnp.where(first_map, zero, qs))
        k_diag = k_ref[0, 0, q0:q0 + tq, :]
        v_diag = v_ref[0, 0, q0:q0 + tq, :]
        outs = []
        for qm in maps:
            s_diag = jnp.where(causal, _dot_nt(qm, k_diag), NEG_INF)
            m = jnp.max(s_diag, axis=-1, keepdims=True)
            if qi > 0:
                s_off = _dot_nt(qm, k_ref[0, 0, 0:q0, :])
                m = jnp.maximum(m, jnp.max(s_off, axis=-1, keepdims=True))
            p_diag = jnp.exp(s_diag - m)
            l = jnp.sum(p_diag, axis=-1, keepdims=True)
            acc = _dot(p_diag.astype(BF16), v_diag)
            if qi > 0:
                p_off = jnp.exp(s_off - m)
                l = l + jnp.sum(p_off, axis=-1, keepdims=True)
                acc = acc + _dot(p_off.astype(BF16), v_ref[0, 0, 0:q0, :])
            outs.append(acc / l)
        o = outs[0] - lam_val * outs[1]
        o = o * lax.rsqrt(jnp.mean(o * o, axis=-1, keepdims=True) + LN_EPS) * gain
        o_ref[0, q0:q0 + tq, :] = o.astype(BF16)


def _diff_attention(z4, da_lambda, subln_g, B, S, lambda_init):
    kern = functools.partial(_diffattn_kernel, S=S, lambda_init=lambda_init)
    return pl.pallas_call(
        kern,
        grid=(B, DA_HEADS),
        in_specs=[pl.BlockSpec((4, DA_DK), lambda b, h: (0, 0)),
                  pl.BlockSpec((1, DA_DV), lambda b, h: (0, 0)),
                  pl.BlockSpec((1, 1, S, DA_DV), lambda b, h: (0, b, 0, h)),
                  pl.BlockSpec((1, 1, S, DA_DV), lambda b, h: (1, b, 0, h)),
                  pl.BlockSpec((1, 1, S, DA_DV), lambda b, h: (2, b, 0, h))],
        out_specs=pl.BlockSpec((1, S, DA_DV), lambda b, h: (b, 0, h)),
        out_shape=jax.ShapeDtypeStruct((B, S, DA_HEADS * DA_DV), BF16),
        compiler_params=_cparams(("parallel", "parallel")),
        name="diff_attention",
    )(da_lambda, subln_g.reshape(1, DA_DV), z4, z4, z4)


def _mlstm_kernel(qk_ref, v_ref, og_ref, gcol_ref, grow_ref, cw_ref, cb_ref, ng_ref, o_ref,
                  qc_scr, kc_scr, c_scr, n_scr, m_scr, *, S):
    L = ML_CHUNK
    H = ML_HEADS
    srow = lax.broadcasted_iota(jnp.int32, (S, ML_DK), 0)

    for cb in range(2 * H):
        cols = slice(cb * ML_DK, (cb + 1) * ML_DK)
        x = qk_ref[0, 0, :, cols].astype(F32)
        y = x * cw_ref[CONV_K - 1:CONV_K, cols] + cb_ref[:, cols]
        for j in range(1, CONV_K):
            xs = jnp.where(srow >= j, pltpu.roll(x, j, 0), 0.0)
            y = y + xs * cw_ref[CONV_K - 1 - j:CONV_K - j, cols]
        y = y * jax.nn.sigmoid(y)
        if cb < H:
            qc_scr[:, cols] = y
        else:
            kc_scr[:, (cb - H) * ML_DK:(cb - H + 1) * ML_DK] = y * (ML_DK ** -0.5)

    r_i = lax.broadcasted_iota(jnp.int32, (L, L), 0)
    c_i = lax.broadcasted_iota(jnp.int32, (L, L), 1)
    causal = c_i <= r_i
    tril = causal.astype(F32)
    triu = (r_i <= c_i).astype(F32)
    c_scr[...] = jnp.zeros_like(c_scr)
    n_scr[...] = jnp.zeros_like(n_scr)
    m_scr[...] = jnp.zeros_like(m_scr)

    def head_chunk(hh, t0, gc, gr, b_cols, b_rows):
        Ct = c_scr[hh]
        n_row = n_scr[hh]
        m = m_scr[hh]
        q = qc_scr[pl.ds(t0, L), hh * ML_DK:(hh + 1) * ML_DK]
        k = kc_scr[pl.ds(t0, L), hh * ML_DK:(hh + 1) * ML_DK]
        v = v_ref[0, 0, pl.ds(t0, L), hh * ML_DV:(hh + 1) * ML_DV]
        ngain = ng_ref[:, hh * ML_DV:(hh + 1) * ML_DV]
        ig_col = gc[:, hh:hh + 1]
        ig_row = gr[hh:hh + 1, :]
        b_col = b_cols[:, H + hh:H + hh + 1]
        b_row = b_rows[H + hh:H + hh + 1, :]
        dm = jnp.where(causal, b_col - b_row + ig_row, NEG_INF)
        m_inter = b_col + m
        m_t = jnp.maximum(m_inter, jnp.max(dm, axis=-1, keepdims=True))
        w = jnp.exp(dm - m_t)
        qb = q.astype(BF16)
        kb = k.astype(BF16)
        p = w * _dot_nt(qb, kb)
        inter = jnp.exp(m_inter - m_t)
        num = _dot(p.astype(BF16), v) + inter * _dot(qb, Ct.astype(BF16))
        nq = jnp.sum(p, axis=-1, keepdims=True) + inter * jnp.sum(q * n_row, axis=-1, keepdims=True)
        hout = num / jnp.maximum(jnp.abs(nq), jnp.exp(-m_t))
        hout = hout * lax.rsqrt(jnp.mean(hout * hout, axis=-1, keepdims=True) + LN_EPS) * ngain
        og = og_ref[0, 0, pl.ds(t0, L), hh * ML_DV:(hh + 1) * ML_DV].astype(F32)
        o_ref[0, pl.ds(t0, L), hh * ML_DV:(hh + 1) * ML_DV] = (hout * jax.nn.sigmoid(og)).astype(BF16)
        m_new = m_t[L - 1:L, :]
        b_last = b_col[L - 1:L, :]
        decay = jnp.exp(b_last + m - m_new)
        w_s = jnp.exp(b_last - b_col + ig_col - m_new)
        c_scr[hh] = decay * Ct + _dot_tn(kb, (v.astype(F32) * w_s).astype(BF16))
        n_scr[hh] = decay * n_row + jnp.sum(k * w_s, axis=0, keepdims=True)
        m_scr[hh] = m_new

    def chunk(ci, carry):
        t0 = pl.multiple_of(ci * L, L)
        gc = gcol_ref[0, pl.ds(t0, L), :]
        gr = grow_ref[:, pl.ds(t0, L)]
        b_cols = jnp.dot(tril, jax.nn.log_sigmoid(gc), preferred_element_type=F32,
                         precision=lax.Precision.HIGHEST)
        b_rows = jnp.dot(jax.nn.log_sigmoid(gr), triu, preferred_element_type=F32,
                         precision=lax.Precision.HIGHEST)
        for hh in range(H):
            head_chunk(hh, t0, gc, gr, b_cols, b_rows)
        return carry

    lax.fori_loop(0, S // L, chunk, 0)


def _mlstm(z4, gcol3, grow, conv_w, conv_b, norm_g, B, S):
    kern = functools.partial(_mlstm_kernel, S=S)
    H = ML_HEADS
    D = H * ML_DV
    piece = lambda n: pl.BlockSpec((1, 1, S, D), lambda b: (n, b, 0, 0))
    return pl.pallas_call(
        kern,
        grid=(B,),
        in_specs=[piece(3), piece(4), piece(5),
                  pl.BlockSpec((1, S, 128), lambda b: (b, 0, 0)),
                  pl.BlockSpec((8, S), lambda b: (0, b)),
                  pl.BlockSpec((CONV_K, 2 * H * ML_DK), lambda b: (0, 0)),
                  pl.BlockSpec((1, 2 * H * ML_DK), lambda b: (0, 0)),
                  pl.BlockSpec((1, D), lambda b: (0, 0))],
        out_specs=pl.BlockSpec((1, S, D), lambda b: (b, 0, 0)),
        out_shape=jax.ShapeDtypeStruct((B, S, D), BF16),
        scratch_shapes=[pltpu.VMEM((S, H * ML_DK), F32), pltpu.VMEM((S, H * ML_DK), F32),
                        pltpu.VMEM((H, ML_DK, ML_DV), F32), pltpu.VMEM((H, 1, ML_DK), F32),
                        pltpu.VMEM((H, 1, 1), F32)],
        compiler_params=_cparams(("parallel",)),
        name="mlstm",
    )(z4, z4, z4, gcol3, grow, conv_w, conv_b, norm_g)


def _merge_kernel(ya_ref, ym_ref, ga_ref, gm_ref, x_ref, mod_ref, wa_ref, wm_ref, wo_ref, g1_ref, b1_ref,
                  x1_ref, h2_ref):
    ya = _dot(ya_ref[...], wa_ref[...])
    ym = _dot(ym_ref[...], wm_ref[...])
    y = (jax.nn.sigmoid(ga_ref[0].astype(F32)) * ya + jax.nn.sigmoid(gm_ref[0].astype(F32)) * ym)
    y2 = _dot(y.astype(BF16), wo_ref[...])
    gt1 = mod_ref[0, 2:3, :]
    sh2 = mod_ref[0, 3:4, :]
    sc2 = mod_ref[0, 4:5, :]
    x1 = _layer_norm_rows(ALPHA * x_ref[...] + gt1 * y2) * g1_ref[...] + b1_ref[...]
    x1_ref[...] = x1
    h2_ref[...] = (_layer_norm_rows(x1) * (1.0 + sc2) + sh2).astype(BF16)


def _merge(ya2, ym2, z3, x2, mod3, wa, wm, wo, ln_g, ln_b, S):
    T, D = x2.shape
    tm = 512
    tok = lambda i: (i, 0)
    const = lambda i: (0, 0)
    return pl.pallas_call(
        _merge_kernel,
        grid=(T // tm,),
        in_specs=[pl.BlockSpec((tm, D), tok), pl.BlockSpec((tm, D), tok),
                  pl.BlockSpec((1, tm, D), lambda i: (6, i, 0)),
                  pl.BlockSpec((1, tm, D), lambda i: (7, i, 0)),
                  pl.BlockSpec((tm, D), tok),
                  pl.BlockSpec((1, 6, D), lambda i: ((i * tm) // S, 0, 0)),
                  pl.BlockSpec((D, D), const), pl.BlockSpec((D, D), const), pl.BlockSpec((D, D), const),
                  pl.BlockSpec((1, D), const), pl.BlockSpec((1, D), const)],
        out_specs=[pl.BlockSpec((tm, D), tok), pl.BlockSpec((tm, D), tok)],
        out_shape=[jax.ShapeDtypeStruct((T, D), F32), jax.ShapeDtypeStruct((T, D), BF16)],
        compiler_params=_cparams(("parallel",)),
        name="merge",
    )(ya2, ym2, z3, z3, x2, mod3, wa, wm, wo, ln_g, ln_b)


PEER_TB = 256
_CAND_COLS = [[k1 for k1 in range(PEER_TOPK) if (k1 + 1) * (k2 + 1) <= PEER_TOPK] for k2 in range(PEER_TOPK)]


def _batcher_pairs(n):
    pairs = []
    p = 1
    while p < n:
        k = p
        while k >= 1:
            for j in range(k % p, n - k, 2 * k):
                for i in range(min(k, n - j - k)):
                    if (i + j) // (2 * p) == (i + j + k) // (2 * p):
                        pairs.append((i + j, i + j + k))
            k //= 2
        p *= 2
    return pairs


_SORT16 = _batcher_pairs(PEER_TOPK)


def _cmpx(v, i, j):
    a, b = v[i], v[j]
    if b is None:
        return
    if a is None:
        v[i], v[j] = b, None
        return
    v[i], v[j] = jnp.maximum(a, b), jnp.minimum(a, b)


def _sort16_desc(v):
    v = list(v)
    for i, j in _SORT16:
        _cmpx(v, i, j)
    return v


def _bitonic_merge_desc(v):
    v = list(v)
    d = PEER_TOPK // 2
    while d >= 1:
        for i in range(PEER_TOPK):
            if i & d == 0:
                _cmpx(v, i, i + d)
        d //= 2
    return v


def _half_clean(a, b):
    out = []
    for g in range(PEER_TOPK):
        x, y = a[g], b[PEER_TOPK - 1 - g]
        out.append(x if y is None else (y if x is None else jnp.maximum(x, y)))
    return out


def _top16_all_sublanes(s):
    rows = _sort16_desc([s[g * 8:(g + 1) * 8, :] for g in range(PEER_TOPK)])
    for shift in (4, 2, 1):
        partner = [pltpu.roll(r, shift, 0) for r in rows]
        rows = _bitonic_merge_desc(_half_clean(rows, partner))
    return rows


def _route_kernel(h_ref, wqt_ref, keys_ref, rk2_ref, e2_ref, c1_ref, e1_ref, q_scr, s_scr, a_scr, f_scr):
    half = PEER_DKEY // 2
    K = PEER_TOPK
    q_scr[...] = _dot_nt(wqt_ref[...], h_ref[...]).astype(BF16)

    def stage_a(h, carry):
        base = pl.multiple_of(h * PEER_DKEY, PEER_DKEY)
        for p in range(2):
            qp = q_scr[pl.ds(base + p * half, half), :]
            s = _dot(keys_ref[p], qp)
            s_scr[p, h] = s
            top = _top16_all_sublanes(s)
            for k in range(K):
                a_scr[p, k, pl.ds(h, 1), :] = top[k][0:1, :]
        return carry

    lax.fori_loop(0, PEER_HEADS, stage_a, 0)

    a1 = [a_scr[0, k] for k in range(K)]
    a2 = [a_scr[1, k] for k in range(K)]
    cand = [[a1[k1] + a2[k2] for k1 in col] for k2, col in enumerate(_CAND_COLS)]
    g0 = [cand[k2][0] for k2 in range(K)]
    rest = [cand[k2][i] for k2 in range(K) for i in range(1, len(_CAND_COLS[k2]))]
    rest += [None] * (-len(rest) % K)
    groups = [g0] + [_sort16_desc(rest[i:i + K]) for i in range(0, len(rest), K)]
    while len(groups) > 2:
        merged = [_bitonic_merge_desc(_half_clean(groups[i], groups[i + 1])) for i in range(0, len(groups) - 1, 2)]
        groups = merged + ([groups[-1]] if len(groups) % 2 else [])
    last = [x for x in _half_clean(groups[0], groups[1]) if x is not None]
    tau = functools.reduce(jnp.minimum, last)
    cmax = a1[0] + a2[0]
    zsum = None
    for k2, col in enumerate(_CAND_COLS):
        phi = None
        for i, k1 in enumerate(col):
            c = cand[k2][i]
            hit = c >= tau
            term = jnp.where(hit, jnp.exp(c - cmax), 0.0)
            zsum = term if zsum is None else zsum + term
            lo = jnp.where(hit, a1[k1], jnp.inf)
            phi = lo if phi is None else jnp.minimum(phi, lo)
        f_scr[k2] = phi
    f_scr[K] = 1.0 / zsum

    def stage_c(h, carry):
        s1 = s_scr[0, h]
        s2 = s_scr[1, h]
        rank = jnp.zeros_like(s2)
        cnt = jnp.zeros_like(s1)
        for k in range(K):
            rank = rank + jnp.where(a_scr[1, k, pl.ds(h, 1), :] > s2, 1.0, 0.0)
            cnt = cnt + jnp.where(s1 >= f_scr[k, pl.ds(h, 1), :], 1.0, 0.0)
        rk2_ref[h] = rank.astype(BF16)
        c1_ref[h] = cnt
        e2_ref[h] = jnp.exp(s2 - a_scr[1, 0, pl.ds(h, 1), :]).astype(BF16)
        e1_ref[h] = jnp.exp(s1 - a_scr[0, 0, pl.ds(h, 1), :]) * f_scr[K, pl.ds(h, 1), :]
        return carry

    lax.fori_loop(0, PEER_HEADS, stage_c, 0)


def _peer_route(h2, wqt, keys):
    T, D = h2.shape
    tb = PEER_TB
    PH = PEER_HEADS
    blk = pl.BlockSpec((PH, N_KEYS, tb), lambda i: (0, 0, i))
    return pl.pallas_call(
        _route_kernel,
        grid=(T // tb,),
        in_specs=[pl.BlockSpec((tb, D), lambda i: (i, 0)),
                  pl.BlockSpec((PH * PEER_DKEY, D), lambda i: (0, 0)),
                  pl.BlockSpec((2, N_KEYS, PEER_DKEY // 2), lambda i: (0, 0, 0))],
        out_specs=[blk, blk, blk, blk],
        out_shape=[jax.ShapeDtypeStruct((PH, N_KEYS, T), BF16),
                   jax.ShapeDtypeStruct((PH, N_KEYS, T), BF16),
                   jax.ShapeDtypeStruct((PH, N_KEYS, T), F32),
                   jax.ShapeDtypeStruct((PH, N_KEYS, T), F32)],
        scratch_shapes=[pltpu.VMEM((PH * PEER_DKEY, tb), BF16),
                        pltpu.VMEM((2, PH, N_KEYS, tb), F32),
                        pltpu.VMEM((2, PEER_TOPK, PH, tb), F32),
                        pltpu.VMEM((PEER_TOPK + 1, PH, tb), F32)],
        compiler_params=_cparams(("parallel",)),
        name="peer_route",
    )(h2, wqt, keys)


EXP_TB = 512
EXP_EB = 512


BF16_ROWS = 16
LANES = 256


def _experts_kernel(h_ref, u_ref, vt_ref, rk2_ref, e2_ref, c1_ref, e1_ref, x1_ref, mod_ref, g2_ref, b2_ref,
                    o_ref, acc_scr, at_scr, w_scr):
    e = pl.program_id(1)
    ne = pl.num_programs(1) - 1
    nsub = EXP_EB // N_KEYS
    ngrp = N_KEYS // BF16_ROWS
    tb = h_ref.shape[0]
    slot = lax.rem(e, 2)

    @pl.when(e == 0)
    def _():
        acc_scr[...] = jnp.zeros_like(acc_scr)
        w_scr[1] = jnp.zeros((EXP_EB, tb), BF16)

    at_scr[...] = _dot_nt(u_ref[...], h_ref[...])
    acc_scr[...] += _dot(vt_ref[...], w_scr[1 - slot])
    for j in range(nsub):
        i1 = jnp.minimum(e * nsub + j, N_KEYS - 1)
        cnt_rows = [c1_ref[h, pl.ds(i1, 1), :] for h in range(PEER_HEADS)]
        e1_rows = [e1_ref[h, pl.ds(i1, 1), :] for h in range(PEER_HEADS)]
        for lt in range(tb // LANES):
            cols = slice(lt * LANES, (lt + 1) * LANES)
            gates = [None] * ngrp
            for h in range(PEER_HEADS):
                cnt = jnp.broadcast_to(cnt_rows[h][:, cols], (BF16_ROWS, LANES)).astype(BF16)
                e1 = jnp.broadcast_to(e1_rows[h][:, cols], (BF16_ROWS, LANES)).astype(BF16)
                for r in range(ngrp):
                    rows = slice(r * BF16_ROWS, (r + 1) * BF16_ROWS)
                    term = jnp.where(rk2_ref[h, rows, cols] < cnt, e2_ref[h, rows, cols],
                                     jnp.zeros((), BF16)) * e1
                    gates[r] = term if gates[r] is None else gates[r] + term
            for r in range(ngrp):
                rows = slice(j * N_KEYS + r * BF16_ROWS, j * N_KEYS + (r + 1) * BF16_ROWS)
                a = at_scr[rows, cols]
                act = (0.5 * a * (1.0 + lax.erf(a * (2.0 ** -0.5)))).astype(BF16)
                w_scr[slot, rows, cols] = gates[r] * act

    @pl.when(e == ne)
    def _():
        yf = acc_scr[...].T
        gt2 = mod_ref[0, 5:6, :]
        r = ALPHA * x1_ref[...] + gt2 * yf
        o_ref[...] = _layer_norm_rows(r) * g2_ref[...] + b2_ref[...]


def _peer_experts(h2, u_b, vt_b, rk2, e2, c1, e1, x1, mod3, ln_g, ln_b, S):
    T, D = h2.shape
    tb, eb = EXP_TB, EXP_EB
    PH = PEER_HEADS
    route = pl.BlockSpec((PH, N_KEYS, tb), lambda i, e: (0, 0, i))
    ne = N_EXPERTS // eb
    return pl.pallas_call(
        _experts_kernel,
        grid=(T // tb, ne + 1),
        in_specs=[pl.BlockSpec((tb, D), lambda i, e: (i, 0)),
                  pl.BlockSpec((eb, D), lambda i, e: (jnp.minimum(e, ne - 1), 0)),
                  pl.BlockSpec((D, eb), lambda i, e: (0, jnp.maximum(e - 1, 0))),
                  route, route, route, route,
                  pl.BlockSpec((tb, D), lambda i, e: (i, 0)),
                  pl.BlockSpec((1, 6, D), lambda i, e: ((i * tb) // S, 0, 0)),
                  pl.BlockSpec((1, D), lambda i, e: (0, 0)),
                  pl.BlockSpec((1, D), lambda i, e: (0, 0))],
        out_specs=pl.BlockSpec((tb, D), lambda i, e: (i, 0)),
        out_shape=jax.ShapeDtypeStruct((T, D), F32),
        scratch_shapes=[pltpu.VMEM((D, tb), F32), pltpu.VMEM((eb, tb), F32), pltpu.VMEM((2, eb, tb), BF16)],
        compiler_params=_cparams(("parallel", "arbitrary")),
        name="peer_experts",
    )(h2, u_b, vt_b, rk2, e2, c1, e1, x1, mod3, ln_g, ln_b)


def kernel(x, c, w_ada, b_ada, w_in, b_if, conv_w, conv_b, da_lambda, da_subln_g, ml_norm_g, w_br_attn,
           w_br_mlstm, w_out, ln1_g, ln1_b, peer_wq, peer_keys, peer_u, peer_v, ln2_g, ln2_b):
    B, S, D = x.shape
    T = B * S
    assert D == D_MODEL and S % DA_TQ == 0 and S % 1024 == 0
    l = 0
    lambda_init = 0.8 - 0.6 * math.exp(-0.3 * l)

    mod3 = _modulation(c, w_ada[l], b_ada[l]).reshape(B, 6, D)

    w = w_in[l]
    o_mq = 3 * D
    o_mv = o_mq + 2 * ML_HEADS * ML_DK
    o_mo = o_mv + D
    o_if = o_mo + D
    o_ga = o_if + 2 * ML_HEADS
    o_gm = o_ga + D
    starts = (0, D, 2 * D, o_mq, o_mv, o_mo, o_ga, o_gm)
    w8 = jnp.stack([w[:, s0:s0 + D] for s0 in starts]).astype(BF16)
    w_if = w[:, o_if:o_if + 2 * ML_HEADS]
    wg = jnp.pad(w_if, ((0, 0), (0, 128 - 2 * ML_HEADS))).astype(BF16)
    wgt = w_if.T.astype(BF16)
    bias8 = b_if[l].reshape(2 * ML_HEADS)
    bcol = jnp.pad(bias8, (0, 128 - 2 * ML_HEADS)).reshape(1, 128)
    brow = bias8.reshape(2 * ML_HEADS, 1)

    x2 = x.reshape(T, D)
    z, gcol, grow = _in_proj(x2, mod3, w8, wg, wgt, bcol, brow, S)
    z4 = z.reshape(8, B, S, D)

    ya = _diff_attention(z4, da_lambda[l], da_subln_g[l], B, S, lambda_init)
    ym = _mlstm(z4, gcol.reshape(B, S, 128), grow, conv_w[l], conv_b[l].reshape(1, -1),
                ml_norm_g[l].reshape(1, -1), B, S)

    x1, h2 = _merge(ya.reshape(T, D), ym.reshape(T, D), z, x2, mod3,
                    w_br_attn[l].astype(BF16), w_br_mlstm[l].astype(BF16), w_out[l].astype(BF16),
                    ln1_g[l].reshape(1, D), ln1_b[l].reshape(1, D), S)

    rk2, e2, c1, e1 = _peer_route(h2, peer_wq[l].T.astype(BF16), peer_keys[l].astype(BF16))
    out = _peer_experts(h2, peer_u[l].astype(BF16), peer_v[l].T.astype(BF16), rk2, e2, c1, e1, x1, mod3,
                        ln2_g[l].reshape(1, D), ln2_b[l].reshape(1, D), S)
    return out.reshape(B, S, D)
```

```python
import functools
import math

import jax
import jax.numpy as jnp
from jax import lax
from jax.experimental import pallas as pl
from jax.experimental.pallas import tpu as pltpu

D_MODEL = 1024
DA_HEADS = 8
DA_DK = 64
DA_DV = 2 * DA_DK
ML_HEADS = 4
ML_DK = 128
ML_DV = 256
ML_CHUNK = 128
CONV_K = 4
PEER_HEADS = 8
PEER_TOPK = 16
N_KEYS = 128
N_EXPERTS = N_KEYS * N_KEYS
PEER_DKEY = 128
DEPTH = 1
ALPHA = (2 * DEPTH) ** 0.25
LN_EPS = 1e-5

F32 = jnp.float32
BF16 = jnp.bfloat16
NEG_INF = float("-inf")

VMEM_LIMIT_BYTES = 56 * 1024 * 1024


def _cparams(sem):
    return pltpu.CompilerParams(dimension_semantics=sem, vmem_limit_bytes=VMEM_LIMIT_BYTES)


def _layer_norm_rows(x):
    mu = jnp.mean(x, axis=-1, keepdims=True)
    xc = x - mu
    var = jnp.mean(xc * xc, axis=-1, keepdims=True)
    return xc * lax.rsqrt(var + LN_EPS)


def _dot(a, b):
    return jnp.dot(a, b, preferred_element_type=F32)


def _dot_nt(a, b):
    return lax.dot_general(a, b, (((1,), (1,)), ((), ())), preferred_element_type=F32)


def _dot_tn(a, b):
    return lax.dot_general(a, b, (((0,), (0,)), ((), ())), preferred_element_type=F32)


def _mod_kernel(c_ref, w_ref, b_ref, o_ref):
    c = c_ref[...]
    a = c * jax.nn.sigmoid(c)
    o_ref[...] = jnp.dot(a, w_ref[...], preferred_element_type=F32,
                         precision=lax.Precision.HIGHEST) + b_ref[...]


def _modulation(c, w_ada, b_ada):
    B, D = c.shape
    N = w_ada.shape[1]
    tn = 1024
    return pl.pallas_call(
        _mod_kernel,
        grid=(N // tn,),
        in_specs=[pl.BlockSpec((B, D), lambda n: (0, 0)),
                  pl.BlockSpec((D, tn), lambda n: (0, n)),
                  pl.BlockSpec((1, tn), lambda n: (0, n))],
        out_specs=pl.BlockSpec((B, tn), lambda n: (0, n)),
        out_shape=jax.ShapeDtypeStruct((B, N), F32),
        compiler_params=_cparams(("arbitrary",)),
        name="modulation",
    )(c, w_ada, b_ada.reshape(1, N))


def _inproj_kernel(x_ref, mod_ref, w_ref, wg_ref, wgt_ref, bcol_ref, brow_ref,
                   z_ref, gcol_ref, grow_ref, h_scr):
    n = pl.program_id(1)

    @pl.when(n == 0)
    def _():
        hn = _layer_norm_rows(x_ref[...])
        sh1 = mod_ref[0, 0:1, :]
        sc1 = mod_ref[0, 1:2, :]
        hb = (hn * (1.0 + sc1) + sh1).astype(BF16)
        h_scr[...] = hb
        gcol_ref[...] = _dot(hb, wg_ref[...]) + bcol_ref[...]
        grow_ref[...] = _dot_nt(wgt_ref[...], hb) + brow_ref[...]

    z_ref[0] = _dot(h_scr[...], w_ref[0]).astype(BF16)


def _in_proj(x2, mod3, w8, wg, wgt, bcol, brow, S):
    T, D = x2.shape
    tm = 1024
    npiece = w8.shape[0]
    return pl.pallas_call(
        _inproj_kernel,
        grid=(T // tm, npiece),
        in_specs=[pl.BlockSpec((tm, D), lambda i, n: (i, 0)),
                  pl.BlockSpec((1, 6, D), lambda i, n: ((i * tm) // S, 0, 0)),
                  pl.BlockSpec((1, D, D), lambda i, n: (n, 0, 0)),
                  pl.BlockSpec((D, 128), lambda i, n: (0, 0)),
                  pl.BlockSpec((8, D), lambda i, n: (0, 0)),
                  pl.BlockSpec((1, 128), lambda i, n: (0, 0)),
                  pl.BlockSpec((8, 1), lambda i, n: (0, 0))],
        out_specs=[pl.BlockSpec((1, tm, D), lambda i, n: (n, i, 0)),
                   pl.BlockSpec((tm, 128), lambda i, n: (i, 0)),
                   pl.BlockSpec((8, tm), lambda i, n: (0, i))],
        out_shape=[jax.ShapeDtypeStruct((npiece, T, D), BF16),
                   jax.ShapeDtypeStruct((T, 128), F32),
                   jax.ShapeDtypeStruct((8, T), F32)],
        scratch_shapes=[pltpu.VMEM((tm, D), BF16)],
        compiler_params=_cparams(("parallel", "arbitrary")),
        name="in_proj",
    )(x2, mod3, w8, wg, wgt, bcol, brow)


DA_TQ = 256


def _diffattn_kernel(lam_ref, g_ref, q_ref, k_ref, v_ref, o_ref, *, S, lambda_init):
    tq = DA_TQ
    lam = lam_ref[...]
    t1 = jnp.sum(lam[0:1] * lam[1:2], axis=-1, keepdims=True)
    t2 = jnp.sum(lam[2:3] * lam[3:4], axis=-1, keepdims=True)
    lam_val = jnp.exp(t1) - jnp.exp(t2) + lambda_init
    first_map = lax.broadcasted_iota(jnp.int32, (1, DA_DV), 1) < DA_DK
    gain = g_ref[...] * (1.0 - lambda_init)
    row = lax.broadcasted_iota(jnp.int32, (tq, tq), 0)
    col = lax.broadcasted_iota(jnp.int32, (tq, tq), 1)
    causal = col <= row

    for qi in range(S // tq):
        q0 = qi * tq
        qs = q_ref[0, 0, q0:q0 + tq, :] * (DA_DK ** -0.5)
        zero = jnp.zeros_like(qs)
        maps = (jnp.where(first_map, qs, zero), jnp.where(first_map, zero, qs))
        k_diag = k_ref[0, 0, q0:q0 + tq, :]
        v_diag = v_ref[0, 0, q0:q0 + tq, :]
        outs = []
        for qm in maps:
            s_diag = jnp.where(causal, _dot_nt(qm, k_diag), NEG_INF)
            m = jnp.max(s_diag, axis=-1, keepdims=True)
            if qi > 0:
                s_off = _dot_nt(qm, k_ref[0, 0, 0:q0, :])
                m = jnp.maximum(m, jnp.max(s_off, axis=-1, keepdims=True))
            p_diag = jnp.exp(s_diag - m)
            l = jnp.sum(p_diag, axis=-1, keepdims=True)
            acc = _dot(p_diag.astype(BF16), v_diag)
            if qi > 0:
                p_off = jnp.exp(s_off - m)
                l = l + jnp.sum(p_off, axis=-1, keepdims=True)
                acc = acc + _dot(p_off.astype(BF16), v_ref[0, 0, 0:q0, :])
            outs.append(acc / l)
        o = outs[0] - lam_val * outs[1]
        o = o * lax.rsqrt(jnp.mean(o * o, axis=-1, keepdims=True) + LN_EPS) * gain
        o_ref[0, q0:q0 + tq, :] = o.astype(BF16)


def _diff_attention(z4, da_lambda, subln_g, B, S, lambda_init):
    kern = functools.partial(_diffattn_kernel, S=S, lambda_init=lambda_init)
    return pl.pallas_call(
        kern,
        grid=(B, DA_HEADS),
        in_specs=[pl.BlockSpec((4, DA_DK), lambda b, h: (0, 0)),
                  pl.BlockSpec((1, DA_DV), lambda b, h: (0, 0)),
                  pl.BlockSpec((1, 1, S, DA_DV), lambda b, h: (0, b, 0, h)),
                  pl.BlockSpec((1, 1, S, DA_DV), lambda b, h: (1, b, 0, h)),
                  pl.BlockSpec((1, 1, S, DA_DV), lambda b, h: (2, b, 0, h))],
        out_specs=pl.BlockSpec((1, S, DA_DV), lambda b, h: (b, 0, h)),
        out_shape=jax.ShapeDtypeStruct((B, S, DA_HEADS * DA_DV), BF16),
        compiler_params=_cparams(("parallel", "parallel")),
        name="diff_attention",
    )(da_lambda, subln_g.reshape(1, DA_DV), z4, z4, z4)


def _mlstm_kernel(qk_ref, v_ref, og_ref, gcol_ref, grow_ref, cw_ref, cb_ref, ng_ref, o_ref,
                  qc_scr, kc_scr, c_scr, n_scr, m_scr, *, S):
    L = ML_CHUNK
    H = ML_HEADS
    srow = lax.broadcasted_iota(jnp.int32, (S, ML_DK), 0)

    for cb in range(2 * H):
        cols = slice(cb * ML_DK, (cb + 1) * ML_DK)
        x = qk_ref[0, 0, :, cols].astype(F32)
        y = x * cw_ref[CONV_K - 1:CONV_K, cols] + cb_ref[:, cols]
        for j in range(1, CONV_K):
            xs = jnp.where(srow >= j, pltpu.roll(x, j, 0), 0.0)
            y = y + xs * cw_ref[CONV_K - 1 - j:CONV_K - j, cols]
        y = y * jax.nn.sigmoid(y)
        if cb < H:
            qc_scr[:, cols] = y
        else:
            kc_scr[:, (cb - H) * ML_DK:(cb - H + 1) * ML_DK] = y * (ML_DK ** -0.5)

    r_i = lax.broadcasted_iota(jnp.int32, (L, L), 0)
    c_i = lax.broadcasted_iota(jnp.int32, (L, L), 1)
    causal = c_i <= r_i
    tril = causal.astype(F32)
    triu = (r_i <= c_i).astype(F32)
    c_scr[...] = jnp.zeros_like(c_scr)
    n_scr[...] = jnp.zeros_like(n_scr)
    m_scr[...] = jnp.zeros_like(m_scr)

    def head_chunk(hh, t0, gc, gr, b_cols, b_rows):
        Ct = c_scr[hh]
        n_row = n_scr[hh]
        m = m_scr[hh]
        q = qc_scr[pl.ds(t0, L), hh * ML_DK:(hh + 1) * ML_DK]
        k = kc_scr[pl.ds(t0, L), hh * ML_DK:(hh + 1) * ML_DK]
        v = v_ref[0, 0, pl.ds(t0, L), hh * ML_DV:(hh + 1) * ML_DV]
        ngain = ng_ref[:, hh * ML_DV:(hh + 1) * ML_DV]
        ig_col = gc[:, hh:hh + 1]
        ig_row = gr[hh:hh + 1, :]
        b_col = b_cols[:, H + hh:H + hh + 1]
        b_row = b_rows[H + hh:H + hh + 1, :]
        dm = jnp.where(causal, b_col - b_row + ig_row, NEG_INF)
        m_inter = b_col + m
        m_t = jnp.maximum(m_inter, jnp.max(dm, axis=-1, keepdims=True))
        w = jnp.exp(dm - m_t)
        qb = q.astype(BF16)
        kb = k.astype(BF16)
        p = w * _dot_nt(qb, kb)
        inter = jnp.exp(m_inter - m_t)
        num = _dot(p.astype(BF16), v) + inter * _dot(qb, Ct.astype(BF16))
        nq = jnp.sum(p, axis=-1, keepdims=True) + inter * jnp.sum(q * n_row, axis=-1, keepdims=True)
        hout = num / jnp.maximum(jnp.abs(nq), jnp.exp(-m_t))
        hout = hout * lax.rsqrt(jnp.mean(hout * hout, axis=-1, keepdims=True) + LN_EPS) * ngain
        og = og_ref[0, 0, pl.ds(t0, L), hh * ML_DV:(hh + 1) * ML_DV].astype(F32)
        o_ref[0, pl.ds(t0, L), hh * ML_DV:(hh + 1) * ML_DV] = (hout * jax.nn.sigmoid(og)).astype(BF16)
        m_new = m_t[L - 1:L, :]
        b_last = b_col[L - 1:L, :]
        decay = jnp.exp(b_last + m - m_new)
        w_s = jnp.exp(b_last - b_col + ig_col - m_new)
        c_scr[hh] = decay * Ct + _dot_tn(kb, (v.astype(F32) * w_s).astype(BF16))
        n_scr[hh] = decay * n_row + jnp.sum(k * w_s, axis=0, keepdims=True)
        m_scr[hh] = m_new

    def chunk(ci, carry):
        t0 = pl.multiple_of(ci * L, L)
        gc = gcol_ref[0, pl.ds(t0, L), :]
        gr = grow_ref[:, pl.ds(t0, L)]
        b_cols = jnp.dot(tril, jax.nn.log_sigmoid(gc), preferred_element_type=F32,
                         precision=lax.Precision.HIGHEST)
        b_rows = jnp.dot(jax.nn.log_sigmoid(gr), triu, preferred_element_type=F32,
                         precision=lax.Precision.HIGHEST)
        for hh in range(H):
            head_chunk(hh, t0, gc, gr, b_cols, b_rows)
        return carry

    lax.fori_loop(0, S // L, chunk, 0)


def _mlstm(z4, gcol3, grow, conv_w, conv_b, norm_g, B, S):
    kern = functools.partial(_mlstm_kernel, S=S)
    H = ML_HEADS
    D = H * ML_DV
    piece = lambda n: pl.BlockSpec((1, 1, S, D), lambda b: (n, b, 0, 0))
    return pl.pallas_call(
        kern,
        grid=(B,),
        in_specs=[piece(3), piece(4), piece(5),
                  pl.BlockSpec((1, S, 128), lambda b: (b, 0, 0)),
                  pl.BlockSpec((8, S), lambda b: (0, b)),
                  pl.BlockSpec((CONV_K, 2 * H * ML_DK), lambda b: (0, 0)),
                  pl.BlockSpec((1, 2 * H * ML_DK), lambda b: (0, 0)),
                  pl.BlockSpec((1, D), lambda b: (0, 0))],
        out_specs=pl.BlockSpec((1, S, D), lambda b: (b, 0, 0)),
        out_shape=jax.ShapeDtypeStruct((B, S, D), BF16),
        scratch_shapes=[pltpu.VMEM((S, H * ML_DK), F32), pltpu.VMEM((S, H * ML_DK), F32),
                        pltpu.VMEM((H, ML_DK, ML_DV), F32), pltpu.VMEM((H, 1, ML_DK), F32),
                        pltpu.VMEM((H, 1, 1), F32)],
        compiler_params=_cparams(("parallel",)),
        name="mlstm",
    )(z4, z4, z4, gcol3, grow, conv_w, conv_b, norm_g)


def _merge_kernel(ya_ref, ym_ref, ga_ref, gm_ref, x_ref, mod_ref, wa_ref, wm_ref, wo_ref, g1_ref, b1_ref,
                  x1_ref, h2_ref):
    ya = _dot(ya_ref[...], wa_ref[...])
    ym = _dot(ym_ref[...], wm_ref[...])
    y = (jax.nn.sigmoid(ga_ref[0].astype(F32)) * ya + jax.nn.sigmoid(gm_ref[0].astype(F32)) * ym)
    y2 = _dot(y.astype(BF16), wo_ref[...])
    gt1 = mod_ref[0, 2:3, :]
    sh2 = mod_ref[0, 3:4, :]
    sc2 = mod_ref[0, 4:5, :]
    x1 = _layer_norm_rows(ALPHA * x_ref[...] + gt1 * y2) * g1_ref[...] + b1_ref[...]
    x1_ref[...] = x1
    h2_ref[...] = (_layer_norm_rows(x1) * (1.0 + sc2) + sh2).T.astype(BF16)


def _merge(ya2, ym2, z3, x2, mod3, wa, wm, wo, ln_g, ln_b, S):
    T, D = x2.shape
    tm = 512
    tok = lambda i: (i, 0)
    const = lambda i: (0, 0)
    return pl.pallas_call(
        _merge_kernel,
        grid=(T // tm,),
        in_specs=[pl.BlockSpec((tm, D), tok), pl.BlockSpec((tm, D), tok),
                  pl.BlockSpec((1, tm, D), lambda i: (6, i, 0)),
                  pl.BlockSpec((1, tm, D), lambda i: (7, i, 0)),
                  pl.BlockSpec((tm, D), tok),
                  pl.BlockSpec((1, 6, D), lambda i: ((i * tm) // S, 0, 0)),
                  pl.BlockSpec((D, D), const), pl.BlockSpec((D, D), const), pl.BlockSpec((D, D), const),
                  pl.BlockSpec((1, D), const), pl.BlockSpec((1, D), const)],
        out_specs=[pl.BlockSpec((tm, D), tok), pl.BlockSpec((D, tm), lambda i: (0, i))],
        out_shape=[jax.ShapeDtypeStruct((T, D), F32), jax.ShapeDtypeStruct((D, T), BF16)],
        compiler_params=_cparams(("parallel",)),
        name="merge",
    )(ya2, ym2, z3, z3, x2, mod3, wa, wm, wo, ln_g, ln_b)


PEER_TB = 256
_CAND_COLS = [[k1 for k1 in range(PEER_TOPK) if (k1 + 1) * (k2 + 1) <= PEER_TOPK] for k2 in range(PEER_TOPK)]


def _batcher_pairs(n):
    pairs = []
    p = 1
    while p < n:
        k = p
        while k >= 1:
            for j in range(k % p, n - k, 2 * k):
                for i in range(min(k, n - j - k)):
                    if (i + j) // (2 * p) == (i + j + k) // (2 * p):
                        pairs.append((i + j, i + j + k))
            k //= 2
        p *= 2
    return pairs


_SORT16 = _batcher_pairs(PEER_TOPK)


def _cmpx(v, i, j):
    a, b = v[i], v[j]
    if b is None:
        return
    if a is None:
        v[i], v[j] = b, None
        return
    v[i], v[j] = jnp.maximum(a, b), jnp.minimum(a, b)


def _sort16_desc(v):
    v = list(v)
    for i, j in _SORT16:
        _cmpx(v, i, j)
    return v


def _bitonic_merge_desc(v):
    v = list(v)
    d = PEER_TOPK // 2
    while d >= 1:
        for i in range(PEER_TOPK):
            if i & d == 0:
                _cmpx(v, i, i + d)
        d //= 2
    return v


def _half_clean(a, b):
    out = []
    for g in range(PEER_TOPK):
        x, y = a[g], b[PEER_TOPK - 1 - g]
        out.append(x if y is None else (y if x is None else jnp.maximum(x, y)))
    return out


def _top16_all_sublanes(s):
    rows = _sort16_desc([s[g * 8:(g + 1) * 8, :] for g in range(PEER_TOPK)])
    for shift in (4, 2, 1):
        partner = [pltpu.roll(r, shift, 0) for r in rows]
        rows = _bitonic_merge_desc(_half_clean(rows, partner))
    return rows


def _route_kernel(h_ref, wqt_ref, keys_ref, rk2_ref, e2_ref, c1_ref, e1_ref, q_scr, s_scr, a_scr, f_scr):
    half = PEER_DKEY // 2
    K = PEER_TOPK
    q_scr[...] = _dot(wqt_ref[...], h_ref[...]).astype(BF16)

    def stage_a(h, carry):
        base = pl.multiple_of(h * PEER_DKEY, PEER_DKEY)
        for p in range(2):
            qp = q_scr[pl.ds(base + p * half, half), :]
            s = _dot(keys_ref[p], qp)
            s_scr[p, h] = s
            top = _top16_all_sublanes(s)
            for k in range(K):
                a_scr[p, k, pl.ds(h, 1), :] = top[k][0:1, :]
        return carry

    lax.fori_loop(0, PEER_HEADS, stage_a, 0)

    a1 = [a_scr[0, k] for k in range(K)]
    a2 = [a_scr[1, k] for k in range(K)]
    cand = [[a1[k1] + a2[k2] for k1 in col] for k2, col in enumerate(_CAND_COLS)]
    g0 = [cand[k2][0] for k2 in range(K)]
    rest = [cand[k2][i] for k2 in range(K) for i in range(1, len(_CAND_COLS[k2]))]
    rest += [None] * (-len(rest) % K)
    groups = [g0] + [_sort16_desc(rest[i:i + K]) for i in range(0, len(rest), K)]
    while len(groups) > 2:
        merged = [_bitonic_merge_desc(_half_clean(groups[i], groups[i + 1])) for i in range(0, len(groups) - 1, 2)]
        groups = merged + ([groups[-1]] if len(groups) % 2 else [])
    last = [x for x in _half_clean(groups[0], groups[1]) if x is not None]
    tau = functools.reduce(jnp.minimum, last)
    cmax = a1[0] + a2[0]
    zsum = None
    for k2, col in enumerate(_CAND_COLS):
        phi = None
        for i, k1 in enumerate(col):
            c = cand[k2][i]
            hit = c >= tau
            term = jnp.where(hit, jnp.exp(c - cmax), 0.0)
            zsum = term if zsum is None else zsum + term
            lo = jnp.where(hit, a1[k1], jnp.inf)
            phi = lo if phi is None else jnp.minimum(phi, lo)
        f_scr[k2] = phi
    f_scr[K] = 1.0 / zsum

    def prefix_count(pred, thr):
        b8 = pred(thr[7])
        b4 = pred(jnp.where(b8, thr[11], thr[3]))
        b2 = pred(jnp.where(b8, jnp.where(b4, thr[13], thr[9]), jnp.where(b4, thr[5], thr[1])))
        b1 = pred(jnp.where(b8,
                            jnp.where(b4, jnp.where(b2, thr[14], thr[12]), jnp.where(b2, thr[10], thr[8])),
                            jnp.where(b4, jnp.where(b2, thr[6], thr[4]), jnp.where(b2, thr[2], thr[0]))))
        b0 = pred(thr[15])
        bit = lambda b, v: jnp.where(b, v, 0.0)
        return bit(b8, 8.0) + bit(b4, 4.0) + bit(b2, 2.0) + bit(b1, 1.0) + bit(b0, 1.0)

    def stage_c(h, carry):
        s1 = s_scr[0, h]
        s2 = s_scr[1, h]
        shape = s2.shape
        top2 = [jnp.broadcast_to(a_scr[1, k, pl.ds(h, 1), :], shape) for k in range(K)]
        phi = [jnp.broadcast_to(f_scr[k, pl.ds(h, 1), :], shape) for k in range(K)]
        rk2_ref[h] = prefix_count(lambda t: t > s2, top2).astype(BF16)
        c1_ref[h] = prefix_count(lambda t: s1 >= t, phi)
        e2_ref[h] = jnp.exp(s2 - a_scr[1, 0, pl.ds(h, 1), :]).astype(BF16)
        e1_ref[h] = jnp.exp(s1 - a_scr[0, 0, pl.ds(h, 1), :]) * f_scr[K, pl.ds(h, 1), :]
        return carry

    lax.fori_loop(0, PEER_HEADS, stage_c, 0)


def _peer_route(h2t, wqt, keys):
    D, T = h2t.shape
    tb = PEER_TB
    PH = PEER_HEADS
    blk = pl.BlockSpec((PH, N_KEYS, tb), lambda i: (0, 0, i))
    return pl.pallas_call(
        _route_kernel,
        grid=(T // tb,),
        in_specs=[pl.BlockSpec((D, tb), lambda i: (0, i)),
                  pl.BlockSpec((PH * PEER_DKEY, D), lambda i: (0, 0)),
                  pl.BlockSpec((2, N_KEYS, PEER_DKEY // 2), lambda i: (0, 0, 0))],
        out_specs=[blk, blk, blk, blk],
        out_shape=[jax.ShapeDtypeStruct((PH, N_KEYS, T), BF16),
                   jax.ShapeDtypeStruct((PH, N_KEYS, T), BF16),
                   jax.ShapeDtypeStruct((PH, N_KEYS, T), F32),
                   jax.ShapeDtypeStruct((PH, N_KEYS, T), F32)],
        scratch_shapes=[pltpu.VMEM((PH * PEER_DKEY, tb), BF16),
                        pltpu.VMEM((2, PH, N_KEYS, tb), F32),
                        pltpu.VMEM((2, PEER_TOPK, PH, tb), F32),
                        pltpu.VMEM((PEER_TOPK + 1, PH, tb), F32)],
        compiler_params=_cparams(("parallel",)),
        name="peer_route",
    )(h2t, wqt, keys)


EXP_TB = 512
EXP_EB = 1024


BF16_ROWS = 16
ACT_ROWS = 32
LANES = 256


def _experts_kernel(h_ref, u_ref, vt_ref, rk2_ref, e2_ref, c1_ref, e1_ref, x1_ref, mod_ref, g2_ref, b2_ref,
                    o_ref, acc_scr, at_scr, w_scr):
    e = pl.program_id(1)
    ne = pl.num_programs(1) - 1
    nsub = EXP_EB // N_KEYS
    ngrp = N_KEYS // BF16_ROWS
    tb = h_ref.shape[1]
    slot = lax.rem(e, 2)

    @pl.when(e == 0)
    def _():
        acc_scr[...] = jnp.zeros_like(acc_scr)
        w_scr[1] = jnp.zeros((EXP_EB, tb), BF16)

    at_scr[...] = _dot(u_ref[...], h_ref[...])
    acc_scr[...] += _dot(vt_ref[...], w_scr[1 - slot])
    for j in range(nsub):
        cnt_rows = [c1_ref[h, j:j + 1, :] for h in range(PEER_HEADS)]
        e1_rows = [e1_ref[h, j:j + 1, :] for h in range(PEER_HEADS)]
        for lt in range(tb // LANES):
            cols = slice(lt * LANES, (lt + 1) * LANES)
            gates = [None] * ngrp
            for h in range(PEER_HEADS):
                cnt = jnp.broadcast_to(cnt_rows[h][:, cols], (BF16_ROWS, LANES)).astype(BF16)
                e1 = jnp.broadcast_to(e1_rows[h][:, cols], (BF16_ROWS, LANES)).astype(BF16)
                for r in range(ngrp):
                    rows = slice(r * BF16_ROWS, (r + 1) * BF16_ROWS)
                    term = jnp.where(rk2_ref[h, rows, cols] < cnt, e2_ref[h, rows, cols],
                                     jnp.zeros((), BF16)) * e1
                    gates[r] = term if gates[r] is None else gates[r] + term
            for r in range(ngrp):
                rows = slice(j * N_KEYS + r * BF16_ROWS, j * N_KEYS + (r + 1) * BF16_ROWS)
                a = at_scr[rows, cols]
                act = (0.5 * a * (1.0 + lax.erf(a * (2.0 ** -0.5)))).astype(BF16)
                w_scr[slot, rows, cols] = gates[r] * act

    @pl.when(e == ne)
    def _():
        yf = acc_scr[...].T
        gt2 = mod_ref[0, 5:6, :]
        r = ALPHA * x1_ref[...] + gt2 * yf
        o_ref[...] = _layer_norm_rows(r) * g2_ref[...] + b2_ref[...]


def _peer_experts(h2t, u_b, vt_b, rk2, e2, c1, e1, x1, mod3, ln_g, ln_b, S):
    D, T = h2t.shape
    tb, eb = EXP_TB, EXP_EB
    PH = PEER_HEADS
    route = pl.BlockSpec((PH, N_KEYS, tb), lambda i, e: (0, 0, i))
    ne = N_EXPERTS // eb
    key1 = pl.BlockSpec((PH, eb // N_KEYS, tb), lambda i, e: (0, jnp.minimum(e, ne - 1), i))
    return pl.pallas_call(
        _experts_kernel,
        grid=(T // tb, ne + 1),
        in_specs=[pl.BlockSpec((D, tb), lambda i, e: (0, i)),
                  pl.BlockSpec((eb, D), lambda i, e: (jnp.minimum(e, ne - 1), 0)),
                  pl.BlockSpec((D, eb), lambda i, e: (0, jnp.maximum(e - 1, 0))),
                  route, route, key1, key1,
                  pl.BlockSpec((tb, D), lambda i, e: (i, 0)),
                  pl.BlockSpec((1, 6, D), lambda i, e: ((i * tb) // S, 0, 0)),
                  pl.BlockSpec((1, D), lambda i, e: (0, 0)),
                  pl.BlockSpec((1, D), lambda i, e: (0, 0))],
        out_specs=pl.BlockSpec((tb, D), lambda i, e: (i, 0)),
        out_shape=jax.ShapeDtypeStruct((T, D), F32),
        scratch_shapes=[pltpu.VMEM((D, tb), F32), pltpu.VMEM((eb, tb), F32), pltpu.VMEM((2, eb, tb), BF16)],
        compiler_params=_cparams(("parallel", "arbitrary")),
        name="peer_experts",
    )(h2t, u_b, vt_b, rk2, e2, c1, e1, x1, mod3, ln_g, ln_b)


def kernel(x, c, w_ada, b_ada, w_in, b_if, conv_w, conv_b, da_lambda, da_subln_g, ml_norm_g, w_br_attn,
           w_br_mlstm, w_out, ln1_g, ln1_b, peer_wq, peer_keys, peer_u, peer_v, ln2_g, ln2_b):
    B, S, D = x.shape
    T = B * S
    assert D == D_MODEL and S % DA_TQ == 0 and S % 1024 == 0
    l = 0
    lambda_init = 0.8 - 0.6 * math.exp(-0.3 * l)

    mod3 = _modulation(c, w_ada[l], b_ada[l]).reshape(B, 6, D)

    w = w_in[l]
    o_mq = 3 * D
    o_mv = o_mq + 2 * ML_HEADS * ML_DK
    o_mo = o_mv + D
    o_if = o_mo + D
    o_ga = o_if + 2 * ML_HEADS
    o_gm = o_ga + D
    starts = (0, D, 2 * D, o_mq, o_mv, o_mo, o_ga, o_gm)
    w8 = jnp.stack([w[:, s0:s0 + D] for s0 in starts]).astype(BF16)
    w_if = w[:, o_if:o_if + 2 * ML_HEADS]
    wg = jnp.pad(w_if, ((0, 0), (0, 128 - 2 * ML_HEADS))).astype(BF16)
    wgt = w_if.T.astype(BF16)
    bias8 = b_if[l].reshape(2 * ML_HEADS)
    bcol = jnp.pad(bias8, (0, 128 - 2 * ML_HEADS)).reshape(1, 128)
    brow = bias8.reshape(2 * ML_HEADS, 1)

    x2 = x.reshape(T, D)
    z, gcol, grow = _in_proj(x2, mod3, w8, wg, wgt, bcol, brow, S)
    z4 = z.reshape(8, B, S, D)

    ya = _diff_attention(z4, da_lambda[l], da_subln_g[l], B, S, lambda_init)
    ym = _mlstm(z4, gcol.reshape(B, S, 128), grow, conv_w[l], conv_b[l].reshape(1, -1),
                ml_norm_g[l].reshape(1, -1), B, S)

    x1, h2 = _merge(ya.reshape(T, D), ym.reshape(T, D), z, x2, mod3,
                    w_br_attn[l].astype(BF16), w_br_mlstm[l].astype(BF16), w_out[l].astype(BF16),
                    ln1_g[l].reshape(1, D), ln1_b[l].reshape(1, D), S)

    rk2, e2, c1, e1 = _peer_route(h2, peer_wq[l].T.astype(BF16), peer_keys[l].astype(BF16))
    out = _peer_experts(h2, peer_u[l].astype(BF16), peer_v[l].T.astype(BF16), rk2, e2, c1, e1, x1, mod3,
                        ln2_g[l].reshape(1, D), ln2_b[l].reshape(1, D), S)
    return out.reshape(B, S, D)
```

```python
import functools
import math

import jax
import jax.numpy as jnp
from jax import lax
from jax.experimental import pallas as pl
from jax.experimental.pallas import tpu as pltpu

D_MODEL = 1024
DA_HEADS = 8
DA_DK = 64
DA_DV = 2 * DA_DK
ML_HEADS = 4
ML_DK = 128
ML_DV = 256
ML_CHUNK = 128
CONV_K = 4
PEER_HEADS = 8
PEER_TOPK = 16
N_KEYS = 128
N_EXPERTS = N_KEYS * N_KEYS
PEER_DKEY = 128
DEPTH = 1
ALPHA = (2 * DEPTH) ** 0.25
LN_EPS = 1e-5

F32 = jnp.float32
BF16 = jnp.bfloat16
NEG_INF = float("-inf")

VMEM_LIMIT_BYTES = 56 * 1024 * 1024


def _cparams(sem):
    return pltpu.CompilerParams(dimension_semantics=sem, vmem_limit_bytes=VMEM_LIMIT_BYTES)


def _layer_norm_rows(x):
    mu = jnp.mean(x, axis=-1, keepdims=True)
    xc = x - mu
    var = jnp.mean(xc * xc, axis=-1, keepdims=True)
    return xc * lax.rsqrt(var + LN_EPS)


def _dot(a, b):
    return jnp.dot(a, b, preferred_element_type=F32)


def _dot_nt(a, b):
    return lax.dot_general(a, b, (((1,), (1,)), ((), ())), preferred_element_type=F32)


def _dot_tn(a, b):
    return lax.dot_general(a, b, (((0,), (0,)), ((), ())), preferred_element_type=F32)


def _mod_kernel(c_ref, w_ref, b_ref, o_ref):
    c = c_ref[...]
    a = c * jax.nn.sigmoid(c)
    o_ref[...] = jnp.dot(a, w_ref[...], preferred_element_type=F32,
                         precision=lax.Precision.HIGHEST) + b_ref[...]


def _modulation(c, w_ada, b_ada):
    B, D = c.shape
    N = w_ada.shape[1]
    tn = 1024
    return pl.pallas_call(
        _mod_kernel,
        grid=(N // tn,),
        in_specs=[pl.BlockSpec((B, D), lambda n: (0, 0)),
                  pl.BlockSpec((D, tn), lambda n: (0, n)),
                  pl.BlockSpec((1, tn), lambda n: (0, n))],
        out_specs=pl.BlockSpec((B, tn), lambda n: (0, n)),
        out_shape=jax.ShapeDtypeStruct((B, N), F32),
        compiler_params=_cparams(("arbitrary",)),
        name="modulation",
    )(c, w_ada, b_ada.reshape(1, N))


def _inproj_kernel(x_ref, mod_ref, w_ref, wg_ref, wgt_ref, bcol_ref, brow_ref,
                   z_ref, gcol_ref, grow_ref, h_scr):
    n = pl.program_id(1)

    @pl.when(n == 0)
    def _():
        hn = _layer_norm_rows(x_ref[...])
        sh1 = mod_ref[0, 0:1, :]
        sc1 = mod_ref[0, 1:2, :]
        hb = (hn * (1.0 + sc1) + sh1).astype(BF16)
        h_scr[...] = hb
        gcol_ref[...] = _dot(hb, wg_ref[...]) + bcol_ref[...]
        grow_ref[...] = _dot_nt(wgt_ref[...], hb) + brow_ref[...]

    z_ref[0] = _dot(h_scr[...], w_ref[0]).astype(BF16)


def _in_proj(x2, mod3, w8, wg, wgt, bcol, brow, S):
    T, D = x2.shape
    tm = 1024
    npiece = w8.shape[0]
    return pl.pallas_call(
        _inproj_kernel,
        grid=(T // tm, npiece),
        in_specs=[pl.BlockSpec((tm, D), lambda i, n: (i, 0)),
                  pl.BlockSpec((1, 6, D), lambda i, n: ((i * tm) // S, 0, 0)),
                  pl.BlockSpec((1, D, D), lambda i, n: (n, 0, 0)),
                  pl.BlockSpec((D, 128), lambda i, n: (0, 0)),
                  pl.BlockSpec((8, D), lambda i, n: (0, 0)),
                  pl.BlockSpec((1, 128), lambda i, n: (0, 0)),
                  pl.BlockSpec((8, 1), lambda i, n: (0, 0))],
        out_specs=[pl.BlockSpec((1, tm, D), lambda i, n: (n, i, 0)),
                   pl.BlockSpec((tm, 128), lambda i, n: (i, 0)),
                   pl.BlockSpec((8, tm), lambda i, n: (0, i))],
        out_shape=[jax.ShapeDtypeStruct((npiece, T, D), BF16),
                   jax.ShapeDtypeStruct((T, 128), F32),
                   jax.ShapeDtypeStruct((8, T), F32)],
        scratch_shapes=[pltpu.VMEM((tm, D), BF16)],
        compiler_params=_cparams(("parallel", "arbitrary")),
        name="in_proj",
    )(x2, mod3, w8, wg, wgt, bcol, brow)


DA_TQ = 256


def _diffattn_kernel(lam_ref, g_ref, q_ref, k_ref, v_ref, o_ref, *, S, lambda_init):
    tq = DA_TQ
    lam = lam_ref[...]
    t1 = jnp.sum(lam[0:1] * lam[1:2], axis=-1, keepdims=True)
    t2 = jnp.sum(lam[2:3] * lam[3:4], axis=-1, keepdims=True)
    lam_val = jnp.exp(t1) - jnp.exp(t2) + lambda_init
    first_map = lax.broadcasted_iota(jnp.int32, (1, DA_DV), 1) < DA_DK
    gain = g_ref[...] * (1.0 - lambda_init)
    row = lax.broadcasted_iota(jnp.int32, (tq, tq), 0)
    col = lax.broadcasted_iota(jnp.int32, (tq, tq), 1)
    causal = col <= row

    for qi in range(S // tq):
        q0 = qi * tq
        qs = q_ref[0, 0, q0:q0 + tq, :] * (DA_DK ** -0.5)
        zero = jnp.zeros_like(qs)
        maps = (jnp.where(first_map, qs, zero), jnp.where(first_map, zero, qs))
        k_diag = k_ref[0, 0, q0:q0 + tq, :]
        v_diag = v_ref[0, 0, q0:q0 + tq, :]
        outs = []
        for qm in maps:
            s_diag = jnp.where(causal, _dot_nt(qm, k_diag), NEG_INF)
            m = jnp.max(s_diag, axis=-1, keepdims=True)
            if qi > 0:
                s_off = _dot_nt(qm, k_ref[0, 0, 0:q0, :])
                m = jnp.maximum(m, jnp.max(s_off, axis=-1, keepdims=True))
            p_diag = jnp.exp(s_diag - m)
            l = jnp.sum(p_diag, axis=-1, keepdims=True)
            acc = _dot(p_diag.astype(BF16), v_diag)
            if qi > 0:
                p_off = jnp.exp(s_off - m)
                l = l + jnp.sum(p_off, axis=-1, keepdims=True)
                acc = acc + _dot(p_off.astype(BF16), v_ref[0, 0, 0:q0, :])
            outs.append(acc / l)
        o = outs[0] - lam_val * outs[1]
        o = o * lax.rsqrt(jnp.mean(o * o, axis=-1, keepdims=True) + LN_EPS) * gain
        o_ref[0, q0:q0 + tq, :] = o.astype(BF16)


def _diff_attention(z4, da_lambda, subln_g, B, S, lambda_init):
    kern = functools.partial(_diffattn_kernel, S=S, lambda_init=lambda_init)
    return pl.pallas_call(
        kern,
        grid=(B, DA_HEADS),
        in_specs=[pl.BlockSpec((4, DA_DK), lambda b, h: (0, 0)),
                  pl.BlockSpec((1, DA_DV), lambda b, h: (0, 0)),
                  pl.BlockSpec((1, 1, S, DA_DV), lambda b, h: (0, b, 0, h)),
                  pl.BlockSpec((1, 1, S, DA_DV), lambda b, h: (1, b, 0, h)),
                  pl.BlockSpec((1, 1, S, DA_DV), lambda b, h: (2, b, 0, h))],
        out_specs=pl.BlockSpec((1, S, DA_DV), lambda b, h: (b, 0, h)),
        out_shape=jax.ShapeDtypeStruct((B, S, DA_HEADS * DA_DV), BF16),
        compiler_params=_cparams(("parallel", "parallel")),
        name="diff_attention",
    )(da_lambda, subln_g.reshape(1, DA_DV), z4, z4, z4)


def _mlstm_kernel(qk_ref, v_ref, og_ref, gcol_ref, grow_ref, cw_ref, cb_ref, ng_ref, o_ref,
                  qc_scr, kc_scr, c_scr, n_scr, m_scr, *, S):
    L = ML_CHUNK
    H = ML_HEADS
    srow = lax.broadcasted_iota(jnp.int32, (S, ML_DK), 0)

    for cb in range(2 * H):
        cols = slice(cb * ML_DK, (cb + 1) * ML_DK)
        x = qk_ref[0, 0, :, cols].astype(F32)
        y = x * cw_ref[CONV_K - 1:CONV_K, cols] + cb_ref[:, cols]
        for j in range(1, CONV_K):
            xs = jnp.where(srow >= j, pltpu.roll(x, j, 0), 0.0)
            y = y + xs * cw_ref[CONV_K - 1 - j:CONV_K - j, cols]
        y = y * jax.nn.sigmoid(y)
        if cb < H:
            qc_scr[:, cols] = y
        else:
            kc_scr[:, (cb - H) * ML_DK:(cb - H + 1) * ML_DK] = y * (ML_DK ** -0.5)

    r_i = lax.broadcasted_iota(jnp.int32, (L, L), 0)
    c_i = lax.broadcasted_iota(jnp.int32, (L, L), 1)
    causal = c_i <= r_i
    tril = causal.astype(F32)
    triu = (r_i <= c_i).astype(F32)
    c_scr[...] = jnp.zeros_like(c_scr)
    n_scr[...] = jnp.zeros_like(n_scr)
    m_scr[...] = jnp.zeros_like(m_scr)

    def head_chunk(hh, t0, gc, gr, b_cols, b_rows):
        Ct = c_scr[hh]
        n_row = n_scr[hh]
        m = m_scr[hh]
        q = qc_scr[pl.ds(t0, L), hh * ML_DK:(hh + 1) * ML_DK]
        k = kc_scr[pl.ds(t0, L), hh * ML_DK:(hh + 1) * ML_DK]
        v = v_ref[0, 0, pl.ds(t0, L), hh * ML_DV:(hh + 1) * ML_DV]
        ngain = ng_ref[:, hh * ML_DV:(hh + 1) * ML_DV]
        ig_col = gc[:, hh:hh + 1]
        ig_row = gr[hh:hh + 1, :]
        b_col = b_cols[:, H + hh:H + hh + 1]
        b_row = b_rows[H + hh:H + hh + 1, :]
        dm = jnp.where(causal, b_col - b_row + ig_row, NEG_INF)
        m_inter = b_col + m
        m_t = jnp.maximum(m_inter, jnp.max(dm, axis=-1, keepdims=True))
        w = jnp.exp(dm - m_t)
        qb = q.astype(BF16)
        kb = k.astype(BF16)
        p = w * _dot_nt(qb, kb)
        inter = jnp.exp(m_inter - m_t)
        num = _dot(p.astype(BF16), v) + inter * _dot(qb, Ct.astype(BF16))
        nq = jnp.sum(p, axis=-1, keepdims=True) + inter * jnp.sum(q * n_row, axis=-1, keepdims=True)
        hout = num / jnp.maximum(jnp.abs(nq), jnp.exp(-m_t))
        hout = hout * lax.rsqrt(jnp.mean(hout * hout, axis=-1, keepdims=True) + LN_EPS) * ngain
        og = og_ref[0, 0, pl.ds(t0, L), hh * ML_DV:(hh + 1) * ML_DV].astype(F32)
        o_ref[0, pl.ds(t0, L), hh * ML_DV:(hh + 1) * ML_DV] = (hout * jax.nn.sigmoid(og)).astype(BF16)
        m_new = m_t[L - 1:L, :]
        b_last = b_col[L - 1:L, :]
        decay = jnp.exp(b_last + m - m_new)
        w_s = jnp.exp(b_last - b_col + ig_col - m_new)
        c_scr[hh] = decay * Ct + _dot_tn(kb, (v.astype(F32) * w_s).astype(BF16))
        n_scr[hh] = decay * n_row + jnp.sum(k * w_s, axis=0, keepdims=True)
        m_scr[hh] = m_new

    def chunk(ci, carry):
        t0 = pl.multiple_of(ci * L, L)
        gc = gcol_ref[0, pl.ds(t0, L), :]
        gr = grow_ref[:, pl.ds(t0, L)]
        b_cols = jnp.dot(tril, jax.nn.log_sigmoid(gc), preferred_element_type=F32,
                         precision=lax.Precision.HIGHEST)
        b_rows = jnp.dot(jax.nn.log_sigmoid(gr), triu, preferred_element_type=F32,
                         precision=lax.Precision.HIGHEST)
        for hh in range(H):
            head_chunk(hh, t0, gc, gr, b_cols, b_rows)
        return carry

    lax.fori_loop(0, S // L, chunk, 0)


def _mlstm(z4, gcol3, grow, conv_w, conv_b, norm_g, B, S):
    kern = functools.partial(_mlstm_kernel, S=S)
    H = ML_HEADS
    D = H * ML_DV
    piece = lambda n: pl.BlockSpec((1, 1, S, D), lambda b: (n, b, 0, 0))
    return pl.pallas_call(
        kern,
        grid=(B,),
        in_specs=[piece(3), piece(4), piece(5),
                  pl.BlockSpec((1, S, 128), lambda b: (b, 0, 0)),
                  pl.BlockSpec((8, S), lambda b: (0, b)),
                  pl.BlockSpec((CONV_K, 2 * H * ML_DK), lambda b: (0, 0)),
                  pl.BlockSpec((1, 2 * H * ML_DK), lambda b: (0, 0)),
                  pl.BlockSpec((1, D), lambda b: (0, 0))],
        out_specs=pl.BlockSpec((1, S, D), lambda b: (b, 0, 0)),
        out_shape=jax.ShapeDtypeStruct((B, S, D), BF16),
        scratch_shapes=[pltpu.VMEM((S, H * ML_DK), F32), pltpu.VMEM((S, H * ML_DK), F32),
                        pltpu.VMEM((H, ML_DK, ML_DV), F32), pltpu.VMEM((H, 1, ML_DK), F32),
                        pltpu.VMEM((H, 1, 1), F32)],
        compiler_params=_cparams(("parallel",)),
        name="mlstm",
    )(z4, z4, z4, gcol3, grow, conv_w, conv_b, norm_g)


def _merge_kernel(ya_ref, ym_ref, ga_ref, gm_ref, x_ref, mod_ref, wa_ref, wm_ref, wo_ref, g1_ref, b1_ref,
                  x1_ref, h2_ref):
    ya = _dot(ya_ref[...], wa_ref[...])
    ym = _dot(ym_ref[...], wm_ref[...])
    y = (jax.nn.sigmoid(ga_ref[0].astype(F32)) * ya + jax.nn.sigmoid(gm_ref[0].astype(F32)) * ym)
    y2 = _dot(y.astype(BF16), wo_ref[...])
    gt1 = mod_ref[0, 2:3, :]
    sh2 = mod_ref[0, 3:4, :]
    sc2 = mod_ref[0, 4:5, :]
    x1 = _layer_norm_rows(ALPHA * x_ref[...] + gt1 * y2) * g1_ref[...] + b1_ref[...]
    x1_ref[...] = x1
    h2_ref[...] = (_layer_norm_rows(x1) * (1.0 + sc2) + sh2).T.astype(BF16)


def _merge(ya2, ym2, z3, x2, mod3, wa, wm, wo, ln_g, ln_b, S):
    T, D = x2.shape
    tm = 512
    tok = lambda i: (i, 0)
    const = lambda i: (0, 0)
    return pl.pallas_call(
        _merge_kernel,
        grid=(T // tm,),
        in_specs=[pl.BlockSpec((tm, D), tok), pl.BlockSpec((tm, D), tok),
                  pl.BlockSpec((1, tm, D), lambda i: (6, i, 0)),
                  pl.BlockSpec((1, tm, D), lambda i: (7, i, 0)),
                  pl.BlockSpec((tm, D), tok),
                  pl.BlockSpec((1, 6, D), lambda i: ((i * tm) // S, 0, 0)),
                  pl.BlockSpec((D, D), const), pl.BlockSpec((D, D), const), pl.BlockSpec((D, D), const),
                  pl.BlockSpec((1, D), const), pl.BlockSpec((1, D), const)],
        out_specs=[pl.BlockSpec((tm, D), tok), pl.BlockSpec((D, tm), lambda i: (0, i))],
        out_shape=[jax.ShapeDtypeStruct((T, D), F32), jax.ShapeDtypeStruct((D, T), BF16)],
        compiler_params=_cparams(("parallel",)),
        name="merge",
    )(ya2, ym2, z3, z3, x2, mod3, wa, wm, wo, ln_g, ln_b)


PEER_TB = 256
_CAND_COLS = [[k1 for k1 in range(PEER_TOPK) if (k1 + 1) * (k2 + 1) <= PEER_TOPK] for k2 in range(PEER_TOPK)]


def _batcher_pairs(n):
    pairs = []
    p = 1
    while p < n:
        k = p
        while k >= 1:
            for j in range(k % p, n - k, 2 * k):
                for i in range(min(k, n - j - k)):
                    if (i + j) // (2 * p) == (i + j + k) // (2 * p):
                        pairs.append((i + j, i + j + k))
            k //= 2
        p *= 2
    return pairs


_SORT16 = _batcher_pairs(PEER_TOPK)


def _cmpx(v, i, j):
    a, b = v[i], v[j]
    if b is None:
        return
    if a is None:
        v[i], v[j] = b, None
        return
    v[i], v[j] = jnp.maximum(a, b), jnp.minimum(a, b)


def _sort16_desc(v):
    v = list(v)
    for i, j in _SORT16:
        _cmpx(v, i, j)
    return v


def _bitonic_merge_desc(v):
    v = list(v)
    d = PEER_TOPK // 2
    while d >= 1:
        for i in range(PEER_TOPK):
            if i & d == 0:
                _cmpx(v, i, i + d)
        d //= 2
    return v


def _half_clean(a, b):
    out = []
    for g in range(PEER_TOPK):
        x, y = a[g], b[PEER_TOPK - 1 - g]
        out.append(x if y is None else (y if x is None else jnp.maximum(x, y)))
    return out


def _top16_all_sublanes(s):
    rows = _sort16_desc([s[g * 8:(g + 1) * 8, :] for g in range(PEER_TOPK)])
    for shift in (4, 2, 1):
        partner = [pltpu.roll(r, shift, 0) for r in rows]
        rows = _bitonic_merge_desc(_half_clean(rows, partner))
    return rows


def _route_kernel(h_ref, wqt_ref, keys_ref, rk2_ref, e2_ref, c1_ref, e1_ref, q_scr, s_scr, a_scr, f_scr):
    half = PEER_DKEY // 2
    K = PEER_TOPK
    q_scr[...] = _dot(wqt_ref[...], h_ref[...]).astype(BF16)

    def stage_a(h, carry):
        base = pl.multiple_of(h * PEER_DKEY, PEER_DKEY)
        for p in range(2):
            qp = q_scr[pl.ds(base + p * half, half), :]
            s = _dot(keys_ref[p], qp)
            s_scr[p, h] = s
            top = _top16_all_sublanes(s)
            for k in range(K):
                a_scr[p, k, pl.ds(h, 1), :] = top[k][0:1, :]
        return carry

    lax.fori_loop(0, PEER_HEADS, stage_a, 0)

    a1 = [a_scr[0, k] for k in range(K)]
    a2 = [a_scr[1, k] for k in range(K)]
    cand = [[a1[k1] + a2[k2] for k1 in col] for k2, col in enumerate(_CAND_COLS)]
    g0 = [cand[k2][0] for k2 in range(K)]
    rest = [cand[k2][i] for k2 in range(K) for i in range(1, len(_CAND_COLS[k2]))]
    rest += [None] * (-len(rest) % K)
    groups = [g0] + [_sort16_desc(rest[i:i + K]) for i in range(0, len(rest), K)]
    while len(groups) > 2:
        merged = [_bitonic_merge_desc(_half_clean(groups[i], groups[i + 1])) for i in range(0, len(groups) - 1, 2)]
        groups = merged + ([groups[-1]] if len(groups) % 2 else [])
    last = [x for x in _half_clean(groups[0], groups[1]) if x is not None]
    tau = functools.reduce(jnp.minimum, last)
    cmax = a1[0] + a2[0]
    zsum = None
    for k2, col in enumerate(_CAND_COLS):
        phi = None
        for i, k1 in enumerate(col):
            c = cand[k2][i]
            hit = c >= tau
            term = jnp.where(hit, jnp.exp(c - cmax), 0.0)
            zsum = term if zsum is None else zsum + term
            lo = jnp.where(hit, a1[k1], jnp.inf)
            phi = lo if phi is None else jnp.minimum(phi, lo)
        f_scr[k2] = phi
    f_scr[K] = 1.0 / zsum

    def prefix_count(pred, thr):
        b8 = pred(thr[7])
        b4 = pred(jnp.where(b8, thr[11], thr[3]))
        b2 = pred(jnp.where(b8, jnp.where(b4, thr[13], thr[9]), jnp.where(b4, thr[5], thr[1])))
        b1 = pred(jnp.where(b8,
                            jnp.where(b4, jnp.where(b2, thr[14], thr[12]), jnp.where(b2, thr[10], thr[8])),
                            jnp.where(b4, jnp.where(b2, thr[6], thr[4]), jnp.where(b2, thr[2], thr[0]))))
        b0 = pred(thr[15])
        bit = lambda b, v: jnp.where(b, v, 0.0)
        return bit(b8, 8.0) + bit(b4, 4.0) + bit(b2, 2.0) + bit(b1, 1.0) + bit(b0, 1.0)

    def stage_c(h, carry):
        s1 = s_scr[0, h]
        s2 = s_scr[1, h]
        shape = s2.shape
        top2 = [jnp.broadcast_to(a_scr[1, k, pl.ds(h, 1), :], shape) for k in range(K)]
        phi = [jnp.broadcast_to(f_scr[k, pl.ds(h, 1), :], shape) for k in range(K)]
        rk2_ref[h] = prefix_count(lambda t: t > s2, top2).astype(BF16)
        c1_ref[h] = prefix_count(lambda t: s1 >= t, phi)
        e2_ref[h] = jnp.exp(s2 - a_scr[1, 0, pl.ds(h, 1), :]).astype(BF16)
        e1_ref[h] = jnp.exp(s1 - a_scr[0, 0, pl.ds(h, 1), :]) * f_scr[K, pl.ds(h, 1), :]
        return carry

    lax.fori_loop(0, PEER_HEADS, stage_c, 0)


def _peer_route(h2t, wqt, keys):
    D, T = h2t.shape
    tb = PEER_TB
    PH = PEER_HEADS
    blk = pl.BlockSpec((PH, N_KEYS, tb), lambda i: (0, 0, i))
    return pl.pallas_call(
        _route_kernel,
        grid=(T // tb,),
        in_specs=[pl.BlockSpec((D, tb), lambda i: (0, i)),
                  pl.BlockSpec((PH * PEER_DKEY, D), lambda i: (0, 0)),
                  pl.BlockSpec((2, N_KEYS, PEER_DKEY // 2), lambda i: (0, 0, 0))],
        out_specs=[blk, blk, blk, blk],
        out_shape=[jax.ShapeDtypeStruct((PH, N_KEYS, T), BF16),
                   jax.ShapeDtypeStruct((PH, N_KEYS, T), BF16),
                   jax.ShapeDtypeStruct((PH, N_KEYS, T), F32),
                   jax.ShapeDtypeStruct((PH, N_KEYS, T), F32)],
        scratch_shapes=[pltpu.VMEM((PH * PEER_DKEY, tb), BF16),
                        pltpu.VMEM((2, PH, N_KEYS, tb), F32),
                        pltpu.VMEM((2, PEER_TOPK, PH, tb), F32),
                        pltpu.VMEM((PEER_TOPK + 1, PH, tb), F32)],
        compiler_params=_cparams(("parallel",)),
        name="peer_route",
    )(h2t, wqt, keys)


EXP_TB = 512
EXP_EB = 1024


BF16_ROWS = 16
ACT_ROWS = 32
LANES = 256


def _experts_kernel(h_ref, u_ref, vt_ref, rk2_ref, e2_ref, c1_ref, e1_ref, x1_ref, mod_ref, g2_ref, b2_ref,
                    o_ref, acc_scr, at_scr, w0_scr, w1_scr):
    e = pl.program_id(1)
    ne = N_EXPERTS // EXP_EB
    nsub = EXP_EB // N_KEYS
    ngrp = N_KEYS // BF16_ROWS
    tb = h_ref.shape[1]
    nhalf = tb // LANES
    assert nhalf == 2 and nsub % nhalf == 0

    @pl.when(e == 0)
    def _():
        acc_scr[...] = jnp.zeros_like(acc_scr)
        w1_scr[...] = jnp.zeros_like(w1_scr)

    def value_matmul(w_read, half):
        cols = slice(half * LANES, (half + 1) * LANES)
        acc_scr[:, cols] += _dot(vt_ref[...], w_read[:, cols])

    def key_block(j, w_write):
        krows = slice(j * N_KEYS, (j + 1) * N_KEYS)
        at_scr[krows, :] = _dot(u_ref[krows, :], h_ref[...])
        for lt in range(nhalf):
            cols = slice(lt * LANES, (lt + 1) * LANES)
            gates = [None] * ngrp
            for h in range(PEER_HEADS):
                cnt = jnp.broadcast_to(c1_ref[h, j:j + 1, cols], (BF16_ROWS, LANES)).astype(BF16)
                e1 = jnp.broadcast_to(e1_ref[h, j:j + 1, cols], (BF16_ROWS, LANES)).astype(BF16)
                for r in range(ngrp):
                    rows = slice(r * BF16_ROWS, (r + 1) * BF16_ROWS)
                    term = jnp.where(rk2_ref[h, rows, cols] < cnt, e2_ref[h, rows, cols],
                                     jnp.zeros((), BF16)) * e1
                    gates[r] = term if gates[r] is None else gates[r] + term
            for r in range(ngrp):
                rows = slice(j * N_KEYS + r * BF16_ROWS, j * N_KEYS + (r + 1) * BF16_ROWS)
                a = at_scr[rows, cols]
                act = (0.5 * a * (1.0 + lax.erf(a * (2.0 ** -0.5)))).astype(BF16)
                w_write[rows, cols] = gates[r] * act

    def step(w_write, w_read):
        for half in range(nhalf):
            value_matmul(w_read, half)
            for j in range(half * nsub // nhalf, (half + 1) * nsub // nhalf):
                key_block(j, w_write)

    parity = lax.rem(e, 2)

    @pl.when((e < ne) & (parity == 0))
    def _():
        step(w0_scr, w1_scr)

    @pl.when((e < ne) & (parity == 1))
    def _():
        step(w1_scr, w0_scr)

    @pl.when(e == ne)
    def _():
        w_last = w1_scr if ne % 2 == 0 else w0_scr
        for half in range(nhalf):
            value_matmul(w_last, half)
        yf = acc_scr[...].T
        gt2 = mod_ref[0, 5:6, :]
        r = ALPHA * x1_ref[...] + gt2 * yf
        o_ref[...] = _layer_norm_rows(r) * g2_ref[...] + b2_ref[...]


def _peer_experts(h2t, u_b, vt_b, rk2, e2, c1, e1, x1, mod3, ln_g, ln_b, S):
    D, T = h2t.shape
    tb, eb = EXP_TB, EXP_EB
    PH = PEER_HEADS
    route = pl.BlockSpec((PH, N_KEYS, tb), lambda i, e: (0, 0, i))
    ne = N_EXPERTS // eb
    key1 = pl.BlockSpec((PH, eb // N_KEYS, tb), lambda i, e: (0, jnp.minimum(e, ne - 1), i))
    return pl.pallas_call(
        _experts_kernel,
        grid=(T // tb, ne + 1),
        in_specs=[pl.BlockSpec((D, tb), lambda i, e: (0, i)),
                  pl.BlockSpec((eb, D), lambda i, e: (jnp.minimum(e, ne - 1), 0)),
                  pl.BlockSpec((D, eb), lambda i, e: (0, jnp.maximum(e - 1, 0))),
                  route, route, key1, key1,
                  pl.BlockSpec((tb, D), lambda i, e: (i, 0)),
                  pl.BlockSpec((1, 6, D), lambda i, e: ((i * tb) // S, 0, 0)),
                  pl.BlockSpec((1, D), lambda i, e: (0, 0)),
                  pl.BlockSpec((1, D), lambda i, e: (0, 0))],
        out_specs=pl.BlockSpec((tb, D), lambda i, e: (i, 0)),
        out_shape=jax.ShapeDtypeStruct((T, D), F32),
        scratch_shapes=[pltpu.VMEM((D, tb), F32), pltpu.VMEM((eb, tb), F32),
                        pltpu.VMEM((eb, tb), BF16), pltpu.VMEM((eb, tb), BF16)],
        compiler_params=_cparams(("parallel", "arbitrary")),
        name="peer_experts",
    )(h2t, u_b, vt_b, rk2, e2, c1, e1, x1, mod3, ln_g, ln_b)


def kernel(x, c, w_ada, b_ada, w_in, b_if, conv_w, conv_b, da_lambda, da_subln_g, ml_norm_g, w_br_attn,
           w_br_mlstm, w_out, ln1_g, ln1_b, peer_wq, peer_keys, peer_u, peer_v, ln2_g, ln2_b):
    B, S, D = x.shape
    T = B * S
    assert D == D_MODEL and S % DA_TQ == 0 and S % 1024 == 0
    l = 0
    lambda_init = 0.8 - 0.6 * math.exp(-0.3 * l)

    mod3 = _modulation(c, w_ada[l], b_ada[l]).reshape(B, 6, D)

    w = w_in[l]
    o_mq = 3 * D
    o_mv = o_mq + 2 * ML_HEADS * ML_DK
    o_mo = o_mv + D
    o_if = o_mo + D
    o_ga = o_if + 2 * ML_HEADS
    o_gm = o_ga + D
    starts = (0, D, 2 * D, o_mq, o_mv, o_mo, o_ga, o_gm)
    w8 = jnp.stack([w[:, s0:s0 + D] for s0 in starts]).astype(BF16)
    w_if = w[:, o_if:o_if + 2 * ML_HEADS]
    wg = jnp.pad(w_if, ((0, 0), (0, 128 - 2 * ML_HEADS))).astype(BF16)
    wgt = w_if.T.astype(BF16)
    bias8 = b_if[l].reshape(2 * ML_HEADS)
    bcol = jnp.pad(bias8, (0, 128 - 2 * ML_HEADS)).reshape(1, 128)
    brow = bias8.reshape(2 * ML_HEADS, 1)

    x2 = x.reshape(T, D)
    z, gcol, grow = _in_proj(x2, mod3, w8, wg, wgt, bcol, brow, S)
    z4 = z.reshape(8, B, S, D)

    ya = _diff_attention(z4, da_lambda[l], da_subln_g[l], B, S, lambda_init)
    ym = _mlstm(z4, gcol.reshape(B, S, 128), grow, conv_w[l], conv_b[l].reshape(1, -1),
                ml_norm_g[l].reshape(1, -1), B, S)

    x1, h2 = _merge(ya.reshape(T, D), ym.reshape(T, D), z, x2, mod3,
                    w_br_attn[l].astype(BF16), w_br_mlstm[l].astype(BF16), w_out[l].astype(BF16),
                    ln1_g[l].reshape(1, D), ln1_b[l].reshape(1, D), S)

    rk2, e2, c1, e1 = _peer_route(h2, peer_wq[l].T.astype(BF16), peer_keys[l].astype(BF16))
    out = _peer_experts(h2, peer_u[l].astype(BF16), peer_v[l].T.astype(BF16), rk2, e2, c1, e1, x1, mod3,
                        ln2_g[l].reshape(1, D), ln2_b[l].reshape(1, D), S)
    return out.reshape(B, S, D)
```

```python
import functools
import math

import jax
import jax.numpy as jnp
from jax import lax
from jax.experimental import pallas as pl
from jax.experimental.pallas import tpu as pltpu

D_MODEL = 1024
DA_HEADS = 8
DA_DK = 64
DA_DV = 2 * DA_DK
ML_HEADS = 4
ML_DK = 128
ML_DV = 256
ML_CHUNK = 128
CONV_K = 4
PEER_HEADS = 8
PEER_TOPK = 16
N_KEYS = 128
N_EXPERTS = N_KEYS * N_KEYS
PEER_DKEY = 128
DEPTH = 1
ALPHA = (2 * DEPTH) ** 0.25
LN_EPS = 1e-5

F32 = jnp.float32
BF16 = jnp.bfloat16
NEG_INF = float("-inf")

VMEM_LIMIT_BYTES = 56 * 1024 * 1024


def _cparams(sem):
    return pltpu.CompilerParams(dimension_semantics=sem, vmem_limit_bytes=VMEM_LIMIT_BYTES)


def _layer_norm_rows(x):
    mu = jnp.mean(x, axis=-1, keepdims=True)
    xc = x - mu
    var = jnp.mean(xc * xc, axis=-1, keepdims=True)
    return xc * lax.rsqrt(var + LN_EPS)


def _dot(a, b):
    return jnp.dot(a, b, preferred_element_type=F32)


def _dot_nt(a, b):
    return lax.dot_general(a, b, (((1,), (1,)), ((), ())), preferred_element_type=F32)


def _dot_tn(a, b):
    return lax.dot_general(a, b, (((0,), (0,)), ((), ())), preferred_element_type=F32)


def _mod_kernel(c_ref, w_ref, b_ref, o_ref):
    c = c_ref[...]
    a = c * jax.nn.sigmoid(c)
    o_ref[...] = jnp.dot(a, w_ref[...], preferred_element_type=F32,
                         precision=lax.Precision.HIGHEST) + b_ref[...]


def _modulation(c, w_ada, b_ada):
    B, D = c.shape
    N = w_ada.shape[1]
    tn = 1024
    return pl.pallas_call(
        _mod_kernel,
        grid=(N // tn,),
        in_specs=[pl.BlockSpec((B, D), lambda n: (0, 0)),
                  pl.BlockSpec((D, tn), lambda n: (0, n)),
                  pl.BlockSpec((1, tn), lambda n: (0, n))],
        out_specs=pl.BlockSpec((B, tn), lambda n: (0, n)),
        out_shape=jax.ShapeDtypeStruct((B, N), F32),
        compiler_params=_cparams(("arbitrary",)),
        name="modulation",
    )(c, w_ada, b_ada.reshape(1, N))


def _inproj_kernel(x_ref, mod_ref, w_ref, wg_ref, wgt_ref, bcol_ref, brow_ref,
                   z_ref, gcol_ref, grow_ref, h_scr):
    n = pl.program_id(1)

    @pl.when(n == 0)
    def _():
        hn = _layer_norm_rows(x_ref[...])
        sh1 = mod_ref[0, 0:1, :]
        sc1 = mod_ref[0, 1:2, :]
        hb = (hn * (1.0 + sc1) + sh1).astype(BF16)
        h_scr[...] = hb
        gcol_ref[...] = _dot(hb, wg_ref[...]) + bcol_ref[...]
        grow_ref[...] = _dot_nt(wgt_ref[...], hb) + brow_ref[...]

    z_ref[0] = _dot(h_scr[...], w_ref[0]).astype(BF16)


def _in_proj(x2, mod3, w8, wg, wgt, bcol, brow, S):
    T, D = x2.shape
    tm = 1024
    npiece = w8.shape[0]
    return pl.pallas_call(
        _inproj_kernel,
        grid=(T // tm, npiece),
        in_specs=[pl.BlockSpec((tm, D), lambda i, n: (i, 0)),
                  pl.BlockSpec((1, 6, D), lambda i, n: ((i * tm) // S, 0, 0)),
                  pl.BlockSpec((1, D, D), lambda i, n: (n, 0, 0)),
                  pl.BlockSpec((D, 128), lambda i, n: (0, 0)),
                  pl.BlockSpec((8, D), lambda i, n: (0, 0)),
                  pl.BlockSpec((1, 128), lambda i, n: (0, 0)),
                  pl.BlockSpec((8, 1), lambda i, n: (0, 0))],
        out_specs=[pl.BlockSpec((1, tm, D), lambda i, n: (n, i, 0)),
                   pl.BlockSpec((tm, 128), lambda i, n: (i, 0)),
                   pl.BlockSpec((8, tm), lambda i, n: (0, i))],
        out_shape=[jax.ShapeDtypeStruct((npiece, T, D), BF16),
                   jax.ShapeDtypeStruct((T, 128), F32),
                   jax.ShapeDtypeStruct((8, T), F32)],
        scratch_shapes=[pltpu.VMEM((tm, D), BF16)],
        compiler_params=_cparams(("parallel", "arbitrary")),
        name="in_proj",
    )(x2, mod3, w8, wg, wgt, bcol, brow)


DA_TQ = 256


def _diffattn_kernel(lam_ref, g_ref, q_ref, k_ref, v_ref, o_ref, s_scr, *, S, lambda_init):
    tq = DA_TQ
    nq = S // tq
    lam = lam_ref[...]
    t1 = jnp.sum(lam[0:1] * lam[1:2], axis=-1, keepdims=True)
    t2 = jnp.sum(lam[2:3] * lam[3:4], axis=-1, keepdims=True)
    lam_val = jnp.exp(t1) - jnp.exp(t2) + lambda_init
    first_map = lax.broadcasted_iota(jnp.int32, (1, DA_DV), 1) < DA_DK
    gain = g_ref[...] * (1.0 - lambda_init)
    row = lax.broadcasted_iota(jnp.int32, (tq, tq), 0)
    col = lax.broadcasted_iota(jnp.int32, (tq, tq), 1)
    causal = col <= row

    def scores(qi):
        q0 = qi * tq
        qs = q_ref[0, 0, q0:q0 + tq, :] * (DA_DK ** -0.5)
        zero = jnp.zeros_like(qs)
        for mp, qm in enumerate((jnp.where(first_map, qs, zero), jnp.where(first_map, zero, qs))):
            s_scr[qi % 2, mp, :, 0:q0 + tq] = _dot_nt(qm, k_ref[0, 0, 0:q0 + tq, :])

    def finish(qi):
        q0 = qi * tq
        slot = qi % 2
        probs = []
        for mp in range(2):
            s_diag = jnp.where(causal, s_scr[slot, mp, :, q0:q0 + tq], NEG_INF)
            m = jnp.max(s_diag, axis=-1, keepdims=True)
            if qi > 0:
                s_off = s_scr[slot, mp, :, 0:q0]
                m = jnp.maximum(m, jnp.max(s_off, axis=-1, keepdims=True))
            p_diag = jnp.exp(s_diag - m)
            l = jnp.sum(p_diag, axis=-1, keepdims=True)
            p_off = None
            if qi > 0:
                p_off = jnp.exp(s_off - m)
                l = l + jnp.sum(p_off, axis=-1, keepdims=True)
            probs.append((p_diag, p_off, l))
        (p1d, p1o, l1), (p2d, p2o, l2) = probs
        w1 = 1.0 / l1
        w2 = lam_val / l2
        o = _dot((p1d * w1 - p2d * w2).astype(BF16), v_ref[0, 0, q0:q0 + tq, :])
        if qi > 0:
            o = o + _dot((p1o * w1 - p2o * w2).astype(BF16), v_ref[0, 0, 0:q0, :])
        o = o * lax.rsqrt(jnp.mean(o * o, axis=-1, keepdims=True) + LN_EPS) * gain
        o_ref[0, q0:q0 + tq, :] = o.astype(BF16)

    scores(0)
    for qi in range(nq):
        if qi + 1 < nq:
            scores(qi + 1)
        finish(qi)


def _diff_attention(z4, da_lambda, subln_g, B, S, lambda_init):
    kern = functools.partial(_diffattn_kernel, S=S, lambda_init=lambda_init)
    return pl.pallas_call(
        kern,
        grid=(B, DA_HEADS),
        in_specs=[pl.BlockSpec((4, DA_DK), lambda b, h: (0, 0)),
                  pl.BlockSpec((1, DA_DV), lambda b, h: (0, 0)),
                  pl.BlockSpec((1, 1, S, DA_DV), lambda b, h: (0, b, 0, h)),
                  pl.BlockSpec((1, 1, S, DA_DV), lambda b, h: (1, b, 0, h)),
                  pl.BlockSpec((1, 1, S, DA_DV), lambda b, h: (2, b, 0, h))],
        out_specs=pl.BlockSpec((1, S, DA_DV), lambda b, h: (b, 0, h)),
        out_shape=jax.ShapeDtypeStruct((B, S, DA_HEADS * DA_DV), BF16),
        scratch_shapes=[pltpu.VMEM((2, 2, DA_TQ, S), F32)],
        compiler_params=_cparams(("parallel", "parallel")),
        name="diff_attention",
    )(da_lambda, subln_g.reshape(1, DA_DV), z4, z4, z4)


def _mlstm_kernel(qk_ref, v_ref, og_ref, gcol_ref, grow_ref, cw_ref, cb_ref, ng_ref, o_ref,
                  qc_scr, kc_scr, c_scr, n_scr, m_scr, *, S):
    L = ML_CHUNK
    H = ML_HEADS
    srow = lax.broadcasted_iota(jnp.int32, (S, ML_DK), 0)

    for cb in range(2 * H):
        cols = slice(cb * ML_DK, (cb + 1) * ML_DK)
        x = qk_ref[0, 0, :, cols].astype(F32)
        y = x * cw_ref[CONV_K - 1:CONV_K, cols] + cb_ref[:, cols]
        for j in range(1, CONV_K):
            xs = jnp.where(srow >= j, pltpu.roll(x, j, 0), 0.0)
            y = y + xs * cw_ref[CONV_K - 1 - j:CONV_K - j, cols]
        y = y * jax.nn.sigmoid(y)
        if cb < H:
            qc_scr[:, cols] = y
        else:
            kc_scr[:, (cb - H) * ML_DK:(cb - H + 1) * ML_DK] = y * (ML_DK ** -0.5)

    r_i = lax.broadcasted_iota(jnp.int32, (L, L), 0)
    c_i = lax.broadcasted_iota(jnp.int32, (L, L), 1)
    causal = c_i <= r_i
    tril = causal.astype(F32)
    triu = (r_i <= c_i).astype(F32)
    c_scr[...] = jnp.zeros_like(c_scr)
    n_scr[...] = jnp.zeros_like(n_scr)
    m_scr[...] = jnp.zeros_like(m_scr)

    def head_chunk(hh, t0, gc, gr, b_cols, b_rows):
        Ct = c_scr[hh]
        n_row = n_scr[hh]
        m = m_scr[hh]
        q = qc_scr[pl.ds(t0, L), hh * ML_DK:(hh + 1) * ML_DK]
        k = kc_scr[pl.ds(t0, L), hh * ML_DK:(hh + 1) * ML_DK]
        v = v_ref[0, 0, pl.ds(t0, L), hh * ML_DV:(hh + 1) * ML_DV]
        ngain = ng_ref[:, hh * ML_DV:(hh + 1) * ML_DV]
        ig_col = gc[:, hh:hh + 1]
        ig_row = gr[hh:hh + 1, :]
        b_col = b_cols[:, H + hh:H + hh + 1]
        b_row = b_rows[H + hh:H + hh + 1, :]
        dm = jnp.where(causal, b_col - b_row + ig_row, NEG_INF)
        m_inter = b_col + m
        m_t = jnp.maximum(m_inter, jnp.max(dm, axis=-1, keepdims=True))
        w = jnp.exp(dm - m_t)
        qb = q.astype(BF16)
        kb = k.astype(BF16)
        p = w * _dot_nt(qb, kb)
        inter = jnp.exp(m_inter - m_t)
        num = _dot(p.astype(BF16), v) + inter * _dot(qb, Ct.astype(BF16))
        nq = jnp.sum(p, axis=-1, keepdims=True) + inter * jnp.sum(q * n_row, axis=-1, keepdims=True)
        hout = num / jnp.maximum(jnp.abs(nq), jnp.exp(-m_t))
        hout = hout * lax.rsqrt(jnp.mean(hout * hout, axis=-1, keepdims=True) + LN_EPS) * ngain
        og = og_ref[0, 0, pl.ds(t0, L), hh * ML_DV:(hh + 1) * ML_DV].astype(F32)
        o_ref[0, pl.ds(t0, L), hh * ML_DV:(hh + 1) * ML_DV] = (hout * jax.nn.sigmoid(og)).astype(BF16)
        m_new = m_t[L - 1:L, :]
        b_last = b_col[L - 1:L, :]
        decay = jnp.exp(b_last + m - m_new)
        w_s = jnp.exp(b_last - b_col + ig_col - m_new)
        c_scr[hh] = decay * Ct + _dot_tn(kb, (v.astype(F32) * w_s).astype(BF16))
        n_scr[hh] = decay * n_row + jnp.sum(k * w_s, axis=0, keepdims=True)
        m_scr[hh] = m_new

    def chunk(ci, carry):
        t0 = pl.multiple_of(ci * L, L)
        gc = gcol_ref[0, pl.ds(t0, L), :]
        gr = grow_ref[:, pl.ds(t0, L)]
        b_cols = jnp.dot(tril, jax.nn.log_sigmoid(gc), preferred_element_type=F32,
                         precision=lax.Precision.HIGHEST)
        b_rows = jnp.dot(jax.nn.log_sigmoid(gr), triu, preferred_element_type=F32,
                         precision=lax.Precision.HIGHEST)
        for hh in range(H):
            head_chunk(hh, t0, gc, gr, b_cols, b_rows)
        return carry

    lax.fori_loop(0, S // L, chunk, 0)


def _mlstm(z4, gcol3, grow, conv_w, conv_b, norm_g, B, S):
    kern = functools.partial(_mlstm_kernel, S=S)
    H = ML_HEADS
    D = H * ML_DV
    piece = lambda n: pl.BlockSpec((1, 1, S, D), lambda b: (n, b, 0, 0))
    return pl.pallas_call(
        kern,
        grid=(B,),
        in_specs=[piece(3), piece(4), piece(5),
                  pl.BlockSpec((1, S, 128), lambda b: (b, 0, 0)),
                  pl.BlockSpec((8, S), lambda b: (0, b)),
                  pl.BlockSpec((CONV_K, 2 * H * ML_DK), lambda b: (0, 0)),
                  pl.BlockSpec((1, 2 * H * ML_DK), lambda b: (0, 0)),
                  pl.BlockSpec((1, D), lambda b: (0, 0))],
        out_specs=pl.BlockSpec((1, S, D), lambda b: (b, 0, 0)),
        out_shape=jax.ShapeDtypeStruct((B, S, D), BF16),
        scratch_shapes=[pltpu.VMEM((S, H * ML_DK), F32), pltpu.VMEM((S, H * ML_DK), F32),
                        pltpu.VMEM((H, ML_DK, ML_DV), F32), pltpu.VMEM((H, 1, ML_DK), F32),
                        pltpu.VMEM((H, 1, 1), F32)],
        compiler_params=_cparams(("parallel",)),
        name="mlstm",
    )(z4, z4, z4, gcol3, grow, conv_w, conv_b, norm_g)


def _merge_kernel(ya_ref, ym_ref, ga_ref, gm_ref, x_ref, mod_ref, wa_ref, wm_ref, wo_ref, g1_ref, b1_ref,
                  x1_ref, h2_ref):
    ya = _dot(ya_ref[...], wa_ref[...])
    ym = _dot(ym_ref[...], wm_ref[...])
    y = (jax.nn.sigmoid(ga_ref[0].astype(F32)) * ya + jax.nn.sigmoid(gm_ref[0].astype(F32)) * ym)
    y2 = _dot(y.astype(BF16), wo_ref[...])
    gt1 = mod_ref[0, 2:3, :]
    sh2 = mod_ref[0, 3:4, :]
    sc2 = mod_ref[0, 4:5, :]
    x1 = _layer_norm_rows(ALPHA * x_ref[...] + gt1 * y2) * g1_ref[...] + b1_ref[...]
    x1_ref[...] = x1
    h2_ref[...] = (_layer_norm_rows(x1) * (1.0 + sc2) + sh2).T.astype(BF16)


def _merge(ya2, ym2, z3, x2, mod3, wa, wm, wo, ln_g, ln_b, S):
    T, D = x2.shape
    tm = 512
    tok = lambda i: (i, 0)
    const = lambda i: (0, 0)
    return pl.pallas_call(
        _merge_kernel,
        grid=(T // tm,),
        in_specs=[pl.BlockSpec((tm, D), tok), pl.BlockSpec((tm, D), tok),
                  pl.BlockSpec((1, tm, D), lambda i: (6, i, 0)),
                  pl.BlockSpec((1, tm, D), lambda i: (7, i, 0)),
                  pl.BlockSpec((tm, D), tok),
                  pl.BlockSpec((1, 6, D), lambda i: ((i * tm) // S, 0, 0)),
                  pl.BlockSpec((D, D), const), pl.BlockSpec((D, D), const), pl.BlockSpec((D, D), const),
                  pl.BlockSpec((1, D), const), pl.BlockSpec((1, D), const)],
        out_specs=[pl.BlockSpec((tm, D), tok), pl.BlockSpec((D, tm), lambda i: (0, i))],
        out_shape=[jax.ShapeDtypeStruct((T, D), F32), jax.ShapeDtypeStruct((D, T), BF16)],
        compiler_params=_cparams(("parallel",)),
        name="merge",
    )(ya2, ym2, z3, z3, x2, mod3, wa, wm, wo, ln_g, ln_b)


PEER_TB = 256
_CAND_COLS = [[k1 for k1 in range(PEER_TOPK) if (k1 + 1) * (k2 + 1) <= PEER_TOPK] for k2 in range(PEER_TOPK)]


def _batcher_pairs(n):
    pairs = []
    p = 1
    while p < n:
        k = p
        while k >= 1:
            for j in range(k % p, n - k, 2 * k):
                for i in range(min(k, n - j - k)):
                    if (i + j) // (2 * p) == (i + j + k) // (2 * p):
                        pairs.append((i + j, i + j + k))
            k //= 2
        p *= 2
    return pairs


_SORT16 = _batcher_pairs(PEER_TOPK)


def _cmpx(v, i, j):
    a, b = v[i], v[j]
    if b is None:
        return
    if a is None:
        v[i], v[j] = b, None
        return
    v[i], v[j] = jnp.maximum(a, b), jnp.minimum(a, b)


def _sort16_desc(v):
    v = list(v)
    for i, j in _SORT16:
        _cmpx(v, i, j)
    return v


def _bitonic_merge_desc(v):
    v = list(v)
    d = PEER_TOPK // 2
    while d >= 1:
        for i in range(PEER_TOPK):
            if i & d == 0:
                _cmpx(v, i, i + d)
        d //= 2
    return v


def _half_clean(a, b):
    out = []
    for g in range(PEER_TOPK):
        x, y = a[g], b[PEER_TOPK - 1 - g]
        out.append(x if y is None else (y if x is None else jnp.maximum(x, y)))
    return out


def _top16_all_sublanes(s):
    rows = _sort16_desc([s[g * 8:(g + 1) * 8, :] for g in range(PEER_TOPK)])
    for shift in (4, 2, 1):
        partner = [pltpu.roll(r, shift, 0) for r in rows]
        rows = _bitonic_merge_desc(_half_clean(rows, partner))
    return rows


def _route_kernel(h_ref, wqt_ref, keys_ref, rk2_ref, e2_ref, c1_ref, e1_ref, q_scr, s_scr, a_scr, f_scr):
    half = PEER_DKEY // 2
    K = PEER_TOPK
    q_scr[...] = _dot(wqt_ref[...], h_ref[...]).astype(BF16)

    def stage_a(h, carry):
        base = pl.multiple_of(h * PEER_DKEY, PEER_DKEY)
        scores = [_dot(keys_ref[p], q_scr[pl.ds(base + p * half, half), :]) for p in range(2)]
        for p in range(2):
            s_scr[p, h] = scores[p]
            top = _top16_all_sublanes(scores[p])
            for k in range(K):
                a_scr[p, k, pl.ds(h, 1), :] = top[k][0:1, :]
        return carry

    lax.fori_loop(0, PEER_HEADS, stage_a, 0)

    a1 = [a_scr[0, k] for k in range(K)]
    a2 = [a_scr[1, k] for k in range(K)]
    cand = [[a1[k1] + a2[k2] for k1 in col] for k2, col in enumerate(_CAND_COLS)]
    g0 = [cand[k2][0] for k2 in range(K)]
    rest = [cand[k2][i] for k2 in range(K) for i in range(1, len(_CAND_COLS[k2]))]
    rest += [None] * (-len(rest) % K)
    groups = [g0] + [_sort16_desc(rest[i:i + K]) for i in range(0, len(rest), K)]
    while len(groups) > 2:
        merged = [_bitonic_merge_desc(_half_clean(groups[i], groups[i + 1])) for i in range(0, len(groups) - 1, 2)]
        groups = merged + ([groups[-1]] if len(groups) % 2 else [])
    last = [x for x in _half_clean(groups[0], groups[1]) if x is not None]
    tau = functools.reduce(jnp.minimum, last)
    cmax = a1[0] + a2[0]
    zsum = None
    for k2, col in enumerate(_CAND_COLS):
        phi = None
        for i, k1 in enumerate(col):
            c = cand[k2][i]
            hit = c >= tau
            term = jnp.where(hit, jnp.exp(c - cmax), 0.0)
            zsum = term if zsum is None else zsum + term
            lo = jnp.where(hit, a1[k1], jnp.inf)
            phi = lo if phi is None else jnp.minimum(phi, lo)
        f_scr[k2] = phi
    f_scr[K] = 1.0 / zsum

    def prefix_count(pred, thr):
        b8 = pred(thr[7])
        b4 = pred(jnp.where(b8, thr[11], thr[3]))
        b2 = pred(jnp.where(b8, jnp.where(b4, thr[13], thr[9]), jnp.where(b4, thr[5], thr[1])))
        b1 = pred(jnp.where(b8,
                            jnp.where(b4, jnp.where(b2, thr[14], thr[12]), jnp.where(b2, thr[10], thr[8])),
                            jnp.where(b4, jnp.where(b2, thr[6], thr[4]), jnp.where(b2, thr[2], thr[0]))))
        b0 = pred(thr[15])
        bit = lambda b, v: jnp.where(b, v, 0.0)
        return bit(b8, 8.0) + bit(b4, 4.0) + bit(b2, 2.0) + bit(b1, 1.0) + bit(b0, 1.0)

    def stage_c(h, carry):
        s1 = s_scr[0, h]
        s2 = s_scr[1, h]
        shape = s2.shape
        top2 = [jnp.broadcast_to(a_scr[1, k, pl.ds(h, 1), :], shape) for k in range(K)]
        phi = [jnp.broadcast_to(f_scr[k, pl.ds(h, 1), :], shape) for k in range(K)]
        rk2_ref[h] = prefix_count(lambda t: t > s2, top2).astype(BF16)
        c1_ref[h] = prefix_count(lambda t: s1 >= t, phi)
        e2_ref[h] = jnp.exp(s2 - a_scr[1, 0, pl.ds(h, 1), :]).astype(BF16)
        e1_ref[h] = jnp.exp(s1 - a_scr[0, 0, pl.ds(h, 1), :]) * f_scr[K, pl.ds(h, 1), :]
        return carry

    lax.fori_loop(0, PEER_HEADS, stage_c, 0)


def _peer_route(h2t, wqt, keys):
    D, T = h2t.shape
    tb = PEER_TB
    PH = PEER_HEADS
    blk = pl.BlockSpec((PH, N_KEYS, tb), lambda i: (0, 0, i))
    return pl.pallas_call(
        _route_kernel,
        grid=(T // tb,),
        in_specs=[pl.BlockSpec((D, tb), lambda i: (0, i)),
                  pl.BlockSpec((PH * PEER_DKEY, D), lambda i: (0, 0)),
                  pl.BlockSpec((2, N_KEYS, PEER_DKEY // 2), lambda i: (0, 0, 0))],
        out_specs=[blk, blk, blk, blk],
        out_shape=[jax.ShapeDtypeStruct((PH, N_KEYS, T), BF16),
                   jax.ShapeDtypeStruct((PH, N_KEYS, T), BF16),
                   jax.ShapeDtypeStruct((PH, N_KEYS, T), F32),
                   jax.ShapeDtypeStruct((PH, N_KEYS, T), F32)],
        scratch_shapes=[pltpu.VMEM((PH * PEER_DKEY, tb), BF16),
                        pltpu.VMEM((2, PH, N_KEYS, tb), F32),
                        pltpu.VMEM((2, PEER_TOPK, PH, tb), F32),
                        pltpu.VMEM((PEER_TOPK + 1, PH, tb), F32)],
        compiler_params=_cparams(("parallel",)),
        name="peer_route",
    )(h2t, wqt, keys)


EXP_TB = 512
EXP_EB = 1024


BF16_ROWS = 16
ACT_ROWS = 32
LANES = 256


def _experts_kernel(h_ref, u_ref, vt_ref, rk2_ref, e2_ref, c1_ref, e1_ref, x1_ref, mod_ref, g2_ref, b2_ref,
                    o_ref, acc_scr, at_scr, w0_scr, w1_scr):
    e = pl.program_id(1)
    ne = N_EXPERTS // EXP_EB
    nsub = EXP_EB // N_KEYS
    ngrp = N_KEYS // BF16_ROWS
    tb = h_ref.shape[1]
    nhalf = tb // LANES
    assert nhalf == 2 and nsub % nhalf == 0

    @pl.when(e == 0)
    def _():
        acc_scr[...] = jnp.zeros_like(acc_scr)
        w1_scr[...] = jnp.zeros_like(w1_scr)

    def value_matmul(w_read, half):
        cols = slice(half * LANES, (half + 1) * LANES)
        acc_scr[:, cols] += _dot(vt_ref[...], w_read[:, cols])

    def key_block(j, w_write):
        krows = slice(j * N_KEYS, (j + 1) * N_KEYS)
        at_scr[krows, :] = _dot(u_ref[krows, :], h_ref[...])
        for lt in range(nhalf):
            cols = slice(lt * LANES, (lt + 1) * LANES)
            gates = [None] * ngrp
            for h in range(PEER_HEADS):
                cnt = jnp.broadcast_to(c1_ref[h, j:j + 1, cols], (BF16_ROWS, LANES)).astype(BF16)
                e1 = jnp.broadcast_to(e1_ref[h, j:j + 1, cols], (BF16_ROWS, LANES)).astype(BF16)
                for r in range(ngrp):
                    rows = slice(r * BF16_ROWS, (r + 1) * BF16_ROWS)
                    term = jnp.where(rk2_ref[h, rows, cols] < cnt, e2_ref[h, rows, cols],
                                     jnp.zeros((), BF16)) * e1
                    gates[r] = term if gates[r] is None else gates[r] + term
            for r in range(ngrp):
                rows = slice(j * N_KEYS + r * BF16_ROWS, j * N_KEYS + (r + 1) * BF16_ROWS)
                a = at_scr[rows, cols]
                act = (0.5 * a * (1.0 + lax.erf(a * (2.0 ** -0.5)))).astype(BF16)
                w_write[rows, cols] = gates[r] * act

    def step(w_write, w_read):
        for half in range(nhalf):
            value_matmul(w_read, half)
            for j in range(half * nsub // nhalf, (half + 1) * nsub // nhalf):
                key_block(j, w_write)

    parity = lax.rem(e, 2)

    @pl.when((e < ne) & (parity == 0))
    def _():
        step(w0_scr, w1_scr)

    @pl.when((e < ne) & (parity == 1))
    def _():
        step(w1_scr, w0_scr)

    @pl.when(e == ne)
    def _():
        w_last = w1_scr if ne % 2 == 0 else w0_scr
        for half in range(nhalf):
            value_matmul(w_last, half)
        yf = acc_scr[...].T
        gt2 = mod_ref[0, 5:6, :]
        r = ALPHA * x1_ref[...] + gt2 * yf
        o_ref[...] = _layer_norm_rows(r) * g2_ref[...] + b2_ref[...]


def _peer_experts(h2t, u_b, vt_b, rk2, e2, c1, e1, x1, mod3, ln_g, ln_b, S):
    D, T = h2t.shape
    tb, eb = EXP_TB, EXP_EB
    PH = PEER_HEADS
    route = pl.BlockSpec((PH, N_KEYS, tb), lambda i, e: (0, 0, i))
    ne = N_EXPERTS // eb
    key1 = pl.BlockSpec((PH, eb // N_KEYS, tb), lambda i, e: (0, jnp.minimum(e, ne - 1), i))
    return pl.pallas_call(
        _experts_kernel,
        grid=(T // tb, ne + 1),
        in_specs=[pl.BlockSpec((D, tb), lambda i, e: (0, i)),
                  pl.BlockSpec((eb, D), lambda i, e: (jnp.minimum(e, ne - 1), 0)),
                  pl.BlockSpec((D, eb), lambda i, e: (0, jnp.maximum(e - 1, 0))),
                  route, route, key1, key1,
                  pl.BlockSpec((tb, D), lambda i, e: (i, 0)),
                  pl.BlockSpec((1, 6, D), lambda i, e: ((i * tb) // S, 0, 0)),
                  pl.BlockSpec((1, D), lambda i, e: (0, 0)),
                  pl.BlockSpec((1, D), lambda i, e: (0, 0))],
        out_specs=pl.BlockSpec((tb, D), lambda i, e: (i, 0)),
        out_shape=jax.ShapeDtypeStruct((T, D), F32),
        scratch_shapes=[pltpu.VMEM((D, tb), F32), pltpu.VMEM((eb, tb), F32),
                        pltpu.VMEM((eb, tb), BF16), pltpu.VMEM((eb, tb), BF16)],
        compiler_params=_cparams(("parallel", "arbitrary")),
        name="peer_experts",
    )(h2t, u_b, vt_b, rk2, e2, c1, e1, x1, mod3, ln_g, ln_b)


def kernel(x, c, w_ada, b_ada, w_in, b_if, conv_w, conv_b, da_lambda, da_subln_g, ml_norm_g, w_br_attn,
           w_br_mlstm, w_out, ln1_g, ln1_b, peer_wq, peer_keys, peer_u, peer_v, ln2_g, ln2_b):
    B, S, D = x.shape
    T = B * S
    assert D == D_MODEL and S % DA_TQ == 0 and S % 1024 == 0
    l = 0
    lambda_init = 0.8 - 0.6 * math.exp(-0.3 * l)

    mod3 = _modulation(c, w_ada[l], b_ada[l]).reshape(B, 6, D)

    w = w_in[l]
    o_mq = 3 * D
    o_mv = o_mq + 2 * ML_HEADS * ML_DK
    o_mo = o_mv + D
    o_if = o_mo + D
    o_ga = o_if + 2 * ML_HEADS
    o_gm = o_ga + D
    starts = (0, D, 2 * D, o_mq, o_mv, o_mo, o_ga, o_gm)
    w8 = jnp.stack([w[:, s0:s0 + D] for s0 in starts]).astype(BF16)
    w_if = w[:, o_if:o_if + 2 * ML_HEADS]
    wg = jnp.pad(w_if, ((0, 0), (0, 128 - 2 * ML_HEADS))).astype(BF16)
    wgt = w_if.T.astype(BF16)
    bias8 = b_if[l].reshape(2 * ML_HEADS)
    bcol = jnp.pad(bias8, (0, 128 - 2 * ML_HEADS)).reshape(1, 128)
    brow = bias8.reshape(2 * ML_HEADS, 1)

    x2 = x.reshape(T, D)
    z, gcol, grow = _in_proj(x2, mod3, w8, wg, wgt, bcol, brow, S)
    z4 = z.reshape(8, B, S, D)

    ya = _diff_attention(z4, da_lambda[l], da_subln_g[l], B, S, lambda_init)
    ym = _mlstm(z4, gcol.reshape(B, S, 128), grow, conv_w[l], conv_b[l].reshape(1, -1),
                ml_norm_g[l].reshape(1, -1), B, S)

    x1, h2 = _merge(ya.reshape(T, D), ym.reshape(T, D), z, x2, mod3,
                    w_br_attn[l].astype(BF16), w_br_mlstm[l].astype(BF16), w_out[l].astype(BF16),
                    ln1_g[l].reshape(1, D), ln1_b[l].reshape(1, D), S)

    rk2, e2, c1, e1 = _peer_route(h2, peer_wq[l].T.astype(BF16), peer_keys[l].astype(BF16))
    out = _peer_experts(h2, peer_u[l].astype(BF16), peer_v[l].T.astype(BF16), rk2, e2, c1, e1, x1, mod3,
                        ln2_g[l].reshape(1, D), ln2_b[l].reshape(1, D), S)
    return out.reshape(B, S, D)
```

```python
import functools
import math

import jax
import jax.numpy as jnp
from jax import lax
from jax.experimental import pallas as pl
from jax.experimental.pallas import tpu as pltpu

D_MODEL = 1024
DA_HEADS = 8
DA_DK = 64
DA_DV = 2 * DA_DK
ML_HEADS = 4
ML_DK = 128
ML_DV = 256
ML_CHUNK = 128
CONV_K = 4
PEER_HEADS = 8
PEER_TOPK = 16
N_KEYS = 128
N_EXPERTS = N_KEYS * N_KEYS
PEER_DKEY = 128
DEPTH = 1
ALPHA = (2 * DEPTH) ** 0.25
LN_EPS = 1e-5

F32 = jnp.float32
BF16 = jnp.bfloat16
NEG_INF = float("-inf")

VMEM_LIMIT_BYTES = 56 * 1024 * 1024


def _cparams(sem):
    return pltpu.CompilerParams(dimension_semantics=sem, vmem_limit_bytes=VMEM_LIMIT_BYTES)


def _layer_norm_rows(x):
    mu = jnp.mean(x, axis=-1, keepdims=True)
    xc = x - mu
    var = jnp.mean(xc * xc, axis=-1, keepdims=True)
    return xc * lax.rsqrt(var + LN_EPS)


def _dot(a, b):
    return jnp.dot(a, b, preferred_element_type=F32)


def _dot_nt(a, b):
    return lax.dot_general(a, b, (((1,), (1,)), ((), ())), preferred_element_type=F32)


def _dot_tn(a, b):
    return lax.dot_general(a, b, (((0,), (0,)), ((), ())), preferred_element_type=F32)


def _mod_kernel(c_ref, w_ref, b_ref, o_ref):
    c = c_ref[...]
    a = c * jax.nn.sigmoid(c)
    o_ref[...] = jnp.dot(a, w_ref[...], preferred_element_type=F32,
                         precision=lax.Precision.HIGHEST) + b_ref[...]


def _modulation(c, w_ada, b_ada):
    B, D = c.shape
    N = w_ada.shape[1]
    tn = 1024
    return pl.pallas_call(
        _mod_kernel,
        grid=(N // tn,),
        in_specs=[pl.BlockSpec((B, D), lambda n: (0, 0)),
                  pl.BlockSpec((D, tn), lambda n: (0, n)),
                  pl.BlockSpec((1, tn), lambda n: (0, n))],
        out_specs=pl.BlockSpec((B, tn), lambda n: (0, n)),
        out_shape=jax.ShapeDtypeStruct((B, N), F32),
        compiler_params=_cparams(("arbitrary",)),
        name="modulation",
    )(c, w_ada, b_ada.reshape(1, N))


INPROJ_ROWS = 256


def _inproj_kernel(x_ref, mod_ref, w_ref, wg_ref, wgt_ref, bcol_ref, brow_ref,
                   z_ref, gcol_ref, grow_ref, h_scr):
    n = pl.program_id(1)

    @pl.when(n == 0)
    def _():
        sh1 = mod_ref[0, 0:1, :]
        sc1 = mod_ref[0, 1:2, :]
        tm = x_ref.shape[0]
        for c in range(tm // INPROJ_ROWS):
            rows = slice(c * INPROJ_ROWS, (c + 1) * INPROJ_ROWS)
            hb = (_layer_norm_rows(x_ref[rows, :]) * (1.0 + sc1) + sh1).astype(BF16)
            h_scr[rows, :] = hb
            z_ref[0, rows, :] = _dot(hb, w_ref[0]).astype(BF16)
            gcol_ref[rows, :] = _dot(hb, wg_ref[...]) + bcol_ref[...]
            grow_ref[:, rows] = _dot_nt(wgt_ref[...], hb) + brow_ref[...]

    @pl.when(n > 0)
    def _():
        z_ref[0] = _dot(h_scr[...], w_ref[0]).astype(BF16)


def _in_proj(x2, mod3, w8, wg, wgt, bcol, brow, S):
    T, D = x2.shape
    tm = 1024
    npiece = w8.shape[0]
    return pl.pallas_call(
        _inproj_kernel,
        grid=(T // tm, npiece),
        in_specs=[pl.BlockSpec((tm, D), lambda i, n: (i, 0)),
                  pl.BlockSpec((1, 6, D), lambda i, n: ((i * tm) // S, 0, 0)),
                  pl.BlockSpec((1, D, D), lambda i, n: (n, 0, 0)),
                  pl.BlockSpec((D, 128), lambda i, n: (0, 0)),
                  pl.BlockSpec((8, D), lambda i, n: (0, 0)),
                  pl.BlockSpec((1, 128), lambda i, n: (0, 0)),
                  pl.BlockSpec((8, 1), lambda i, n: (0, 0))],
        out_specs=[pl.BlockSpec((1, tm, D), lambda i, n: (n, i, 0)),
                   pl.BlockSpec((tm, 128), lambda i, n: (i, 0)),
                   pl.BlockSpec((8, tm), lambda i, n: (0, i))],
        out_shape=[jax.ShapeDtypeStruct((npiece, T, D), BF16),
                   jax.ShapeDtypeStruct((T, 128), F32),
                   jax.ShapeDtypeStruct((8, T), F32)],
        scratch_shapes=[pltpu.VMEM((tm, D), BF16)],
        compiler_params=_cparams(("parallel", "arbitrary")),
        name="in_proj",
    )(x2, mod3, w8, wg, wgt, bcol, brow)


DA_TQ = 256


def _diffattn_kernel(lam_ref, g_ref, q_ref, k_ref, v_ref, o_ref, s_scr, *, S, lambda_init):
    tq = DA_TQ
    nq = S // tq
    lam = lam_ref[...]
    t1 = jnp.sum(lam[0:1] * lam[1:2], axis=-1, keepdims=True)
    t2 = jnp.sum(lam[2:3] * lam[3:4], axis=-1, keepdims=True)
    lam_val = jnp.exp(t1) - jnp.exp(t2) + lambda_init
    first_map = lax.broadcasted_iota(jnp.int32, (1, DA_DV), 1) < DA_DK
    gain = g_ref[...] * (1.0 - lambda_init)
    row = lax.broadcasted_iota(jnp.int32, (tq, tq), 0)
    col = lax.broadcasted_iota(jnp.int32, (tq, tq), 1)
    causal = col <= row

    def scores(qi):
        q0 = qi * tq
        qs = q_ref[0, 0, q0:q0 + tq, :] * (DA_DK ** -0.5)
        zero = jnp.zeros_like(qs)
        for mp, qm in enumerate((jnp.where(first_map, qs, zero), jnp.where(first_map, zero, qs))):
            s_scr[qi % 2, mp, :, 0:q0 + tq] = _dot_nt(qm, k_ref[0, 0, 0:q0 + tq, :])

    def finish(qi):
        q0 = qi * tq
        slot = qi % 2
        probs = []
        for mp in range(2):
            s_diag = jnp.where(causal, s_scr[slot, mp, :, q0:q0 + tq], NEG_INF)
            m = jnp.max(s_diag, axis=-1, keepdims=True)
            if qi > 0:
                s_off = s_scr[slot, mp, :, 0:q0]
                m = jnp.maximum(m, jnp.max(s_off, axis=-1, keepdims=True))
            p_diag = jnp.exp(s_diag - m)
            l = jnp.sum(p_diag, axis=-1, keepdims=True)
            p_off = None
            if qi > 0:
                p_off = jnp.exp(s_off - m)
                l = l + jnp.sum(p_off, axis=-1, keepdims=True)
            probs.append((p_diag, p_off, l))
        (p1d, p1o, l1), (p2d, p2o, l2) = probs
        w1 = 1.0 / l1
        w2 = lam_val / l2
        o = _dot((p1d * w1 - p2d * w2).astype(BF16), v_ref[0, 0, q0:q0 + tq, :])
        if qi > 0:
            o = o + _dot((p1o * w1 - p2o * w2).astype(BF16), v_ref[0, 0, 0:q0, :])
        o = o * lax.rsqrt(jnp.mean(o * o, axis=-1, keepdims=True) + LN_EPS) * gain
        o_ref[0, q0:q0 + tq, :] = o.astype(BF16)

    scores(0)
    for qi in range(nq):
        if qi + 1 < nq:
            scores(qi + 1)
        finish(qi)


def _diff_attention(z4, da_lambda, subln_g, B, S, lambda_init):
    kern = functools.partial(_diffattn_kernel, S=S, lambda_init=lambda_init)
    return pl.pallas_call(
        kern,
        grid=(B, DA_HEADS),
        in_specs=[pl.BlockSpec((4, DA_DK), lambda b, h: (0, 0)),
                  pl.BlockSpec((1, DA_DV), lambda b, h: (0, 0)),
                  pl.BlockSpec((1, 1, S, DA_DV), lambda b, h: (0, b, 0, h)),
                  pl.BlockSpec((1, 1, S, DA_DV), lambda b, h: (1, b, 0, h)),
                  pl.BlockSpec((1, 1, S, DA_DV), lambda b, h: (2, b, 0, h))],
        out_specs=pl.BlockSpec((1, S, DA_DV), lambda b, h: (b, 0, h)),
        out_shape=jax.ShapeDtypeStruct((B, S, DA_HEADS * DA_DV), BF16),
        scratch_shapes=[pltpu.VMEM((2, 2, DA_TQ, S), F32)],
        compiler_params=_cparams(("parallel", "parallel")),
        name="diff_attention",
    )(da_lambda, subln_g.reshape(1, DA_DV), z4, z4, z4)


def _mlstm_kernel(qk_ref, v_ref, og_ref, gcol_ref, grow_ref, cw_ref, cb_ref, ng_ref, o_ref,
                  qc_scr, kc_scr, c_scr, n_scr, m_scr, *, S):
    L = ML_CHUNK
    H = ML_HEADS
    srow = lax.broadcasted_iota(jnp.int32, (S, ML_DK), 0)

    for cb in range(2 * H):
        cols = slice(cb * ML_DK, (cb + 1) * ML_DK)
        x = qk_ref[0, 0, :, cols].astype(F32)
        y = x * cw_ref[CONV_K - 1:CONV_K, cols] + cb_ref[:, cols]
        for j in range(1, CONV_K):
            xs = jnp.where(srow >= j, pltpu.roll(x, j, 0), 0.0)
            y = y + xs * cw_ref[CONV_K - 1 - j:CONV_K - j, cols]
        y = y * jax.nn.sigmoid(y)
        if cb < H:
            qc_scr[:, cols] = y
        else:
            kc_scr[:, (cb - H) * ML_DK:(cb - H + 1) * ML_DK] = y * (ML_DK ** -0.5)

    r_i = lax.broadcasted_iota(jnp.int32, (L, L), 0)
    c_i = lax.broadcasted_iota(jnp.int32, (L, L), 1)
    causal = c_i <= r_i
    tril = causal.astype(F32)
    triu = (r_i <= c_i).astype(F32)
    c_scr[...] = jnp.zeros_like(c_scr)
    n_scr[...] = jnp.zeros_like(n_scr)
    m_scr[...] = jnp.zeros_like(m_scr)

    def head_chunk(hh, t0, gc, gr, b_cols, b_rows):
        Ct = c_scr[hh]
        n_row = n_scr[hh]
        m = m_scr[hh]
        q = qc_scr[pl.ds(t0, L), hh * ML_DK:(hh + 1) * ML_DK]
        k = kc_scr[pl.ds(t0, L), hh * ML_DK:(hh + 1) * ML_DK]
        v = v_ref[0, 0, pl.ds(t0, L), hh * ML_DV:(hh + 1) * ML_DV]
        ngain = ng_ref[:, hh * ML_DV:(hh + 1) * ML_DV]
        ig_col = gc[:, hh:hh + 1]
        ig_row = gr[hh:hh + 1, :]
        b_col = b_cols[:, H + hh:H + hh + 1]
        b_row = b_rows[H + hh:H + hh + 1, :]
        dm = jnp.where(causal, b_col - b_row + ig_row, NEG_INF)
        m_inter = b_col + m
        m_t = jnp.maximum(m_inter, jnp.max(dm, axis=-1, keepdims=True))
        w = jnp.exp(dm - m_t)
        qb = q.astype(BF16)
        kb = k.astype(BF16)
        p = w * _dot_nt(qb, kb)
        inter = jnp.exp(m_inter - m_t)
        num = _dot(p.astype(BF16), v) + inter * _dot(qb, Ct.astype(BF16))
        nq = jnp.sum(p, axis=-1, keepdims=True) + inter * jnp.sum(q * n_row, axis=-1, keepdims=True)
        hout = num / jnp.maximum(jnp.abs(nq), jnp.exp(-m_t))
        hout = hout * lax.rsqrt(jnp.mean(hout * hout, axis=-1, keepdims=True) + LN_EPS) * ngain
        og = og_ref[0, 0, pl.ds(t0, L), hh * ML_DV:(hh + 1) * ML_DV].astype(F32)
        o_ref[0, pl.ds(t0, L), hh * ML_DV:(hh + 1) * ML_DV] = (hout * jax.nn.sigmoid(og)).astype(BF16)
        m_new = m_t[L - 1:L, :]
        b_last = b_col[L - 1:L, :]
        decay = jnp.exp(b_last + m - m_new)
        w_s = jnp.exp(b_last - b_col + ig_col - m_new)
        c_scr[hh] = decay * Ct + _dot_tn(kb, (v.astype(F32) * w_s).astype(BF16))
        n_scr[hh] = decay * n_row + jnp.sum(k * w_s, axis=0, keepdims=True)
        m_scr[hh] = m_new

    def chunk(ci, carry):
        t0 = pl.multiple_of(ci * L, L)
        gc = gcol_ref[0, pl.ds(t0, L), :]
        gr = grow_ref[:, pl.ds(t0, L)]
        b_cols = jnp.dot(tril, jax.nn.log_sigmoid(gc), preferred_element_type=F32,
                         precision=lax.Precision.HIGHEST)
        b_rows = jnp.dot(jax.nn.log_sigmoid(gr), triu, preferred_element_type=F32,
                         precision=lax.Precision.HIGHEST)
        for hh in range(H):
            head_chunk(hh, t0, gc, gr, b_cols, b_rows)
        return carry

    lax.fori_loop(0, S // L, chunk, 0)


def _mlstm(z4, gcol3, grow, conv_w, conv_b, norm_g, B, S):
    kern = functools.partial(_mlstm_kernel, S=S)
    H = ML_HEADS
    D = H * ML_DV
    piece = lambda n: pl.BlockSpec((1, 1, S, D), lambda b: (n, b, 0, 0))
    return pl.pallas_call(
        kern,
        grid=(B,),
        in_specs=[piece(3), piece(4), piece(5),
                  pl.BlockSpec((1, S, 128), lambda b: (b, 0, 0)),
                  pl.BlockSpec((8, S), lambda b: (0, b)),
                  pl.BlockSpec((CONV_K, 2 * H * ML_DK), lambda b: (0, 0)),
                  pl.BlockSpec((1, 2 * H * ML_DK), lambda b: (0, 0)),
                  pl.BlockSpec((1, D), lambda b: (0, 0))],
        out_specs=pl.BlockSpec((1, S, D), lambda b: (b, 0, 0)),
        out_shape=jax.ShapeDtypeStruct((B, S, D), BF16),
        scratch_shapes=[pltpu.VMEM((S, H * ML_DK), F32), pltpu.VMEM((S, H * ML_DK), F32),
                        pltpu.VMEM((H, ML_DK, ML_DV), F32), pltpu.VMEM((H, 1, ML_DK), F32),
                        pltpu.VMEM((H, 1, 1), F32)],
        compiler_params=_cparams(("parallel",)),
        name="mlstm",
    )(z4, z4, z4, gcol3, grow, conv_w, conv_b, norm_g)


MERGE_ROWS = 256


def _merge_kernel(ya_ref, ym_ref, ga_ref, gm_ref, x_ref, mod_ref, wa_ref, wm_ref, wo_ref, g1_ref, b1_ref,
                  x1_ref, h2_ref):
    gt1 = mod_ref[0, 2:3, :]
    sh2 = mod_ref[0, 3:4, :]
    sc2 = mod_ref[0, 4:5, :]
    for c in range(ya_ref.shape[0] // MERGE_ROWS):
        rows = slice(c * MERGE_ROWS, (c + 1) * MERGE_ROWS)
        ya = _dot(ya_ref[rows, :], wa_ref[...])
        ym = _dot(ym_ref[rows, :], wm_ref[...])
        y = (jax.nn.sigmoid(ga_ref[0, rows, :].astype(F32)) * ya
             + jax.nn.sigmoid(gm_ref[0, rows, :].astype(F32)) * ym)
        y2 = _dot(y.astype(BF16), wo_ref[...])
        x1 = _layer_norm_rows(ALPHA * x_ref[rows, :] + gt1 * y2) * g1_ref[...] + b1_ref[...]
        x1_ref[rows, :] = x1
        h2_ref[:, rows] = (_layer_norm_rows(x1) * (1.0 + sc2) + sh2).T.astype(BF16)


def _merge(ya2, ym2, z3, x2, mod3, wa, wm, wo, ln_g, ln_b, S):
    T, D = x2.shape
    tm = 512
    tok = lambda i: (i, 0)
    const = lambda i: (0, 0)
    return pl.pallas_call(
        _merge_kernel,
        grid=(T // tm,),
        in_specs=[pl.BlockSpec((tm, D), tok), pl.BlockSpec((tm, D), tok),
                  pl.BlockSpec((1, tm, D), lambda i: (6, i, 0)),
                  pl.BlockSpec((1, tm, D), lambda i: (7, i, 0)),
                  pl.BlockSpec((tm, D), tok),
                  pl.BlockSpec((1, 6, D), lambda i: ((i * tm) // S, 0, 0)),
                  pl.BlockSpec((D, D), const), pl.BlockSpec((D, D), const), pl.BlockSpec((D, D), const),
                  pl.BlockSpec((1, D), const), pl.BlockSpec((1, D), const)],
        out_specs=[pl.BlockSpec((tm, D), tok), pl.BlockSpec((D, tm), lambda i: (0, i))],
        out_shape=[jax.ShapeDtypeStruct((T, D), F32), jax.ShapeDtypeStruct((D, T), BF16)],
        compiler_params=_cparams(("parallel",)),
        name="merge",
    )(ya2, ym2, z3, z3, x2, mod3, wa, wm, wo, ln_g, ln_b)


PEER_TB = 256
_CAND_COLS = [[k1 for k1 in range(PEER_TOPK) if (k1 + 1) * (k2 + 1) <= PEER_TOPK] for k2 in range(PEER_TOPK)]


def _batcher_pairs(n):
    pairs = []
    p = 1
    while p < n:
        k = p
        while k >= 1:
            for j in range(k % p, n - k, 2 * k):
                for i in range(min(k, n - j - k)):
                    if (i + j) // (2 * p) == (i + j + k) // (2 * p):
                        pairs.append((i + j, i + j + k))
            k //= 2
        p *= 2
    return pairs


_SORT16 = _batcher_pairs(PEER_TOPK)


def _cmpx(v, i, j):
    a, b = v[i], v[j]
    if b is None:
        return
    if a is None:
        v[i], v[j] = b, None
        return
    v[i], v[j] = jnp.maximum(a, b), jnp.minimum(a, b)


def _sort16_desc(v):
    v = list(v)
    for i, j in _SORT16:
        _cmpx(v, i, j)
    return v


def _bitonic_merge_desc(v):
    v = list(v)
    d = PEER_TOPK // 2
    while d >= 1:
        for i in range(PEER_TOPK):
            if i & d == 0:
                _cmpx(v, i, i + d)
        d //= 2
    return v


def _half_clean(a, b):
    out = []
    for g in range(PEER_TOPK):
        x, y = a[g], b[PEER_TOPK - 1 - g]
        out.append(x if y is None else (y if x is None else jnp.maximum(x, y)))
    return out


def _top16_all_sublanes(s):
    rows = _sort16_desc([s[g * 8:(g + 1) * 8, :] for g in range(PEER_TOPK)])
    for shift in (4, 2, 1):
        partner = [pltpu.roll(r, shift, 0) for r in rows]
        rows = _bitonic_merge_desc(_half_clean(rows, partner))
    return rows


def _route_kernel(h_ref, wqt_ref, keys_ref, rk2_ref, e2_ref, c1_ref, e1_ref, q_scr, s_scr, a_scr, f_scr):
    half = PEER_DKEY // 2
    K = PEER_TOPK
    q_scr[...] = _dot(wqt_ref[...], h_ref[...]).astype(BF16)

    def stage_a(h, carry):
        base = pl.multiple_of(h * PEER_DKEY, PEER_DKEY)
        scores = [_dot(keys_ref[p], q_scr[pl.ds(base + p * half, half), :]) for p in range(2)]
        for p in range(2):
            s_scr[p, h] = scores[p]
            top = _top16_all_sublanes(scores[p])
            for k in range(K):
                a_scr[p, k, pl.ds(h, 1), :] = top[k][0:1, :]
        return carry

    lax.fori_loop(0, PEER_HEADS, stage_a, 0)

    a1 = [a_scr[0, k] for k in range(K)]
    a2 = [a_scr[1, k] for k in range(K)]
    cand = [[a1[k1] + a2[k2] for k1 in col] for k2, col in enumerate(_CAND_COLS)]
    g0 = [cand[k2][0] for k2 in range(K)]
    rest = [cand[k2][i] for k2 in range(K) for i in range(1, len(_CAND_COLS[k2]))]
    rest += [None] * (-len(rest) % K)
    groups = [g0] + [_sort16_desc(rest[i:i + K]) for i in range(0, len(rest), K)]
    while len(groups) > 2:
        merged = [_bitonic_merge_desc(_half_clean(groups[i], groups[i + 1])) for i in range(0, len(groups) - 1, 2)]
        groups = merged + ([groups[-1]] if len(groups) % 2 else [])
    last = [x for x in _half_clean(groups[0], groups[1]) if x is not None]
    tau = functools.reduce(jnp.minimum, last)
    cmax = a1[0] + a2[0]
    zsum = None
    for k2, col in enumerate(_CAND_COLS):
        phi = None
        for i, k1 in enumerate(col):
            c = cand[k2][i]
            hit = c >= tau
            term = jnp.where(hit, jnp.exp(c - cmax), 0.0)
            zsum = term if zsum is None else zsum + term
            lo = jnp.where(hit, a1[k1], jnp.inf)
            phi = lo if phi is None else jnp.minimum(phi, lo)
        f_scr[k2] = phi
    f_scr[K] = 1.0 / zsum

    def prefix_count(pred, thr):
        b8 = pred(thr[7])
        b4 = pred(jnp.where(b8, thr[11], thr[3]))
        b2 = pred(jnp.where(b8, jnp.where(b4, thr[13], thr[9]), jnp.where(b4, thr[5], thr[1])))
        b1 = pred(jnp.where(b8,
                            jnp.where(b4, jnp.where(b2, thr[14], thr[12]), jnp.where(b2, thr[10], thr[8])),
                            jnp.where(b4, jnp.where(b2, thr[6], thr[4]), jnp.where(b2, thr[2], thr[0]))))
        b0 = pred(thr[15])
        bit = lambda b, v: jnp.where(b, v, 0.0)
        return bit(b8, 8.0) + bit(b4, 4.0) + bit(b2, 2.0) + bit(b1, 1.0) + bit(b0, 1.0)

    def stage_c(h, carry):
        s1 = s_scr[0, h]
        s2 = s_scr[1, h]
        shape = s2.shape
        top2 = [jnp.broadcast_to(a_scr[1, k, pl.ds(h, 1), :], shape) for k in range(K)]
        phi = [jnp.broadcast_to(f_scr[k, pl.ds(h, 1), :], shape) for k in range(K)]
        rk2_ref[h] = prefix_count(lambda t: t > s2, top2).astype(BF16)
        e2_ref[h] = jnp.exp(s2 - a_scr[1, 0, pl.ds(h, 1), :]).astype(BF16)
        c1_ref[h] = prefix_count(lambda t: s1 >= t, phi)
        e1_ref[h] = jnp.exp(s1 - a_scr[0, 0, pl.ds(h, 1), :]) * f_scr[K, pl.ds(h, 1), :]
        return carry

    lax.fori_loop(0, PEER_HEADS, stage_c, 0)


def _peer_route(h2t, wqt, keys):
    D, T = h2t.shape
    tb = PEER_TB
    PH = PEER_HEADS
    blk = pl.BlockSpec((PH, N_KEYS, tb), lambda i: (0, 0, i))
    return pl.pallas_call(
        _route_kernel,
        grid=(T // tb,),
        in_specs=[pl.BlockSpec((D, tb), lambda i: (0, i)),
                  pl.BlockSpec((PH * PEER_DKEY, D), lambda i: (0, 0)),
                  pl.BlockSpec((2, N_KEYS, PEER_DKEY // 2), lambda i: (0, 0, 0))],
        out_specs=[blk, blk, blk, blk],
        out_shape=[jax.ShapeDtypeStruct((PH, N_KEYS, T), BF16),
                   jax.ShapeDtypeStruct((PH, N_KEYS, T), BF16),
                   jax.ShapeDtypeStruct((PH, N_KEYS, T), F32),
                   jax.ShapeDtypeStruct((PH, N_KEYS, T), F32)],
        scratch_shapes=[pltpu.VMEM((PH * PEER_DKEY, tb), BF16),
                        pltpu.VMEM((2, PH, N_KEYS, tb), F32),
                        pltpu.VMEM((2, PEER_TOPK, PH, tb), F32),
                        pltpu.VMEM((PEER_TOPK + 1, PH, tb), F32)],
        compiler_params=_cparams(("parallel",)),
        name="peer_route",
    )(h2t, wqt, keys)


EXP_TB = 512
EXP_EB = 1024


EXPERT_STEP_ORDER = (("v", 0), ("k", 0), ("g", 0), ("k", 1), ("g", 1), ("k", 2), ("g", 2), ("k", 3), ("g", 3),
                     ("v", 1), ("k", 4), ("g", 4), ("k", 5), ("g", 5), ("k", 6), ("g", 6), ("k", 7), ("g", 7))
BF16_ROWS = 16
LANES = 256


def _experts_kernel(h_ref, u_ref, vt_ref, rk2_ref, e2_ref, c1_ref, e1_ref, x1_ref, mod_ref, g2_ref, b2_ref,
                    o_ref, acc_scr, at_scr, w0_scr, w1_scr):
    e = pl.program_id(1)
    ne = N_EXPERTS // EXP_EB
    nsub = EXP_EB // N_KEYS
    ngrp = N_KEYS // BF16_ROWS
    tb = h_ref.shape[1]
    nhalf = tb // LANES
    assert nhalf == 2 and nsub % nhalf == 0

    @pl.when(e == 0)
    def _():
        acc_scr[...] = jnp.zeros_like(acc_scr)
        w1_scr[...] = jnp.zeros_like(w1_scr)

    def value_matmul(w_read, half):
        cols = slice(half * LANES, (half + 1) * LANES)
        acc_scr[:, cols] += _dot(vt_ref[...], w_read[:, cols])

    def key_matmul(j):
        krows = slice(j * N_KEYS, (j + 1) * N_KEYS)
        at_scr[krows, :] = _dot(u_ref[krows, :], h_ref[...])

    def gated_activation(j, w_write):
        for lt in range(nhalf):
            cols = slice(lt * LANES, (lt + 1) * LANES)
            gates = [None] * ngrp
            for h in range(PEER_HEADS):
                cnt = jnp.broadcast_to(c1_ref[h, j:j + 1, cols], (BF16_ROWS, LANES)).astype(BF16)
                e1 = jnp.broadcast_to(e1_ref[h, j:j + 1, cols], (BF16_ROWS, LANES)).astype(BF16)
                for r in range(ngrp):
                    rows = slice(r * BF16_ROWS, (r + 1) * BF16_ROWS)
                    term = jnp.where(rk2_ref[h, rows, cols] < cnt, e2_ref[h, rows, cols],
                                     jnp.zeros((), BF16)) * e1
                    gates[r] = term if gates[r] is None else gates[r] + term
            for r in range(ngrp):
                rows = slice(j * N_KEYS + r * BF16_ROWS, j * N_KEYS + (r + 1) * BF16_ROWS)
                a = at_scr[rows, cols].astype(BF16)
                act = (0.5 * a) * (1.0 + lax.erf(a * (2.0 ** -0.5)))
                w_write[rows, cols] = gates[r] * act

    def step(w_write, w_read):
        for kind, idx in EXPERT_STEP_ORDER:
            if kind == "v":
                value_matmul(w_read, idx)
            elif kind == "k":
                key_matmul(idx)
            else:
                gated_activation(idx, w_write)

    parity = lax.rem(e, 2)

    @pl.when((e < ne) & (parity == 0))
    def _():
        step(w0_scr, w1_scr)

    @pl.when((e < ne) & (parity == 1))
    def _():
        step(w1_scr, w0_scr)

    @pl.when(e == ne)
    def _():
        w_last = w1_scr if ne % 2 == 0 else w0_scr
        for half in range(nhalf):
            value_matmul(w_last, half)
        yf = acc_scr[...].T
        gt2 = mod_ref[0, 5:6, :]
        r = ALPHA * x1_ref[...] + gt2 * yf
        o_ref[...] = _layer_norm_rows(r) * g2_ref[...] + b2_ref[...]


def _peer_experts(h2t, u_b, vt_b, rk2, e2, c1, e1, x1, mod3, ln_g, ln_b, S):
    D, T = h2t.shape
    tb, eb = EXP_TB, EXP_EB
    PH = PEER_HEADS
    route = pl.BlockSpec((PH, N_KEYS, tb), lambda i, e: (0, 0, i))
    ne = N_EXPERTS // eb
    key1 = pl.BlockSpec((PH, eb // N_KEYS, tb), lambda i, e: (0, jnp.minimum(e, ne - 1), i))
    return pl.pallas_call(
        _experts_kernel,
        grid=(T // tb, ne + 1),
        in_specs=[pl.BlockSpec((D, tb), lambda i, e: (0, i)),
                  pl.BlockSpec((eb, D), lambda i, e: (jnp.minimum(e, ne - 1), 0)),
                  pl.BlockSpec((D, eb), lambda i, e: (0, jnp.maximum(e - 1, 0))),
                  route, route, key1, key1,
                  pl.BlockSpec((tb, D), lambda i, e: (i, 0)),
                  pl.BlockSpec((1, 6, D), lambda i, e: ((i * tb) // S, 0, 0)),
                  pl.BlockSpec((1, D), lambda i, e: (0, 0)),
                  pl.BlockSpec((1, D), lambda i, e: (0, 0))],
        out_specs=pl.BlockSpec((tb, D), lambda i, e: (i, 0)),
        out_shape=jax.ShapeDtypeStruct((T, D), F32),
        scratch_shapes=[pltpu.VMEM((D, tb), F32), pltpu.VMEM((eb, tb), F32),
                        pltpu.VMEM((eb, tb), BF16), pltpu.VMEM((eb, tb), BF16)],
        compiler_params=_cparams(("parallel", "arbitrary")),
        name="peer_experts",
    )(h2t, u_b, vt_b, rk2, e2, c1, e1, x1, mod3, ln_g, ln_b)


def kernel(x, c, w_ada, b_ada, w_in, b_if, conv_w, conv_b, da_lambda, da_subln_g, ml_norm_g, w_br_attn,
           w_br_mlstm, w_out, ln1_g, ln1_b, peer_wq, peer_keys, peer_u, peer_v, ln2_g, ln2_b):
    B, S, D = x.shape
    T = B * S
    assert D == D_MODEL and S % DA_TQ == 0 and S % 1024 == 0
    l = 0
    lambda_init = 0.8 - 0.6 * math.exp(-0.3 * l)

    mod3 = _modulation(c, w_ada[l], b_ada[l]).reshape(B, 6, D)

    w = w_in[l]
    o_mq = 3 * D
    o_mv = o_mq + 2 * ML_HEADS * ML_DK
    o_mo = o_mv + D
    o_if = o_mo + D
    o_ga = o_if + 2 * ML_HEADS
    o_gm = o_ga + D
    starts = (0, D, 2 * D, o_mq, o_mv, o_mo, o_ga, o_gm)
    w8 = jnp.stack([w[:, s0:s0 + D] for s0 in starts]).astype(BF16)
    w_if = w[:, o_if:o_if + 2 * ML_HEADS]
    wg = jnp.pad(w_if, ((0, 0), (0, 128 - 2 * ML_HEADS))).astype(BF16)
    wgt = w_if.T.astype(BF16)
    bias8 = b_if[l].reshape(2 * ML_HEADS)
    bcol = jnp.pad(bias8, (0, 128 - 2 * ML_HEADS)).reshape(1, 128)
    brow = bias8.reshape(2 * ML_HEADS, 1)

    x2 = x.reshape(T, D)
    z, gcol, grow = _in_proj(x2, mod3, w8, wg, wgt, bcol, brow, S)
    z4 = z.reshape(8, B, S, D)

    ya = _diff_attention(z4, da_lambda[l], da_subln_g[l], B, S, lambda_init)
    ym = _mlstm(z4, gcol.reshape(B, S, 128), grow, conv_w[l], conv_b[l].reshape(1, -1),
                ml_norm_g[l].reshape(1, -1), B, S)

    x1, h2 = _merge(ya.reshape(T, D), ym.reshape(T, D), z, x2, mod3,
                    w_br_attn[l].astype(BF16), w_br_mlstm[l].astype(BF16), w_out[l].astype(BF16),
                    ln1_g[l].reshape(1, D), ln1_b[l].reshape(1, D), S)

    rk2, e2, c1, e1 = _peer_route(h2, peer_wq[l].T.astype(BF16), peer_keys[l].astype(BF16))
    out = _peer_experts(h2, peer_u[l].astype(BF16), peer_v[l].T.astype(BF16), rk2, e2, c1, e1, x1, mod3,
                        ln2_g[l].reshape(1, D), ln2_b[l].reshape(1, D), S)
    return out.reshape(B, S, D)
```

```python
import functools
import math

import jax
import jax.numpy as jnp
from jax import lax
from jax.experimental import pallas as pl
from jax.experimental.pallas import tpu as pltpu

D_MODEL = 1024
DA_HEADS = 8
DA_DK = 64
DA_DV = 2 * DA_DK
ML_HEADS = 4
ML_DK = 128
ML_DV = 256
ML_CHUNK = 128
CONV_K = 4
PEER_HEADS = 8
PEER_TOPK = 16
N_KEYS = 128
N_EXPERTS = N_KEYS * N_KEYS
PEER_DKEY = 128
DEPTH = 1
ALPHA = (2 * DEPTH) ** 0.25
LN_EPS = 1e-5

F32 = jnp.float32
BF16 = jnp.bfloat16
NEG_INF = float("-inf")

VMEM_LIMIT_BYTES = 56 * 1024 * 1024


def _cparams(sem):
    return pltpu.CompilerParams(dimension_semantics=sem, vmem_limit_bytes=VMEM_LIMIT_BYTES)


def _layer_norm_rows(x):
    mu = jnp.mean(x, axis=-1, keepdims=True)
    xc = x - mu
    var = jnp.mean(xc * xc, axis=-1, keepdims=True)
    return xc * lax.rsqrt(var + LN_EPS)


def _dot(a, b):
    return jnp.dot(a, b, preferred_element_type=F32)


def _dot_nt(a, b):
    return lax.dot_general(a, b, (((1,), (1,)), ((), ())), preferred_element_type=F32)


def _dot_tn(a, b):
    return lax.dot_general(a, b, (((0,), (0,)), ((), ())), preferred_element_type=F32)


def _mod_kernel(c_ref, w_ref, b_ref, o_ref):
    c = c_ref[...]
    a = c * jax.nn.sigmoid(c)
    o_ref[...] = jnp.dot(a, w_ref[...], preferred_element_type=F32,
                         precision=lax.Precision.HIGHEST) + b_ref[...]


def _modulation(c, w_ada, b_ada):
    B, D = c.shape
    N = w_ada.shape[1]
    tn = 1024
    return pl.pallas_call(
        _mod_kernel,
        grid=(N // tn,),
        in_specs=[pl.BlockSpec((B, D), lambda n: (0, 0)),
                  pl.BlockSpec((D, tn), lambda n: (0, n)),
                  pl.BlockSpec((1, tn), lambda n: (0, n))],
        out_specs=pl.BlockSpec((B, tn), lambda n: (0, n)),
        out_shape=jax.ShapeDtypeStruct((B, N), F32),
        compiler_params=_cparams(("arbitrary",)),
        name="modulation",
    )(c, w_ada, b_ada.reshape(1, N))


INPROJ_ROWS = 256


def _inproj_kernel(x_ref, mod_ref, w_ref, wg_ref, wgt_ref, bcol_ref, brow_ref,
                   z_ref, gcol_ref, grow_ref, h_scr):
    n = pl.program_id(1)

    @pl.when(n == 0)
    def _():
        sh1 = mod_ref[0, 0:1, :]
        sc1 = mod_ref[0, 1:2, :]
        tm = x_ref.shape[0]
        for c in range(tm // INPROJ_ROWS):
            rows = slice(c * INPROJ_ROWS, (c + 1) * INPROJ_ROWS)
            hb = (_layer_norm_rows(x_ref[rows, :]) * (1.0 + sc1) + sh1).astype(BF16)
            h_scr[rows, :] = hb
            z_ref[0, rows, :] = _dot(hb, w_ref[0]).astype(BF16)
            gcol_ref[rows, :] = _dot(hb, wg_ref[...]) + bcol_ref[...]
            grow_ref[:, rows] = _dot_nt(wgt_ref[...], hb) + brow_ref[...]

    @pl.when(n > 0)
    def _():
        z_ref[0] = _dot(h_scr[...], w_ref[0]).astype(BF16)


def _in_proj(x2, mod3, w8, wg, wgt, bcol, brow, S):
    T, D = x2.shape
    tm = 1024
    npiece = w8.shape[0]
    return pl.pallas_call(
        _inproj_kernel,
        grid=(T // tm, npiece),
        in_specs=[pl.BlockSpec((tm, D), lambda i, n: (i, 0)),
                  pl.BlockSpec((1, 6, D), lambda i, n: ((i * tm) // S, 0, 0)),
                  pl.BlockSpec((1, D, D), lambda i, n: (n, 0, 0)),
                  pl.BlockSpec((D, 128), lambda i, n: (0, 0)),
                  pl.BlockSpec((8, D), lambda i, n: (0, 0)),
                  pl.BlockSpec((1, 128), lambda i, n: (0, 0)),
                  pl.BlockSpec((8, 1), lambda i, n: (0, 0))],
        out_specs=[pl.BlockSpec((1, tm, D), lambda i, n: (n, i, 0)),
                   pl.BlockSpec((tm, 128), lambda i, n: (i, 0)),
                   pl.BlockSpec((8, tm), lambda i, n: (0, i))],
        out_shape=[jax.ShapeDtypeStruct((npiece, T, D), BF16),
                   jax.ShapeDtypeStruct((T, 128), F32),
                   jax.ShapeDtypeStruct((8, T), F32)],
        scratch_shapes=[pltpu.VMEM((tm, D), BF16)],
        compiler_params=_cparams(("parallel", "arbitrary")),
        name="in_proj",
    )(x2, mod3, w8, wg, wgt, bcol, brow)


DA_TQ = 256


def _diffattn_kernel(lam_ref, g_ref, q_ref, k_ref, v_ref, o_ref, s_scr, *, S, lambda_init):
    tq = DA_TQ
    nq = S // tq
    lam = lam_ref[...]
    t1 = jnp.sum(lam[0:1] * lam[1:2], axis=-1, keepdims=True)
    t2 = jnp.sum(lam[2:3] * lam[3:4], axis=-1, keepdims=True)
    lam_val = jnp.exp(t1) - jnp.exp(t2) + lambda_init
    first_map = lax.broadcasted_iota(jnp.int32, (1, DA_DV), 1) < DA_DK
    gain = g_ref[...] * (1.0 - lambda_init)
    row = lax.broadcasted_iota(jnp.int32, (tq, tq), 0)
    col = lax.broadcasted_iota(jnp.int32, (tq, tq), 1)
    causal = col <= row

    def scores(qi):
        q0 = qi * tq
        qs = q_ref[0, 0, q0:q0 + tq, :]
        zero = jnp.zeros_like(qs)
        for mp, qm in enumerate((jnp.where(first_map, qs, zero), jnp.where(first_map, zero, qs))):
            s_scr[qi % 2, mp, :, 0:q0 + tq] = _dot_nt(qm, k_ref[0, 0, 0:q0 + tq, :])

    def finish(qi):
        q0 = qi * tq
        slot = qi % 2
        probs = []
        for mp in range(2):
            s_diag = jnp.where(causal, s_scr[slot, mp, :, q0:q0 + tq], NEG_INF)
            m = jnp.max(s_diag, axis=-1, keepdims=True)
            if qi > 0:
                s_off = s_scr[slot, mp, :, 0:q0]
                m = jnp.maximum(m, jnp.max(s_off, axis=-1, keepdims=True))
            p_diag = jnp.exp2(s_diag - m)
            l = jnp.sum(p_diag, axis=-1, keepdims=True)
            p_off = None
            if qi > 0:
                p_off = jnp.exp2(s_off - m)
                l = l + jnp.sum(p_off, axis=-1, keepdims=True)
            probs.append((p_diag, p_off, l))
        (p1d, p1o, l1), (p2d, p2o, l2) = probs
        ratio = lam_val * l1 / l2
        o = _dot((p1d - p2d * ratio).astype(BF16), v_ref[0, 0, q0:q0 + tq, :])
        if qi > 0:
            o = o + _dot((p1o - p2o * ratio).astype(BF16), v_ref[0, 0, 0:q0, :])
        o = o / l1
        o = o * lax.rsqrt(jnp.mean(o * o, axis=-1, keepdims=True) + LN_EPS) * gain
        o_ref[0, q0:q0 + tq, :] = o.astype(BF16)

    scores(0)
    for qi in range(nq):
        if qi + 1 < nq:
            scores(qi + 1)
        finish(qi)


def _diff_attention(z4, da_lambda, subln_g, B, S, lambda_init):
    kern = functools.partial(_diffattn_kernel, S=S, lambda_init=lambda_init)
    return pl.pallas_call(
        kern,
        grid=(B, DA_HEADS),
        in_specs=[pl.BlockSpec((4, DA_DK), lambda b, h: (0, 0)),
                  pl.BlockSpec((1, DA_DV), lambda b, h: (0, 0)),
                  pl.BlockSpec((1, 1, S, DA_DV), lambda b, h: (0, b, 0, h)),
                  pl.BlockSpec((1, 1, S, DA_DV), lambda b, h: (1, b, 0, h)),
                  pl.BlockSpec((1, 1, S, DA_DV), lambda b, h: (2, b, 0, h))],
        out_specs=pl.BlockSpec((1, S, DA_DV), lambda b, h: (b, 0, h)),
        out_shape=jax.ShapeDtypeStruct((B, S, DA_HEADS * DA_DV), BF16),
        scratch_shapes=[pltpu.VMEM((2, 2, DA_TQ, S), F32)],
        compiler_params=_cparams(("parallel", "parallel")),
        name="diff_attention",
    )(da_lambda, subln_g.reshape(1, DA_DV), z4, z4, z4)


def _mlstm_kernel(qk_ref, v_ref, og_ref, gcol_ref, grow_ref, cw_ref, cb_ref, ng_ref, o_ref,
                  qc_scr, kc_scr, c_scr, n_scr, m_scr, *, S):
    L = ML_CHUNK
    H = ML_HEADS
    srow = lax.broadcasted_iota(jnp.int32, (S, ML_DK), 0)

    for cb in range(2 * H):
        cols = slice(cb * ML_DK, (cb + 1) * ML_DK)
        x = qk_ref[0, 0, :, cols].astype(F32)
        y = x * cw_ref[CONV_K - 1:CONV_K, cols] + cb_ref[:, cols]
        for j in range(1, CONV_K):
            xs = jnp.where(srow >= j, pltpu.roll(x, j, 0), 0.0)
            y = y + xs * cw_ref[CONV_K - 1 - j:CONV_K - j, cols]
        y = y * jax.nn.sigmoid(y)
        if cb < H:
            qc_scr[:, cols] = y
        else:
            kc_scr[:, (cb - H) * ML_DK:(cb - H + 1) * ML_DK] = y * (ML_DK ** -0.5)

    r_i = lax.broadcasted_iota(jnp.int32, (L, L), 0)
    c_i = lax.broadcasted_iota(jnp.int32, (L, L), 1)
    causal = c_i <= r_i
    tril = causal.astype(F32)
    triu = (r_i <= c_i).astype(F32)
    c_scr[...] = jnp.zeros_like(c_scr)
    n_scr[...] = jnp.zeros_like(n_scr)
    m_scr[...] = jnp.zeros_like(m_scr)

    def head_chunk(hh, t0, gc, gr, b_cols, b_rows):
        Ct = c_scr[hh]
        n_row = n_scr[hh]
        m = m_scr[hh]
        q = qc_scr[pl.ds(t0, L), hh * ML_DK:(hh + 1) * ML_DK]
        k = kc_scr[pl.ds(t0, L), hh * ML_DK:(hh + 1) * ML_DK]
        v = v_ref[0, 0, pl.ds(t0, L), hh * ML_DV:(hh + 1) * ML_DV]
        ngain = ng_ref[:, hh * ML_DV:(hh + 1) * ML_DV]
        ig_col = gc[:, hh:hh + 1]
        ig_row = gr[hh:hh + 1, :]
        b_col = b_cols[:, H + hh:H + hh + 1]
        b_row = b_rows[H + hh:H + hh + 1, :]
        dm = jnp.where(causal, b_col - b_row + ig_row, NEG_INF)
        m_inter = b_col + m
        m_t = jnp.maximum(m_inter, jnp.max(dm, axis=-1, keepdims=True))
        w = jnp.exp(dm - m_t)
        qb = q.astype(BF16)
        kb = k.astype(BF16)
        p = w * _dot_nt(qb, kb)
        inter = jnp.exp(m_inter - m_t)
        num = _dot(p.astype(BF16), v) + inter * _dot(qb, Ct.astype(BF16))
        nq = jnp.sum(p, axis=-1, keepdims=True) + inter * jnp.sum(q * n_row, axis=-1, keepdims=True)
        hout = num / jnp.maximum(jnp.abs(nq), jnp.exp(-m_t))
        hout = hout * lax.rsqrt(jnp.mean(hout * hout, axis=-1, keepdims=True) + LN_EPS) * ngain
        og = og_ref[0, 0, pl.ds(t0, L), hh * ML_DV:(hh + 1) * ML_DV].astype(F32)
        o_ref[0, pl.ds(t0, L), hh * ML_DV:(hh + 1) * ML_DV] = (hout * jax.nn.sigmoid(og)).astype(BF16)
        m_new = m_t[L - 1:L, :]
        b_last = b_col[L - 1:L, :]
        decay = jnp.exp(b_last + m - m_new)
        w_s = jnp.exp(b_last - b_col + ig_col - m_new)
        c_scr[hh] = decay * Ct + _dot_tn(kb, (v.astype(F32) * w_s).astype(BF16))
        n_scr[hh] = decay * n_row + jnp.sum(k * w_s, axis=0, keepdims=True)
        m_scr[hh] = m_new

    def chunk(ci, carry):
        t0 = pl.multiple_of(ci * L, L)
        gc = gcol_ref[0, pl.ds(t0, L), :]
        gr = grow_ref[:, pl.ds(t0, L)]
        b_cols = jnp.dot(tril, jax.nn.log_sigmoid(gc), preferred_element_type=F32,
                         precision=lax.Precision.HIGHEST)
        b_rows = jnp.dot(jax.nn.log_sigmoid(gr), triu, preferred_element_type=F32,
                         precision=lax.Precision.HIGHEST)
        for hh in range(H):
            head_chunk(hh, t0, gc, gr, b_cols, b_rows)
        return carry

    lax.fori_loop(0, S // L, chunk, 0)


def _mlstm(z4, gcol3, grow, conv_w, conv_b, norm_g, B, S):
    kern = functools.partial(_mlstm_kernel, S=S)
    H = ML_HEADS
    D = H * ML_DV
    piece = lambda n: pl.BlockSpec((1, 1, S, D), lambda b: (n, b, 0, 0))
    return pl.pallas_call(
        kern,
        grid=(B,),
        in_specs=[piece(3), piece(4), piece(5),
                  pl.BlockSpec((1, S, 128), lambda b: (b, 0, 0)),
                  pl.BlockSpec((8, S), lambda b: (0, b)),
                  pl.BlockSpec((CONV_K, 2 * H * ML_DK), lambda b: (0, 0)),
                  pl.BlockSpec((1, 2 * H * ML_DK), lambda b: (0, 0)),
                  pl.BlockSpec((1, D), lambda b: (0, 0))],
        out_specs=pl.BlockSpec((1, S, D), lambda b: (b, 0, 0)),
        out_shape=jax.ShapeDtypeStruct((B, S, D), BF16),
        scratch_shapes=[pltpu.VMEM((S, H * ML_DK), F32), pltpu.VMEM((S, H * ML_DK), F32),
                        pltpu.VMEM((H, ML_DK, ML_DV), F32), pltpu.VMEM((H, 1, ML_DK), F32),
                        pltpu.VMEM((H, 1, 1), F32)],
        compiler_params=_cparams(("parallel",)),
        name="mlstm",
    )(z4, z4, z4, gcol3, grow, conv_w, conv_b, norm_g)


MERGE_ROWS = 256


def _merge_kernel(ya_ref, ym_ref, ga_ref, gm_ref, x_ref, mod_ref, wa_ref, wm_ref, wo_ref, g1_ref, b1_ref,
                  x1_ref, h2_ref):
    gt1 = mod_ref[0, 2:3, :]
    sh2 = mod_ref[0, 3:4, :]
    sc2 = mod_ref[0, 4:5, :]
    for c in range(ya_ref.shape[0] // MERGE_ROWS):
        rows = slice(c * MERGE_ROWS, (c + 1) * MERGE_ROWS)
        ya = _dot(ya_ref[rows, :], wa_ref[...])
        ym = _dot(ym_ref[rows, :], wm_ref[...])
        y = (jax.nn.sigmoid(ga_ref[0, rows, :].astype(F32)) * ya
             + jax.nn.sigmoid(gm_ref[0, rows, :].astype(F32)) * ym)
        y2 = _dot(y.astype(BF16), wo_ref[...])
        x1 = _layer_norm_rows(ALPHA * x_ref[rows, :] + gt1 * y2) * g1_ref[...] + b1_ref[...]
        x1_ref[rows, :] = x1
        h2_ref[:, rows] = (_layer_norm_rows(x1) * (1.0 + sc2) + sh2).T.astype(BF16)


def _merge(ya2, ym2, z3, x2, mod3, wa, wm, wo, ln_g, ln_b, S):
    T, D = x2.shape
    tm = 512
    tok = lambda i: (i, 0)
    const = lambda i: (0, 0)
    return pl.pallas_call(
        _merge_kernel,
        grid=(T // tm,),
        in_specs=[pl.BlockSpec((tm, D), tok), pl.BlockSpec((tm, D), tok),
                  pl.BlockSpec((1, tm, D), lambda i: (6, i, 0)),
                  pl.BlockSpec((1, tm, D), lambda i: (7, i, 0)),
                  pl.BlockSpec((tm, D), tok),
                  pl.BlockSpec((1, 6, D), lambda i: ((i * tm) // S, 0, 0)),
                  pl.BlockSpec((D, D), const), pl.BlockSpec((D, D), const), pl.BlockSpec((D, D), const),
                  pl.BlockSpec((1, D), const), pl.BlockSpec((1, D), const)],
        out_specs=[pl.BlockSpec((tm, D), tok), pl.BlockSpec((D, tm), lambda i: (0, i))],
        out_shape=[jax.ShapeDtypeStruct((T, D), F32), jax.ShapeDtypeStruct((D, T), BF16)],
        compiler_params=_cparams(("parallel",)),
        name="merge",
    )(ya2, ym2, z3, z3, x2, mod3, wa, wm, wo, ln_g, ln_b)


PEER_TB = 256
_CAND_COLS = [[k1 for k1 in range(PEER_TOPK) if (k1 + 1) * (k2 + 1) <= PEER_TOPK] for k2 in range(PEER_TOPK)]


def _batcher_pairs(n):
    pairs = []
    p = 1
    while p < n:
        k = p
        while k >= 1:
            for j in range(k % p, n - k, 2 * k):
                for i in range(min(k, n - j - k)):
                    if (i + j) // (2 * p) == (i + j + k) // (2 * p):
                        pairs.append((i + j, i + j + k))
            k //= 2
        p *= 2
    return pairs


_SORT16 = _batcher_pairs(PEER_TOPK)


def _cmpx(v, i, j):
    a, b = v[i], v[j]
    if b is None:
        return
    if a is None:
        v[i], v[j] = b, None
        return
    v[i], v[j] = jnp.maximum(a, b), jnp.minimum(a, b)


def _sort16_desc(v):
    v = list(v)
    for i, j in _SORT16:
        _cmpx(v, i, j)
    return v


def _bitonic_merge_desc(v):
    v = list(v)
    d = PEER_TOPK // 2
    while d >= 1:
        for i in range(PEER_TOPK):
            if i & d == 0:
                _cmpx(v, i, i + d)
        d //= 2
    return v


def _half_clean(a, b):
    out = []
    for g in range(PEER_TOPK):
        x, y = a[g], b[PEER_TOPK - 1 - g]
        out.append(x if y is None else (y if x is None else jnp.maximum(x, y)))
    return out


def _top16_all_sublanes(s):
    rows = _sort16_desc([s[g * 8:(g + 1) * 8, :] for g in range(PEER_TOPK)])
    for shift in (4, 2, 1):
        partner = [pltpu.roll(r, shift, 0) for r in rows]
        rows = _bitonic_merge_desc(_half_clean(rows, partner))
    return rows


def _route_kernel(h_ref, wqt_ref, keys_ref, rk2_ref, e2_ref, c1_ref, e1_ref, q_scr, s_scr, a_scr, f_scr):
    half = PEER_DKEY // 2
    K = PEER_TOPK
    q_scr[...] = _dot(wqt_ref[...], h_ref[...]).astype(BF16)

    def stage_a(h, carry):
        base = pl.multiple_of(h * PEER_DKEY, PEER_DKEY)
        scores = [_dot(keys_ref[p], q_scr[pl.ds(base + p * half, half), :]) for p in range(2)]
        for p in range(2):
            s_scr[p, h] = scores[p]
            top = _top16_all_sublanes(scores[p])
            for k in range(K):
                a_scr[p, k, pl.ds(h, 1), :] = top[k][0:1, :]
        return carry

    lax.fori_loop(0, PEER_HEADS, stage_a, 0)

    a1 = [a_scr[0, k] for k in range(K)]
    a2 = [a_scr[1, k] for k in range(K)]
    cand = [[a1[k1] + a2[k2] for k1 in col] for k2, col in enumerate(_CAND_COLS)]
    g0 = [cand[k2][0] for k2 in range(K)]
    rest = [cand[k2][i] for k2 in range(K) for i in range(1, len(_CAND_COLS[k2]))]
    rest += [None] * (-len(rest) % K)
    groups = [g0] + [_sort16_desc(rest[i:i + K]) for i in range(0, len(rest), K)]
    while len(groups) > 2:
        merged = [_bitonic_merge_desc(_half_clean(groups[i], groups[i + 1])) for i in range(0, len(groups) - 1, 2)]
        groups = merged + ([groups[-1]] if len(groups) % 2 else [])
    last = [x for x in _half_clean(groups[0], groups[1]) if x is not None]
    tau = functools.reduce(jnp.minimum, last)
    cmax = a1[0] + a2[0]
    zsum = None
    for k2, col in enumerate(_CAND_COLS):
        phi = None
        for i, k1 in enumerate(col):
            c = cand[k2][i]
            hit = c >= tau
            term = jnp.where(hit, jnp.exp(c - cmax), 0.0)
            zsum = term if zsum is None else zsum + term
            lo = jnp.where(hit, a1[k1], jnp.inf)
            phi = lo if phi is None else jnp.minimum(phi, lo)
        f_scr[k2] = phi
    f_scr[K] = 1.0 / zsum

    def prefix_count(pred, thr):
        b8 = pred(thr[7])
        b4 = pred(jnp.where(b8, thr[11], thr[3]))
        b2 = pred(jnp.where(b8, jnp.where(b4, thr[13], thr[9]), jnp.where(b4, thr[5], thr[1])))
        b1 = pred(jnp.where(b8,
                            jnp.where(b4, jnp.where(b2, thr[14], thr[12]), jnp.where(b2, thr[10], thr[8])),
                            jnp.where(b4, jnp.where(b2, thr[6], thr[4]), jnp.where(b2, thr[2], thr[0]))))
        b0 = pred(thr[15])
        bit = lambda b, v: jnp.where(b, v, 0.0)
        return bit(b8, 8.0) + bit(b4, 4.0) + bit(b2, 2.0) + bit(b1, 1.0) + bit(b0, 1.0)

    def stage_c(h, carry):
        s1 = s_scr[0, h]
        s2 = s_scr[1, h]
        shape = s2.shape
        top2 = [jnp.broadcast_to(a_scr[1, k, pl.ds(h, 1), :], shape) for k in range(K)]
        phi = [jnp.broadcast_to(f_scr[k, pl.ds(h, 1), :], shape) for k in range(K)]
        rk2_ref[h] = prefix_count(lambda t: t > s2, top2).astype(BF16)
        e2_ref[h] = jnp.exp(s2 - a_scr[1, 0, pl.ds(h, 1), :]).astype(BF16)
        c1_ref[h] = prefix_count(lambda t: s1 >= t, phi)
        e1_ref[h] = jnp.exp(s1 - a_scr[0, 0, pl.ds(h, 1), :]) * f_scr[K, pl.ds(h, 1), :]
        return carry

    lax.fori_loop(0, PEER_HEADS, stage_c, 0)


def _peer_route(h2t, wqt, keys):
    D, T = h2t.shape
    tb = PEER_TB
    PH = PEER_HEADS
    blk = pl.BlockSpec((PH, N_KEYS, tb), lambda i: (0, 0, i))
    return pl.pallas_call(
        _route_kernel,
        grid=(T // tb,),
        in_specs=[pl.BlockSpec((D, tb), lambda i: (0, i)),
                  pl.BlockSpec((PH * PEER_DKEY, D), lambda i: (0, 0)),
                  pl.BlockSpec((2, N_KEYS, PEER_DKEY // 2), lambda i: (0, 0, 0))],
        out_specs=[blk, blk, blk, blk],
        out_shape=[jax.ShapeDtypeStruct((PH, N_KEYS, T), BF16),
                   jax.ShapeDtypeStruct((PH, N_KEYS, T), BF16),
                   jax.ShapeDtypeStruct((PH, N_KEYS, T), F32),
                   jax.ShapeDtypeStruct((PH, N_KEYS, T), F32)],
        scratch_shapes=[pltpu.VMEM((PH * PEER_DKEY, tb), BF16),
                        pltpu.VMEM((2, PH, N_KEYS, tb), F32),
                        pltpu.VMEM((2, PEER_TOPK, PH, tb), F32),
                        pltpu.VMEM((PEER_TOPK + 1, PH, tb), F32)],
        compiler_params=_cparams(("parallel",)),
        name="peer_route",
    )(h2t, wqt, keys)


EXP_TB = 512
EXP_EB = 1024
BF16_ROWS = 16
LANES = 256


def _expert_step_order(nsub, ngroup):
    order = []
    per = nsub // ngroup
    for g in range(ngroup):
        order.append(("v", g))
        for j in range(g * per, (g + 1) * per):
            order += [("k", j), ("g", j)]
    return tuple(order)


def _experts_kernel(h_ref, u_ref, vt_ref, rk2_ref, e2_ref, c1_ref, e1_ref, x1_ref, mod_ref, g2_ref, b2_ref,
                    o_ref, acc_scr, at_scr, w0_scr, w1_scr):
    e = pl.program_id(1)
    ne = N_EXPERTS // EXP_EB
    nsub = EXP_EB // N_KEYS
    ngrp = N_KEYS // BF16_ROWS
    tb = h_ref.shape[1]
    nhalf = tb // LANES
    assert nsub % nhalf == 0

    def value_matmul(w_read, half):
        cols = slice(half * LANES, (half + 1) * LANES)
        acc_scr[:, cols] += _dot(vt_ref[...], w_read[:, cols])

    def key_matmul(j):
        krows = slice(j * N_KEYS, (j + 1) * N_KEYS)
        at_scr[krows, :] = _dot(u_ref[krows, :], h_ref[...]).astype(BF16)

    def gated_activation(j, w_write):
        for lt in range(nhalf):
            cols = slice(lt * LANES, (lt + 1) * LANES)
            gates = [None] * ngrp
            for h in range(PEER_HEADS):
                cnt = jnp.broadcast_to(c1_ref[h, j:j + 1, cols], (BF16_ROWS, LANES)).astype(BF16)
                e1 = jnp.broadcast_to(e1_ref[h, j:j + 1, cols], (BF16_ROWS, LANES)).astype(BF16)
                for r in range(ngrp):
                    rows = slice(r * BF16_ROWS, (r + 1) * BF16_ROWS)
                    term = jnp.where(rk2_ref[h, rows, cols] < cnt, e2_ref[h, rows, cols],
                                     jnp.zeros((), BF16)) * e1
                    gates[r] = term if gates[r] is None else gates[r] + term
            for r in range(ngrp):
                rows = slice(j * N_KEYS + r * BF16_ROWS, j * N_KEYS + (r + 1) * BF16_ROWS)
                a = at_scr[rows, cols]
                act = (0.5 * a) * (1.0 + lax.erf(a * (2.0 ** -0.5)))
                w_write[rows, cols] = gates[r] * act

    def step(w_write, w_read):
        for kind, idx in _expert_step_order(nsub, nhalf):
            if kind == "v":
                if w_read is not None:
                    value_matmul(w_read, idx)
            elif kind == "k":
                key_matmul(idx)
            else:
                gated_activation(idx, w_write)

    parity = lax.rem(e, 2)

    @pl.when(e == 0)
    def _():
        acc_scr[...] = jnp.zeros_like(acc_scr)
        step(w0_scr, None)

    @pl.when((e > 0) & (e < ne) & (parity == 0))
    def _():
        step(w0_scr, w1_scr)

    @pl.when((e < ne) & (parity == 1))
    def _():
        step(w1_scr, w0_scr)

    @pl.when(e == ne)
    def _():
        w_last = w1_scr if ne % 2 == 0 else w0_scr
        for half in range(nhalf):
            value_matmul(w_last, half)
        gt2 = mod_ref[0, 5:6, :]
        for half in range(nhalf):
            cols = slice(half * LANES, (half + 1) * LANES)
            yf = acc_scr[:, cols].T
            r = ALPHA * x1_ref[cols, :] + gt2 * yf
            o_ref[cols, :] = _layer_norm_rows(r) * g2_ref[...] + b2_ref[...]


def _peer_experts(h2t, u_b, vt_b, rk2, e2, c1, e1, x1, mod3, ln_g, ln_b, S):
    D, T = h2t.shape
    tb, eb = EXP_TB, EXP_EB
    PH = PEER_HEADS
    route = pl.BlockSpec((PH, N_KEYS, tb), lambda i, e: (0, 0, i))
    ne = N_EXPERTS // eb
    key1 = pl.BlockSpec((PH, eb // N_KEYS, tb), lambda i, e: (0, jnp.minimum(e, ne - 1), i))
    return pl.pallas_call(
        _experts_kernel,
        grid=(T // tb, ne + 1),
        in_specs=[pl.BlockSpec((D, tb), lambda i, e: (0, i)),
                  pl.BlockSpec((eb, D), lambda i, e: (jnp.minimum(e, ne - 1), 0)),
                  pl.BlockSpec((D, eb), lambda i, e: (0, jnp.maximum(e - 1, 0))),
                  route, route, key1, key1,
                  pl.BlockSpec((tb, D), lambda i, e: (i, 0)),
                  pl.BlockSpec((1, 6, D), lambda i, e: ((i * tb) // S, 0, 0)),
                  pl.BlockSpec((1, D), lambda i, e: (0, 0)),
                  pl.BlockSpec((1, D), lambda i, e: (0, 0))],
        out_specs=pl.BlockSpec((tb, D), lambda i, e: (i, 0)),
        out_shape=jax.ShapeDtypeStruct((T, D), F32),
        scratch_shapes=[pltpu.VMEM((D, tb), F32), pltpu.VMEM((eb, tb), BF16),
                        pltpu.VMEM((eb, tb), BF16), pltpu.VMEM((eb, tb), BF16)],
        compiler_params=_cparams(("parallel", "arbitrary")),
        name="peer_experts",
    )(h2t, u_b, vt_b, rk2, e2, c1, e1, x1, mod3, ln_g, ln_b)


def kernel(x, c, w_ada, b_ada, w_in, b_if, conv_w, conv_b, da_lambda, da_subln_g, ml_norm_g, w_br_attn,
           w_br_mlstm, w_out, ln1_g, ln1_b, peer_wq, peer_keys, peer_u, peer_v, ln2_g, ln2_b):
    B, S, D = x.shape
    T = B * S
    assert D == D_MODEL and S % DA_TQ == 0 and S % 1024 == 0
    l = 0
    lambda_init = 0.8 - 0.6 * math.exp(-0.3 * l)

    mod3 = _modulation(c, w_ada[l], b_ada[l]).reshape(B, 6, D)

    w = w_in[l]
    o_mq = 3 * D
    o_mv = o_mq + 2 * ML_HEADS * ML_DK
    o_mo = o_mv + D
    o_if = o_mo + D
    o_ga = o_if + 2 * ML_HEADS
    o_gm = o_ga + D
    starts = (0, D, 2 * D, o_mq, o_mv, o_mo, o_ga, o_gm)
    q_fold = (DA_DK ** -0.5) * math.log2(math.e)
    pieces = [w[:, s0:s0 + D] for s0 in starts]
    pieces[0] = pieces[0] * q_fold
    w8 = jnp.stack(pieces).astype(BF16)
    w_if = w[:, o_if:o_if + 2 * ML_HEADS]
    wg = jnp.pad(w_if, ((0, 0), (0, 128 - 2 * ML_HEADS))).astype(BF16)
    wgt = w_if.T.astype(BF16)
    bias8 = b_if[l].reshape(2 * ML_HEADS)
    bcol = jnp.pad(bias8, (0, 128 - 2 * ML_HEADS)).reshape(1, 128)
    brow = bias8.reshape(2 * ML_HEADS, 1)

    x2 = x.reshape(T, D)
    z, gcol, grow = _in_proj(x2, mod3, w8, wg, wgt, bcol, brow, S)
    z4 = z.reshape(8, B, S, D)

    ya = _diff_attention(z4, da_lambda[l], da_subln_g[l], B, S, lambda_init)
    ym = _mlstm(z4, gcol.reshape(B, S, 128), grow, conv_w[l], conv_b[l].reshape(1, -1),
                ml_norm_g[l].reshape(1, -1), B, S)

    x1, h2 = _merge(ya.reshape(T, D), ym.reshape(T, D), z, x2, mod3,
                    w_br_attn[l].astype(BF16), w_br_mlstm[l].astype(BF16), w_out[l].astype(BF16),
                    ln1_g[l].reshape(1, D), ln1_b[l].reshape(1, D), S)

    rk2, e2, c1, e1 = _peer_route(h2, peer_wq[l].T.astype(BF16), peer_keys[l].astype(BF16))
    out = _peer_experts(h2, peer_u[l].astype(BF16), peer_v[l].T.astype(BF16), rk2, e2, c1, e1, x1, mod3,
                        ln2_g[l].reshape(1, D), ln2_b[l].reshape(1, D), S)
    return out.reshape(B, S, D)
```

```python
import functools
import math

import jax
import jax.numpy as jnp
from jax import lax
from jax.experimental import pallas as pl
from jax.experimental.pallas import tpu as pltpu

D_MODEL = 1024
DA_HEADS = 8
DA_DK = 64
DA_DV = 2 * DA_DK
ML_HEADS = 4
ML_DK = 128
ML_DV = 256
ML_CHUNK = 128
CONV_K = 4
PEER_HEADS = 8
PEER_TOPK = 16
N_KEYS = 128
N_EXPERTS = N_KEYS * N_KEYS
PEER_DKEY = 128
DEPTH = 1
ALPHA = (2 * DEPTH) ** 0.25
LN_EPS = 1e-5

F32 = jnp.float32
BF16 = jnp.bfloat16
NEG_INF = float("-inf")

VMEM_LIMIT_BYTES = 56 * 1024 * 1024


def _cparams(sem):
    return pltpu.CompilerParams(dimension_semantics=sem, vmem_limit_bytes=VMEM_LIMIT_BYTES)


def _layer_norm_rows(x):
    mu = jnp.mean(x, axis=-1, keepdims=True)
    xc = x - mu
    var = jnp.mean(xc * xc, axis=-1, keepdims=True)
    return xc * lax.rsqrt(var + LN_EPS)


def _dot(a, b):
    return jnp.dot(a, b, preferred_element_type=F32)


def _dot_nt(a, b):
    return lax.dot_general(a, b, (((1,), (1,)), ((), ())), preferred_element_type=F32)


def _dot_tn(a, b):
    return lax.dot_general(a, b, (((0,), (0,)), ((), ())), preferred_element_type=F32)


def _mod_kernel(c_ref, w_ref, b_ref, o_ref):
    c = c_ref[...]
    a = c * jax.nn.sigmoid(c)
    o_ref[...] = jnp.dot(a, w_ref[...], preferred_element_type=F32,
                         precision=lax.Precision.HIGHEST) + b_ref[...]


def _modulation(c, w_ada, b_ada):
    B, D = c.shape
    N = w_ada.shape[1]
    tn = 1024
    return pl.pallas_call(
        _mod_kernel,
        grid=(N // tn,),
        in_specs=[pl.BlockSpec((B, D), lambda n: (0, 0)),
                  pl.BlockSpec((D, tn), lambda n: (0, n)),
                  pl.BlockSpec((1, tn), lambda n: (0, n))],
        out_specs=pl.BlockSpec((B, tn), lambda n: (0, n)),
        out_shape=jax.ShapeDtypeStruct((B, N), F32),
        compiler_params=_cparams(("arbitrary",)),
        name="modulation",
    )(c, w_ada, b_ada.reshape(1, N))


INPROJ_ROWS = 256


def _inproj_kernel(x_ref, mod_ref, w_ref, wg_ref, wgt_ref, bcol_ref, brow_ref,
                   z_ref, gcol_ref, grow_ref, h_scr):
    n = pl.program_id(1)

    @pl.when(n == 0)
    def _():
        sh1 = mod_ref[0, 0:1, :]
        sc1 = mod_ref[0, 1:2, :]
        tm = x_ref.shape[0]
        for c in range(tm // INPROJ_ROWS):
            rows = slice(c * INPROJ_ROWS, (c + 1) * INPROJ_ROWS)
            hb = (_layer_norm_rows(x_ref[rows, :]) * (1.0 + sc1) + sh1).astype(BF16)
            h_scr[rows, :] = hb
            z_ref[0, rows, :] = _dot(hb, w_ref[0]).astype(BF16)
            gcol_ref[rows, :] = _dot(hb, wg_ref[...]) + bcol_ref[...]
            grow_ref[:, rows] = _dot_nt(wgt_ref[...], hb) + brow_ref[...]

    @pl.when(n > 0)
    def _():
        z_ref[0] = _dot(h_scr[...], w_ref[0]).astype(BF16)


def _in_proj(x2, mod3, w8, wg, wgt, bcol, brow, S):
    T, D = x2.shape
    tm = 1024
    npiece = w8.shape[0]
    return pl.pallas_call(
        _inproj_kernel,
        grid=(T // tm, npiece),
        in_specs=[pl.BlockSpec((tm, D), lambda i, n: (i, 0)),
                  pl.BlockSpec((1, 6, D), lambda i, n: ((i * tm) // S, 0, 0)),
                  pl.BlockSpec((1, D, D), lambda i, n: (n, 0, 0)),
                  pl.BlockSpec((D, 128), lambda i, n: (0, 0)),
                  pl.BlockSpec((8, D), lambda i, n: (0, 0)),
                  pl.BlockSpec((1, 128), lambda i, n: (0, 0)),
                  pl.BlockSpec((8, 1), lambda i, n: (0, 0))],
        out_specs=[pl.BlockSpec((1, tm, D), lambda i, n: (n, i, 0)),
                   pl.BlockSpec((tm, 128), lambda i, n: (i, 0)),
                   pl.BlockSpec((8, tm), lambda i, n: (0, i))],
        out_shape=[jax.ShapeDtypeStruct((npiece, T, D), BF16),
                   jax.ShapeDtypeStruct((T, 128), F32),
                   jax.ShapeDtypeStruct((8, T), F32)],
        scratch_shapes=[pltpu.VMEM((tm, D), BF16)],
        compiler_params=_cparams(("parallel", "arbitrary")),
        name="in_proj",
    )(x2, mod3, w8, wg, wgt, bcol, brow)


DA_TQ = 256


def _diffattn_kernel(lam_ref, g_ref, q_ref, k_ref, v_ref, o_ref, s_scr, *, S, lambda_init):
    tq = DA_TQ
    nq = S // tq
    lam = lam_ref[...]
    t1 = jnp.sum(lam[0:1] * lam[1:2], axis=-1, keepdims=True)
    t2 = jnp.sum(lam[2:3] * lam[3:4], axis=-1, keepdims=True)
    lam_val = jnp.exp(t1) - jnp.exp(t2) + lambda_init
    first_map = lax.broadcasted_iota(jnp.int32, (1, DA_DV), 1) < DA_DK
    gain = g_ref[...] * (1.0 - lambda_init)
    row = lax.broadcasted_iota(jnp.int32, (tq, tq), 0)
    col = lax.broadcasted_iota(jnp.int32, (tq, tq), 1)
    causal = col <= row

    def scores(qi):
        q0 = qi * tq
        qs = q_ref[0, 0, q0:q0 + tq, :]
        zero = jnp.zeros_like(qs)
        for mp, qm in enumerate((jnp.where(first_map, qs, zero), jnp.where(first_map, zero, qs))):
            s_scr[qi % 2, mp, :, 0:q0 + tq] = _dot_nt(qm, k_ref[0, 0, 0:q0 + tq, :])

    def finish(qi):
        q0 = qi * tq
        slot = qi % 2
        probs = []
        for mp in range(2):
            s_diag = jnp.where(causal, s_scr[slot, mp, :, q0:q0 + tq], NEG_INF)
            m = jnp.max(s_diag, axis=-1, keepdims=True)
            if qi > 0:
                s_off = s_scr[slot, mp, :, 0:q0]
                m = jnp.maximum(m, jnp.max(s_off, axis=-1, keepdims=True))
            p_diag = jnp.exp2(s_diag - m)
            l = jnp.sum(p_diag, axis=-1, keepdims=True)
            p_off = None
            if qi > 0:
                p_off = jnp.exp2(s_off - m)
                l = l + jnp.sum(p_off, axis=-1, keepdims=True)
            probs.append((p_diag, p_off, l))
        (p1d, p1o, l1), (p2d, p2o, l2) = probs
        ratio = lam_val * l1 / l2
        o = _dot((p1d - p2d * ratio).astype(BF16), v_ref[0, 0, q0:q0 + tq, :])
        if qi > 0:
            o = o + _dot((p1o - p2o * ratio).astype(BF16), v_ref[0, 0, 0:q0, :])
        o = o / l1
        o = o * lax.rsqrt(jnp.mean(o * o, axis=-1, keepdims=True) + LN_EPS) * gain
        o_ref[0, q0:q0 + tq, :] = o.astype(BF16)

    scores(0)
    for qi in range(nq):
        if qi + 1 < nq:
            scores(qi + 1)
        finish(qi)


def _diff_attention(z4, da_lambda, subln_g, B, S, lambda_init):
    kern = functools.partial(_diffattn_kernel, S=S, lambda_init=lambda_init)
    return pl.pallas_call(
        kern,
        grid=(B, DA_HEADS),
        in_specs=[pl.BlockSpec((4, DA_DK), lambda b, h: (0, 0)),
                  pl.BlockSpec((1, DA_DV), lambda b, h: (0, 0)),
                  pl.BlockSpec((1, 1, S, DA_DV), lambda b, h: (0, b, 0, h)),
                  pl.BlockSpec((1, 1, S, DA_DV), lambda b, h: (1, b, 0, h)),
                  pl.BlockSpec((1, 1, S, DA_DV), lambda b, h: (2, b, 0, h))],
        out_specs=pl.BlockSpec((1, S, DA_DV), lambda b, h: (b, 0, h)),
        out_shape=jax.ShapeDtypeStruct((B, S, DA_HEADS * DA_DV), BF16),
        scratch_shapes=[pltpu.VMEM((2, 2, DA_TQ, S), F32)],
        compiler_params=_cparams(("parallel", "parallel")),
        name="diff_attention",
    )(da_lambda, subln_g.reshape(1, DA_DV), z4, z4, z4)


def _mlstm_kernel(qk_ref, v_ref, og_ref, gcol_ref, grow_ref, cw_ref, cb_ref, ng_ref, o_ref,
                  qc_scr, kc_scr, c_scr, n_scr, m_scr, *, S):
    L = ML_CHUNK
    H = ML_HEADS
    srow = lax.broadcasted_iota(jnp.int32, (S, ML_DK), 0)

    for cb in range(2 * H):
        cols = slice(cb * ML_DK, (cb + 1) * ML_DK)
        x = qk_ref[0, 0, :, cols].astype(F32)
        y = x * cw_ref[CONV_K - 1:CONV_K, cols] + cb_ref[:, cols]
        for j in range(1, CONV_K):
            xs = jnp.where(srow >= j, pltpu.roll(x, j, 0), 0.0)
            y = y + xs * cw_ref[CONV_K - 1 - j:CONV_K - j, cols]
        y = y * jax.nn.sigmoid(y)
        if cb < H:
            qc_scr[:, cols] = y
        else:
            kc_scr[:, (cb - H) * ML_DK:(cb - H + 1) * ML_DK] = y * (ML_DK ** -0.5)

    r_i = lax.broadcasted_iota(jnp.int32, (L, L), 0)
    c_i = lax.broadcasted_iota(jnp.int32, (L, L), 1)
    causal = c_i <= r_i
    tril = causal.astype(F32)
    triu = (r_i <= c_i).astype(F32)
    c_scr[...] = jnp.zeros_like(c_scr)
    n_scr[...] = jnp.zeros_like(n_scr)
    m_scr[...] = jnp.zeros_like(m_scr)

    def head_chunk(hh, t0, gc, gr, b_cols, b_rows):
        Ct = c_scr[hh]
        n_row = n_scr[hh]
        m = m_scr[hh]
        q = qc_scr[pl.ds(t0, L), hh * ML_DK:(hh + 1) * ML_DK]
        k = kc_scr[pl.ds(t0, L), hh * ML_DK:(hh + 1) * ML_DK]
        v = v_ref[0, 0, pl.ds(t0, L), hh * ML_DV:(hh + 1) * ML_DV]
        ngain = ng_ref[:, hh * ML_DV:(hh + 1) * ML_DV]
        ig_col = gc[:, hh:hh + 1]
        ig_row = gr[hh:hh + 1, :]
        b_col = b_cols[:, H + hh:H + hh + 1]
        b_row = b_rows[H + hh:H + hh + 1, :]
        dm = jnp.where(causal, b_col - b_row + ig_row, NEG_INF)
        m_inter = b_col + m
        m_t = jnp.maximum(m_inter, jnp.max(dm, axis=-1, keepdims=True))
        w = jnp.exp(dm - m_t)
        qb = q.astype(BF16)
        kb = k.astype(BF16)
        p = w * _dot_nt(qb, kb)
        inter = jnp.exp(m_inter - m_t)
        num = _dot(p.astype(BF16), v) + inter * _dot(qb, Ct.astype(BF16))
        nq = jnp.sum(p, axis=-1, keepdims=True) + inter * jnp.sum(q * n_row, axis=-1, keepdims=True)
        hout = num / jnp.maximum(jnp.abs(nq), jnp.exp(-m_t))
        hout = hout * lax.rsqrt(jnp.mean(hout * hout, axis=-1, keepdims=True) + LN_EPS) * ngain
        og = og_ref[0, 0, pl.ds(t0, L), hh * ML_DV:(hh + 1) * ML_DV].astype(F32)
        o_ref[0, pl.ds(t0, L), hh * ML_DV:(hh + 1) * ML_DV] = (hout * jax.nn.sigmoid(og)).astype(BF16)
        m_new = m_t[L - 1:L, :]
        b_last = b_col[L - 1:L, :]
        decay = jnp.exp(b_last + m - m_new)
        w_s = jnp.exp(b_last - b_col + ig_col - m_new)
        c_scr[hh] = decay * Ct + _dot_tn(kb, (v.astype(F32) * w_s).astype(BF16))
        n_scr[hh] = decay * n_row + jnp.sum(k * w_s, axis=0, keepdims=True)
        m_scr[hh] = m_new

    def chunk(ci, carry):
        t0 = pl.multiple_of(ci * L, L)
        gc = gcol_ref[0, pl.ds(t0, L), :]
        gr = grow_ref[:, pl.ds(t0, L)]
        b_cols = jnp.dot(tril, jax.nn.log_sigmoid(gc), preferred_element_type=F32,
                         precision=lax.Precision.HIGHEST)
        b_rows = jnp.dot(jax.nn.log_sigmoid(gr), triu, preferred_element_type=F32,
                         precision=lax.Precision.HIGHEST)
        for hh in range(H):
            head_chunk(hh, t0, gc, gr, b_cols, b_rows)
        return carry

    lax.fori_loop(0, S // L, chunk, 0)


def _mlstm(z4, gcol3, grow, conv_w, conv_b, norm_g, B, S):
    kern = functools.partial(_mlstm_kernel, S=S)
    H = ML_HEADS
    D = H * ML_DV
    piece = lambda n: pl.BlockSpec((1, 1, S, D), lambda b: (n, b, 0, 0))
    return pl.pallas_call(
        kern,
        grid=(B,),
        in_specs=[piece(3), piece(4), piece(5),
                  pl.BlockSpec((1, S, 128), lambda b: (b, 0, 0)),
                  pl.BlockSpec((8, S), lambda b: (0, b)),
                  pl.BlockSpec((CONV_K, 2 * H * ML_DK), lambda b: (0, 0)),
                  pl.BlockSpec((1, 2 * H * ML_DK), lambda b: (0, 0)),
                  pl.BlockSpec((1, D), lambda b: (0, 0))],
        out_specs=pl.BlockSpec((1, S, D), lambda b: (b, 0, 0)),
        out_shape=jax.ShapeDtypeStruct((B, S, D), BF16),
        scratch_shapes=[pltpu.VMEM((S, H * ML_DK), F32), pltpu.VMEM((S, H * ML_DK), F32),
                        pltpu.VMEM((H, ML_DK, ML_DV), F32), pltpu.VMEM((H, 1, ML_DK), F32),
                        pltpu.VMEM((H, 1, 1), F32)],
        compiler_params=_cparams(("parallel",)),
        name="mlstm",
    )(z4, z4, z4, gcol3, grow, conv_w, conv_b, norm_g)


MERGE_ROWS = 256


def _merge_kernel(ya_ref, ym_ref, ga_ref, gm_ref, x_ref, mod_ref, wa_ref, wm_ref, wo_ref, g1_ref, b1_ref,
                  x1_ref, h2_ref):
    gt1 = mod_ref[0, 2:3, :]
    sh2 = mod_ref[0, 3:4, :]
    sc2 = mod_ref[0, 4:5, :]
    for c in range(ya_ref.shape[0] // MERGE_ROWS):
        rows = slice(c * MERGE_ROWS, (c + 1) * MERGE_ROWS)
        ya = _dot(ya_ref[rows, :], wa_ref[...])
        ym = _dot(ym_ref[rows, :], wm_ref[...])
        y = (jax.nn.sigmoid(ga_ref[0, rows, :].astype(F32)) * ya
             + jax.nn.sigmoid(gm_ref[0, rows, :].astype(F32)) * ym)
        y2 = _dot(y.astype(BF16), wo_ref[...])
        x1 = _layer_norm_rows(ALPHA * x_ref[rows, :] + gt1 * y2) * g1_ref[...] + b1_ref[...]
        x1_ref[rows, :] = x1
        h2_ref[:, rows] = (_layer_norm_rows(x1) * (1.0 + sc2) + sh2).T.astype(BF16)


def _merge(ya2, ym2, z3, x2, mod3, wa, wm, wo, ln_g, ln_b, S):
    T, D = x2.shape
    tm = 512
    tok = lambda i: (i, 0)
    const = lambda i: (0, 0)
    return pl.pallas_call(
        _merge_kernel,
        grid=(T // tm,),
        in_specs=[pl.BlockSpec((tm, D), tok), pl.BlockSpec((tm, D), tok),
                  pl.BlockSpec((1, tm, D), lambda i: (6, i, 0)),
                  pl.BlockSpec((1, tm, D), lambda i: (7, i, 0)),
                  pl.BlockSpec((tm, D), tok),
                  pl.BlockSpec((1, 6, D), lambda i: ((i * tm) // S, 0, 0)),
                  pl.BlockSpec((D, D), const), pl.BlockSpec((D, D), const), pl.BlockSpec((D, D), const),
                  pl.BlockSpec((1, D), const), pl.BlockSpec((1, D), const)],
        out_specs=[pl.BlockSpec((tm, D), tok), pl.BlockSpec((D, tm), lambda i: (0, i))],
        out_shape=[jax.ShapeDtypeStruct((T, D), F32), jax.ShapeDtypeStruct((D, T), BF16)],
        compiler_params=_cparams(("parallel",)),
        name="merge",
    )(ya2, ym2, z3, z3, x2, mod3, wa, wm, wo, ln_g, ln_b)


PEER_TB = 256
_CAND_COLS = [[k1 for k1 in range(PEER_TOPK) if (k1 + 1) * (k2 + 1) <= PEER_TOPK] for k2 in range(PEER_TOPK)]


def _batcher_pairs(n):
    pairs = []
    p = 1
    while p < n:
        k = p
        while k >= 1:
            for j in range(k % p, n - k, 2 * k):
                for i in range(min(k, n - j - k)):
                    if (i + j) // (2 * p) == (i + j + k) // (2 * p):
                        pairs.append((i + j, i + j + k))
            k //= 2
        p *= 2
    return pairs


_SORT16 = _batcher_pairs(PEER_TOPK)


def _cmpx(v, i, j):
    a, b = v[i], v[j]
    if b is None:
        return
    if a is None:
        v[i], v[j] = b, None
        return
    v[i], v[j] = jnp.maximum(a, b), jnp.minimum(a, b)


def _sort16_desc(v):
    v = list(v)
    for i, j in _SORT16:
        _cmpx(v, i, j)
    return v


def _bitonic_merge_desc(v):
    v = list(v)
    d = PEER_TOPK // 2
    while d >= 1:
        for i in range(PEER_TOPK):
            if i & d == 0:
                _cmpx(v, i, i + d)
        d //= 2
    return v


def _half_clean(a, b):
    out = []
    for g in range(PEER_TOPK):
        x, y = a[g], b[PEER_TOPK - 1 - g]
        out.append(x if y is None else (y if x is None else jnp.maximum(x, y)))
    return out


def _top16_all_sublanes(scores):
    rows = [_sort16_desc([s[g * 8:(g + 1) * 8, :] for g in range(PEER_TOPK)]) for s in scores]
    for shift in (4, 2, 1):
        partner = [[pltpu.roll(r, shift, 0) for r in rr] for rr in rows]
        rows = [_bitonic_merge_desc(_half_clean(rr, pp)) for rr, pp in zip(rows, partner)]
    return rows


def _route_kernel(h_ref, wqt_ref, keys_ref, rk2_ref, e2_ref, c1_ref, e1_ref, q_scr, s_scr, a_scr, f_scr):
    half = PEER_DKEY // 2
    K = PEER_TOPK
    q_scr[...] = _dot(wqt_ref[...], h_ref[...]).astype(BF16)

    def score_matmuls(h_src, slot):
        base = pl.multiple_of(h_src * PEER_DKEY, PEER_DKEY)
        for p in range(2):
            s_scr[p, slot] = _dot(keys_ref[p], q_scr[pl.ds(base + p * half, half), :])

    score_matmuls(0, 0)

    def stage_a(h, carry):
        score_matmuls(jnp.minimum(h + 1, PEER_HEADS - 1), h + 1)
        tops = _top16_all_sublanes([s_scr[p, h] for p in range(2)])
        for p in range(2):
            for k in range(K):
                a_scr[p, k, pl.ds(h, 1), :] = tops[p][k][0:1, :]
        return carry

    lax.fori_loop(0, PEER_HEADS, stage_a, 0)

    a1 = [a_scr[0, k] for k in range(K)]
    a2 = [a_scr[1, k] for k in range(K)]
    cand = [[a1[k1] + a2[k2] for k1 in col] for k2, col in enumerate(_CAND_COLS)]
    g0 = [cand[k2][0] for k2 in range(K)]
    rest = [cand[k2][i] for k2 in range(K) for i in range(1, len(_CAND_COLS[k2]))]
    rest += [None] * (-len(rest) % K)
    groups = [g0] + [_sort16_desc(rest[i:i + K]) for i in range(0, len(rest), K)]
    while len(groups) > 2:
        merged = [_bitonic_merge_desc(_half_clean(groups[i], groups[i + 1])) for i in range(0, len(groups) - 1, 2)]
        groups = merged + ([groups[-1]] if len(groups) % 2 else [])
    last = [x for x in _half_clean(groups[0], groups[1]) if x is not None]
    tau = functools.reduce(jnp.minimum, last)
    cmax = a1[0] + a2[0]
    zsum = None
    for k2, col in enumerate(_CAND_COLS):
        phi = None
        for i, k1 in enumerate(col):
            c = cand[k2][i]
            hit = c >= tau
            term = jnp.where(hit, jnp.exp(c - cmax), 0.0)
            zsum = term if zsum is None else zsum + term
            lo = jnp.where(hit, a1[k1], jnp.inf)
            phi = lo if phi is None else jnp.minimum(phi, lo)
        f_scr[k2] = phi
    f_scr[K] = 1.0 / zsum

    def prefix_count(pred, thr):
        b8 = pred(thr[7])
        b4 = pred(jnp.where(b8, thr[11], thr[3]))
        b2 = pred(jnp.where(b8, jnp.where(b4, thr[13], thr[9]), jnp.where(b4, thr[5], thr[1])))
        b1 = pred(jnp.where(b8,
                            jnp.where(b4, jnp.where(b2, thr[14], thr[12]), jnp.where(b2, thr[10], thr[8])),
                            jnp.where(b4, jnp.where(b2, thr[6], thr[4]), jnp.where(b2, thr[2], thr[0]))))
        b0 = pred(thr[15])
        bit = lambda b, v: jnp.where(b, v, 0.0)
        return bit(b8, 8.0) + bit(b4, 4.0) + bit(b2, 2.0) + bit(b1, 1.0) + bit(b0, 1.0)

    def stage_c(h, carry):
        s1 = s_scr[0, h]
        s2 = s_scr[1, h]
        shape = s2.shape
        top2 = [jnp.broadcast_to(a_scr[1, k, pl.ds(h, 1), :], shape) for k in range(K)]
        phi = [jnp.broadcast_to(f_scr[k, pl.ds(h, 1), :], shape) for k in range(K)]
        rk2_ref[h] = prefix_count(lambda t: t > s2, top2).astype(BF16)
        e2_ref[h] = jnp.exp(s2 - a_scr[1, 0, pl.ds(h, 1), :]).astype(BF16)
        c1_ref[h] = prefix_count(lambda t: s1 >= t, phi)
        e1_ref[h] = jnp.exp(s1 - a_scr[0, 0, pl.ds(h, 1), :]) * f_scr[K, pl.ds(h, 1), :]
        return carry

    lax.fori_loop(0, PEER_HEADS, stage_c, 0)


def _peer_route(h2t, wqt, keys):
    D, T = h2t.shape
    tb = PEER_TB
    PH = PEER_HEADS
    blk = pl.BlockSpec((PH, N_KEYS, tb), lambda i: (0, 0, i))
    return pl.pallas_call(
        _route_kernel,
        grid=(T // tb,),
        in_specs=[pl.BlockSpec((D, tb), lambda i: (0, i)),
                  pl.BlockSpec((PH * PEER_DKEY, D), lambda i: (0, 0)),
                  pl.BlockSpec((2, N_KEYS, PEER_DKEY // 2), lambda i: (0, 0, 0))],
        out_specs=[blk, blk, blk, blk],
        out_shape=[jax.ShapeDtypeStruct((PH, N_KEYS, T), BF16),
                   jax.ShapeDtypeStruct((PH, N_KEYS, T), BF16),
                   jax.ShapeDtypeStruct((PH, N_KEYS, T), F32),
                   jax.ShapeDtypeStruct((PH, N_KEYS, T), F32)],
        scratch_shapes=[pltpu.VMEM((PH * PEER_DKEY, tb), BF16),
                        pltpu.VMEM((2, PH + 1, N_KEYS, tb), F32),
                        pltpu.VMEM((2, PEER_TOPK, PH, tb), F32),
                        pltpu.VMEM((PEER_TOPK + 1, PH, tb), F32)],
        compiler_params=_cparams(("parallel",)),
        name="peer_route",
    )(h2t, wqt, keys)


EXP_TB = 512
EXP_EB = 1024
BF16_ROWS = 16
LANES = 256


def _expert_step_order(nsub, ngroup):
    order = []
    per = nsub // ngroup
    for g in range(ngroup):
        order.append(("v", g))
        for j in range(g * per, (g + 1) * per):
            order += [("k", j), ("g", j)]
    return tuple(order)


def _experts_kernel(h_ref, u_ref, vt_ref, rk2_ref, e2_ref, c1_ref, e1_ref, x1_ref, mod_ref, g2_ref, b2_ref,
                    o_ref, acc_scr, at_scr, w0_scr, w1_scr):
    e = pl.program_id(1)
    ne = N_EXPERTS // EXP_EB
    nsub = EXP_EB // N_KEYS
    ngrp = N_KEYS // BF16_ROWS
    tb = h_ref.shape[1]
    nhalf = tb // LANES
    assert nsub % nhalf == 0

    def value_matmul(w_read, half):
        cols = slice(half * LANES, (half + 1) * LANES)
        acc_scr[:, cols] += _dot(vt_ref[...], w_read[:, cols])

    def key_matmul(j):
        krows = slice(j * N_KEYS, (j + 1) * N_KEYS)
        at_scr[krows, :] = _dot(u_ref[krows, :], h_ref[...]).astype(BF16)

    def gated_activation(j, w_write):
        for lt in range(nhalf):
            cols = slice(lt * LANES, (lt + 1) * LANES)
            gates = [None] * ngrp
            for h in range(PEER_HEADS):
                cnt = jnp.broadcast_to(c1_ref[h, j:j + 1, cols], (BF16_ROWS, LANES)).astype(BF16)
                e1 = jnp.broadcast_to(e1_ref[h, j:j + 1, cols], (BF16_ROWS, LANES)).astype(BF16)
                for r in range(ngrp):
                    rows = slice(r * BF16_ROWS, (r + 1) * BF16_ROWS)
                    term = jnp.where(rk2_ref[h, rows, cols] < cnt, e2_ref[h, rows, cols],
                                     jnp.zeros((), BF16)) * e1
                    gates[r] = term if gates[r] is None else gates[r] + term
            for r in range(ngrp):
                rows = slice(j * N_KEYS + r * BF16_ROWS, j * N_KEYS + (r + 1) * BF16_ROWS)
                a = at_scr[rows, cols]
                act = (0.5 * a) * (1.0 + lax.erf(a * (2.0 ** -0.5)))
                w_write[rows, cols] = gates[r] * act

    def step(w_write, w_read):
        for kind, idx in _expert_step_order(nsub, nhalf):
            if kind == "v":
                value_matmul(w_read, idx)
            elif kind == "k":
                key_matmul(idx)
            else:
                gated_activation(idx, w_write)

    parity = lax.rem(e, 2)

    @pl.when(e == 0)
    def _():
        acc_scr[...] = jnp.zeros_like(acc_scr)
        w1_scr[...] = jnp.zeros_like(w1_scr)

    @pl.when((e < ne) & (parity == 0))
    def _():
        step(w0_scr, w1_scr)

    @pl.when((e < ne) & (parity == 1))
    def _():
        step(w1_scr, w0_scr)

    @pl.when(e == ne)
    def _():
        w_last = w1_scr if ne % 2 == 0 else w0_scr
        for half in range(nhalf):
            value_matmul(w_last, half)
        gt2 = mod_ref[0, 5:6, :]
        for half in range(nhalf):
            cols = slice(half * LANES, (half + 1) * LANES)
            yf = acc_scr[:, cols].T
            r = ALPHA * x1_ref[cols, :] + gt2 * yf
            o_ref[cols, :] = _layer_norm_rows(r) * g2_ref[...] + b2_ref[...]


def _peer_experts(h2t, u_b, vt_b, rk2, e2, c1, e1, x1, mod3, ln_g, ln_b, S):
    D, T = h2t.shape
    tb, eb = EXP_TB, EXP_EB
    PH = PEER_HEADS
    route = pl.BlockSpec((PH, N_KEYS, tb), lambda i, e: (0, 0, i))
    ne = N_EXPERTS // eb
    key1 = pl.BlockSpec((PH, eb // N_KEYS, tb), lambda i, e: (0, jnp.minimum(e, ne - 1), i))
    return pl.pallas_call(
        _experts_kernel,
        grid=(T // tb, ne + 1),
        in_specs=[pl.BlockSpec((D, tb), lambda i, e: (0, i)),
                  pl.BlockSpec((eb, D), lambda i, e: (jnp.minimum(e, ne - 1), 0)),
                  pl.BlockSpec((D, eb), lambda i, e: (0, jnp.maximum(e - 1, 0))),
                  route, route, key1, key1,
                  pl.BlockSpec((tb, D), lambda i, e: (i, 0)),
                  pl.BlockSpec((1, 6, D), lambda i, e: ((i * tb) // S, 0, 0)),
                  pl.BlockSpec((1, D), lambda i, e: (0, 0)),
                  pl.BlockSpec((1, D), lambda i, e: (0, 0))],
        out_specs=pl.BlockSpec((tb, D), lambda i, e: (i, 0)),
        out_shape=jax.ShapeDtypeStruct((T, D), F32),
        scratch_shapes=[pltpu.VMEM((D, tb), F32), pltpu.VMEM((eb, tb), BF16),
                        pltpu.VMEM((eb, tb), BF16), pltpu.VMEM((eb, tb), BF16)],
        compiler_params=_cparams(("parallel", "arbitrary")),
        name="peer_experts",
    )(h2t, u_b, vt_b, rk2, e2, c1, e1, x1, mod3, ln_g, ln_b)


def kernel(x, c, w_ada, b_ada, w_in, b_if, conv_w, conv_b, da_lambda, da_subln_g, ml_norm_g, w_br_attn,
           w_br_mlstm, w_out, ln1_g, ln1_b, peer_wq, peer_keys, peer_u, peer_v, ln2_g, ln2_b):
    B, S, D = x.shape
    T = B * S
    assert D == D_MODEL and S % DA_TQ == 0 and S % 1024 == 0
    l = 0
    lambda_init = 0.8 - 0.6 * math.exp(-0.3 * l)

    mod3 = _modulation(c, w_ada[l], b_ada[l]).reshape(B, 6, D)

    w = w_in[l]
    o_mq = 3 * D
    o_mv = o_mq + 2 * ML_HEADS * ML_DK
    o_mo = o_mv + D
    o_if = o_mo + D
    o_ga = o_if + 2 * ML_HEADS
    o_gm = o_ga + D
    starts = (0, D, 2 * D, o_mq, o_mv, o_mo, o_ga, o_gm)
    q_fold = (DA_DK ** -0.5) * math.log2(math.e)
    pieces = [w[:, s0:s0 + D] for s0 in starts]
    pieces[0] = pieces[0] * q_fold
    w8 = jnp.stack(pieces).astype(BF16)
    w_if = w[:, o_if:o_if + 2 * ML_HEADS]
    wg = jnp.pad(w_if, ((0, 0), (0, 128 - 2 * ML_HEADS))).astype(BF16)
    wgt = w_if.T.astype(BF16)
    bias8 = b_if[l].reshape(2 * ML_HEADS)
    bcol = jnp.pad(bias8, (0, 128 - 2 * ML_HEADS)).reshape(1, 128)
    brow = bias8.reshape(2 * ML_HEADS, 1)

    x2 = x.reshape(T, D)
    z, gcol, grow = _in_proj(x2, mod3, w8, wg, wgt, bcol, brow, S)
    z4 = z.reshape(8, B, S, D)

    ya = _diff_attention(z4, da_lambda[l], da_subln_g[l], B, S, lambda_init)
    ym = _mlstm(z4, gcol.reshape(B, S, 128), grow, conv_w[l], conv_b[l].reshape(1, -1),
                ml_norm_g[l].reshape(1, -1), B, S)

    x1, h2 = _merge(ya.reshape(T, D), ym.reshape(T, D), z, x2, mod3,
                    w_br_attn[l].astype(BF16), w_br_mlstm[l].astype(BF16), w_out[l].astype(BF16),
                    ln1_g[l].reshape(1, D), ln1_b[l].reshape(1, D), S)

    rk2, e2, c1, e1 = _peer_route(h2, peer_wq[l].T.astype(BF16), peer_keys[l].astype(BF16))
    out = _peer_experts(h2, peer_u[l].astype(BF16), peer_v[l].T.astype(BF16), rk2, e2, c1, e1, x1, mod3,
                        ln2_g[l].reshape(1, D), ln2_b[l].reshape(1, D), S)
    return out.reshape(B, S, D)
```

```python
import functools
import math

import jax
import jax.numpy as jnp
from jax import lax
from jax.experimental import pallas as pl
from jax.experimental.pallas import tpu as pltpu

D_MODEL = 1024
DA_HEADS = 8
DA_DK = 64
DA_DV = 2 * DA_DK
ML_HEADS = 4
ML_DK = 128
ML_DV = 256
ML_CHUNK = 128
CONV_K = 4
PEER_HEADS = 8
PEER_TOPK = 16
N_KEYS = 128
N_EXPERTS = N_KEYS * N_KEYS
PEER_DKEY = 128
DEPTH = 1
ALPHA = (2 * DEPTH) ** 0.25
LN_EPS = 1e-5

F32 = jnp.float32
BF16 = jnp.bfloat16
NEG_INF = float("-inf")

VMEM_LIMIT_BYTES = 56 * 1024 * 1024


def _cparams(sem):
    return pltpu.CompilerParams(dimension_semantics=sem, vmem_limit_bytes=VMEM_LIMIT_BYTES)


def _layer_norm_rows(x):
    mu = jnp.mean(x, axis=-1, keepdims=True)
    xc = x - mu
    var = jnp.mean(xc * xc, axis=-1, keepdims=True)
    return xc * lax.rsqrt(var + LN_EPS)


def _dot(a, b):
    return jnp.dot(a, b, preferred_element_type=F32)


def _dot_nt(a, b):
    return lax.dot_general(a, b, (((1,), (1,)), ((), ())), preferred_element_type=F32)


def _dot_tn(a, b):
    return lax.dot_general(a, b, (((0,), (0,)), ((), ())), preferred_element_type=F32)


def _mod_kernel(c_ref, w_ref, b_ref, o_ref):
    c = c_ref[...]
    a = c * jax.nn.sigmoid(c)
    o_ref[...] = jnp.dot(a, w_ref[...], preferred_element_type=F32,
                         precision=lax.Precision.HIGHEST) + b_ref[...]


def _modulation(c, w_ada, b_ada):
    B, D = c.shape
    N = w_ada.shape[1]
    tn = 1024
    return pl.pallas_call(
        _mod_kernel,
        grid=(N // tn,),
        in_specs=[pl.BlockSpec((B, D), lambda n: (0, 0)),
                  pl.BlockSpec((D, tn), lambda n: (0, n)),
                  pl.BlockSpec((1, tn), lambda n: (0, n))],
        out_specs=pl.BlockSpec((B, tn), lambda n: (0, n)),
        out_shape=jax.ShapeDtypeStruct((B, N), F32),
        compiler_params=_cparams(("arbitrary",)),
        name="modulation",
    )(c, w_ada, b_ada.reshape(1, N))


INPROJ_ROWS = 256


def _inproj_kernel(x_ref, mod_ref, w_ref, wg_ref, wgt_ref, bcol_ref, brow_ref,
                   z_ref, gcol_ref, grow_ref, h_scr):
    n = pl.program_id(1)

    @pl.when(n == 0)
    def _():
        sh1 = mod_ref[0, 0:1, :]
        sc1 = mod_ref[0, 1:2, :]
        tm = x_ref.shape[0]
        for c in range(tm // INPROJ_ROWS):
            rows = slice(c * INPROJ_ROWS, (c + 1) * INPROJ_ROWS)
            hb = (_layer_norm_rows(x_ref[rows, :]) * (1.0 + sc1) + sh1).astype(BF16)
            h_scr[rows, :] = hb
            z_ref[0, rows, :] = _dot(hb, w_ref[0]).astype(BF16)
            gcol_ref[rows, :] = _dot(hb, wg_ref[...]) + bcol_ref[...]
            grow_ref[:, rows] = _dot_nt(wgt_ref[...], hb) + brow_ref[...]

    @pl.when(n > 0)
    def _():
        z_ref[0] = _dot(h_scr[...], w_ref[0]).astype(BF16)


def _in_proj(x2, mod3, w8, wg, wgt, bcol, brow, S):
    T, D = x2.shape
    tm = 1024
    npiece = w8.shape[0]
    return pl.pallas_call(
        _inproj_kernel,
        grid=(T // tm, npiece),
        in_specs=[pl.BlockSpec((tm, D), lambda i, n: (i, 0)),
                  pl.BlockSpec((1, 6, D), lambda i, n: ((i * tm) // S, 0, 0)),
                  pl.BlockSpec((1, D, D), lambda i, n: (n, 0, 0)),
                  pl.BlockSpec((D, 128), lambda i, n: (0, 0)),
                  pl.BlockSpec((8, D), lambda i, n: (0, 0)),
                  pl.BlockSpec((1, 128), lambda i, n: (0, 0)),
                  pl.BlockSpec((8, 1), lambda i, n: (0, 0))],
        out_specs=[pl.BlockSpec((1, tm, D), lambda i, n: (n, i, 0)),
                   pl.BlockSpec((tm, 128), lambda i, n: (i, 0)),
                   pl.BlockSpec((8, tm), lambda i, n: (0, i))],
        out_shape=[jax.ShapeDtypeStruct((npiece, T, D), BF16),
                   jax.ShapeDtypeStruct((T, 128), F32),
                   jax.ShapeDtypeStruct((8, T), F32)],
        scratch_shapes=[pltpu.VMEM((tm, D), BF16)],
        compiler_params=_cparams(("parallel", "arbitrary")),
        name="in_proj",
    )(x2, mod3, w8, wg, wgt, bcol, brow)


DA_TQ = 256


def _diffattn_kernel(lam_ref, g_ref, q_ref, k_ref, v_ref, o_ref, s_scr, *, S, lambda_init):
    tq = DA_TQ
    nq = S // tq
    lam = lam_ref[...]
    t1 = jnp.sum(lam[0:1] * lam[1:2], axis=-1, keepdims=True)
    t2 = jnp.sum(lam[2:3] * lam[3:4], axis=-1, keepdims=True)
    lam_val = jnp.exp(t1) - jnp.exp(t2) + lambda_init
    first_map = lax.broadcasted_iota(jnp.int32, (1, DA_DV), 1) < DA_DK
    gain = g_ref[...] * (1.0 - lambda_init)
    row = lax.broadcasted_iota(jnp.int32, (tq, tq), 0)
    col = lax.broadcasted_iota(jnp.int32, (tq, tq), 1)
    causal = col <= row

    def scores(qi):
        q0 = qi * tq
        qs = q_ref[0, 0, q0:q0 + tq, :]
        zero = jnp.zeros_like(qs)
        for mp, qm in enumerate((jnp.where(first_map, qs, zero), jnp.where(first_map, zero, qs))):
            s_scr[qi % 2, mp, :, 0:q0 + tq] = _dot_nt(qm, k_ref[0, 0, 0:q0 + tq, :])

    def finish(qi):
        q0 = qi * tq
        slot = qi % 2
        probs = []
        for mp in range(2):
            s_diag = jnp.where(causal, s_scr[slot, mp, :, q0:q0 + tq], NEG_INF)
            m = jnp.max(s_diag, axis=-1, keepdims=True)
            if qi > 0:
                s_off = s_scr[slot, mp, :, 0:q0]
                m = jnp.maximum(m, jnp.max(s_off, axis=-1, keepdims=True))
            p_diag = jnp.exp2(s_diag - m)
            l = jnp.sum(p_diag, axis=-1, keepdims=True)
            p_off = None
            if qi > 0:
                p_off = jnp.exp2(s_off - m)
                l = l + jnp.sum(p_off, axis=-1, keepdims=True)
            probs.append((p_diag, p_off, l))
        (p1d, p1o, l1), (p2d, p2o, l2) = probs
        ratio = lam_val * l1 / l2
        o = _dot((p1d - p2d * ratio).astype(BF16), v_ref[0, 0, q0:q0 + tq, :])
        if qi > 0:
            o = o + _dot((p1o - p2o * ratio).astype(BF16), v_ref[0, 0, 0:q0, :])
        o = o / l1
        o = o * lax.rsqrt(jnp.mean(o * o, axis=-1, keepdims=True) + LN_EPS) * gain
        o_ref[0, q0:q0 + tq, :] = o.astype(BF16)

    scores(0)
    for qi in range(nq):
        if qi + 1 < nq:
            scores(qi + 1)
        finish(qi)


def _diff_attention(z4, da_lambda, subln_g, B, S, lambda_init):
    kern = functools.partial(_diffattn_kernel, S=S, lambda_init=lambda_init)
    return pl.pallas_call(
        kern,
        grid=(B, DA_HEADS),
        in_specs=[pl.BlockSpec((4, DA_DK), lambda b, h: (0, 0)),
                  pl.BlockSpec((1, DA_DV), lambda b, h: (0, 0)),
                  pl.BlockSpec((1, 1, S, DA_DV), lambda b, h: (0, b, 0, h)),
                  pl.BlockSpec((1, 1, S, DA_DV), lambda b, h: (1, b, 0, h)),
                  pl.BlockSpec((1, 1, S, DA_DV), lambda b, h: (2, b, 0, h))],
        out_specs=pl.BlockSpec((1, S, DA_DV), lambda b, h: (b, 0, h)),
        out_shape=jax.ShapeDtypeStruct((B, S, DA_HEADS * DA_DV), BF16),
        scratch_shapes=[pltpu.VMEM((2, 2, DA_TQ, S), F32)],
        compiler_params=_cparams(("parallel", "parallel")),
        name="diff_attention",
    )(da_lambda, subln_g.reshape(1, DA_DV), z4, z4, z4)


def _mlstm_kernel(qk_ref, v_ref, og_ref, gcol_ref, grow_ref, cw_ref, cb_ref, ng_ref, o_ref,
                  qc_scr, kc_scr, c_scr, n_scr, m_scr, *, S):
    L = ML_CHUNK
    H = ML_HEADS
    srow = lax.broadcasted_iota(jnp.int32, (S, ML_DK), 0)

    for cb in range(2 * H):
        cols = slice(cb * ML_DK, (cb + 1) * ML_DK)
        x = qk_ref[0, 0, :, cols].astype(F32)
        y = x * cw_ref[CONV_K - 1:CONV_K, cols] + cb_ref[:, cols]
        for j in range(1, CONV_K):
            xs = jnp.where(srow >= j, pltpu.roll(x, j, 0), 0.0)
            y = y + xs * cw_ref[CONV_K - 1 - j:CONV_K - j, cols]
        y = y * jax.nn.sigmoid(y)
        if cb < H:
            qc_scr[:, cols] = y
        else:
            kc_scr[:, (cb - H) * ML_DK:(cb - H + 1) * ML_DK] = y * (ML_DK ** -0.5)

    r_i = lax.broadcasted_iota(jnp.int32, (L, L), 0)
    c_i = lax.broadcasted_iota(jnp.int32, (L, L), 1)
    causal = c_i <= r_i
    tril = causal.astype(F32)
    triu = (r_i <= c_i).astype(F32)
    c_scr[...] = jnp.zeros_like(c_scr)
    n_scr[...] = jnp.zeros_like(n_scr)
    m_scr[...] = jnp.zeros_like(m_scr)

    def head_chunk(hh, t0, gc, gr, b_cols, b_rows):
        Ct = c_scr[hh]
        n_row = n_scr[hh]
        m = m_scr[hh]
        q = qc_scr[pl.ds(t0, L), hh * ML_DK:(hh + 1) * ML_DK]
        k = kc_scr[pl.ds(t0, L), hh * ML_DK:(hh + 1) * ML_DK]
        v = v_ref[0, 0, pl.ds(t0, L), hh * ML_DV:(hh + 1) * ML_DV]
        ngain = ng_ref[:, hh * ML_DV:(hh + 1) * ML_DV]
        ig_col = gc[:, hh:hh + 1]
        ig_row = gr[hh:hh + 1, :]
        b_col = b_cols[:, H + hh:H + hh + 1]
        b_row = b_rows[H + hh:H + hh + 1, :]
        dm = jnp.where(causal, b_col - b_row + ig_row, NEG_INF)
        m_inter = b_col + m
        m_t = jnp.maximum(m_inter, jnp.max(dm, axis=-1, keepdims=True))
        w = jnp.exp(dm - m_t)
        qb = q.astype(BF16)
        kb = k.astype(BF16)
        p = w * _dot_nt(qb, kb)
        inter = jnp.exp(m_inter - m_t)
        num = _dot(p.astype(BF16), v) + inter * _dot(qb, Ct.astype(BF16))
        nq = jnp.sum(p, axis=-1, keepdims=True) + inter * jnp.sum(q * n_row, axis=-1, keepdims=True)
        hout = num / jnp.maximum(jnp.abs(nq), jnp.exp(-m_t))
        hout = hout * lax.rsqrt(jnp.mean(hout * hout, axis=-1, keepdims=True) + LN_EPS) * ngain
        og = og_ref[0, 0, pl.ds(t0, L), hh * ML_DV:(hh + 1) * ML_DV].astype(F32)
        o_ref[0, pl.ds(t0, L), hh * ML_DV:(hh + 1) * ML_DV] = (hout * jax.nn.sigmoid(og)).astype(BF16)
        m_new = m_t[L - 1:L, :]
        b_last = b_col[L - 1:L, :]
        decay = jnp.exp(b_last + m - m_new)
        w_s = jnp.exp(b_last - b_col + ig_col - m_new)
        c_scr[hh] = decay * Ct + _dot_tn(kb, (v.astype(F32) * w_s).astype(BF16))
        n_scr[hh] = decay * n_row + jnp.sum(k * w_s, axis=0, keepdims=True)
        m_scr[hh] = m_new

    def chunk(ci, carry):
        t0 = pl.multiple_of(ci * L, L)
        gc = gcol_ref[0, pl.ds(t0, L), :]
        gr = grow_ref[:, pl.ds(t0, L)]
        b_cols = jnp.dot(tril, jax.nn.log_sigmoid(gc), preferred_element_type=F32,
                         precision=lax.Precision.HIGHEST)
        b_rows = jnp.dot(jax.nn.log_sigmoid(gr), triu, preferred_element_type=F32,
                         precision=lax.Precision.HIGHEST)
        for hh in range(H):
            head_chunk(hh, t0, gc, gr, b_cols, b_rows)
        return carry

    lax.fori_loop(0, S // L, chunk, 0)


def _mlstm(z4, gcol3, grow, conv_w, conv_b, norm_g, B, S):
    kern = functools.partial(_mlstm_kernel, S=S)
    H = ML_HEADS
    D = H * ML_DV
    piece = lambda n: pl.BlockSpec((1, 1, S, D), lambda b: (n, b, 0, 0))
    return pl.pallas_call(
        kern,
        grid=(B,),
        in_specs=[piece(3), piece(4), piece(5),
                  pl.BlockSpec((1, S, 128), lambda b: (b, 0, 0)),
                  pl.BlockSpec((8, S), lambda b: (0, b)),
                  pl.BlockSpec((CONV_K, 2 * H * ML_DK), lambda b: (0, 0)),
                  pl.BlockSpec((1, 2 * H * ML_DK), lambda b: (0, 0)),
                  pl.BlockSpec((1, D), lambda b: (0, 0))],
        out_specs=pl.BlockSpec((1, S, D), lambda b: (b, 0, 0)),
        out_shape=jax.ShapeDtypeStruct((B, S, D), BF16),
        scratch_shapes=[pltpu.VMEM((S, H * ML_DK), F32), pltpu.VMEM((S, H * ML_DK), F32),
                        pltpu.VMEM((H, ML_DK, ML_DV), F32), pltpu.VMEM((H, 1, ML_DK), F32),
                        pltpu.VMEM((H, 1, 1), F32)],
        compiler_params=_cparams(("parallel",)),
        name="mlstm",
    )(z4, z4, z4, gcol3, grow, conv_w, conv_b, norm_g)


MERGE_ROWS = 256


def _merge_kernel(ya_ref, ym_ref, ga_ref, gm_ref, x_ref, mod_ref, wa_ref, wm_ref, wo_ref, g1_ref, b1_ref,
                  x1_ref, h2_ref):
    gt1 = mod_ref[0, 2:3, :]
    sh2 = mod_ref[0, 3:4, :]
    sc2 = mod_ref[0, 4:5, :]
    for c in range(ya_ref.shape[0] // MERGE_ROWS):
        rows = slice(c * MERGE_ROWS, (c + 1) * MERGE_ROWS)
        ya = _dot(ya_ref[rows, :], wa_ref[...])
        ym = _dot(ym_ref[rows, :], wm_ref[...])
        y = (jax.nn.sigmoid(ga_ref[0, rows, :].astype(F32)) * ya
             + jax.nn.sigmoid(gm_ref[0, rows, :].astype(F32)) * ym)
        y2 = _dot(y.astype(BF16), wo_ref[...])
        x1 = _layer_norm_rows(ALPHA * x_ref[rows, :] + gt1 * y2) * g1_ref[...] + b1_ref[...]
        x1_ref[rows, :] = x1
        h2_ref[:, rows] = (_layer_norm_rows(x1) * (1.0 + sc2) + sh2).T.astype(BF16)


def _merge(ya2, ym2, z3, x2, mod3, wa, wm, wo, ln_g, ln_b, S):
    T, D = x2.shape
    tm = 512
    tok = lambda i: (i, 0)
    const = lambda i: (0, 0)
    return pl.pallas_call(
        _merge_kernel,
        grid=(T // tm,),
        in_specs=[pl.BlockSpec((tm, D), tok), pl.BlockSpec((tm, D), tok),
                  pl.BlockSpec((1, tm, D), lambda i: (6, i, 0)),
                  pl.BlockSpec((1, tm, D), lambda i: (7, i, 0)),
                  pl.BlockSpec((tm, D), tok),
                  pl.BlockSpec((1, 6, D), lambda i: ((i * tm) // S, 0, 0)),
                  pl.BlockSpec((D, D), const), pl.BlockSpec((D, D), const), pl.BlockSpec((D, D), const),
                  pl.BlockSpec((1, D), const), pl.BlockSpec((1, D), const)],
        out_specs=[pl.BlockSpec((tm, D), tok), pl.BlockSpec((D, tm), lambda i: (0, i))],
        out_shape=[jax.ShapeDtypeStruct((T, D), F32), jax.ShapeDtypeStruct((D, T), BF16)],
        compiler_params=_cparams(("parallel",)),
        name="merge",
    )(ya2, ym2, z3, z3, x2, mod3, wa, wm, wo, ln_g, ln_b)


PEER_TB = 256
_CAND_COLS = [[k1 for k1 in range(PEER_TOPK) if (k1 + 1) * (k2 + 1) <= PEER_TOPK] for k2 in range(PEER_TOPK)]


def _batcher_pairs(n):
    pairs = []
    p = 1
    while p < n:
        k = p
        while k >= 1:
            for j in range(k % p, n - k, 2 * k):
                for i in range(min(k, n - j - k)):
                    if (i + j) // (2 * p) == (i + j + k) // (2 * p):
                        pairs.append((i + j, i + j + k))
            k //= 2
        p *= 2
    return pairs


_SORT16 = _batcher_pairs(PEER_TOPK)


def _cmpx(v, i, j):
    a, b = v[i], v[j]
    if b is None:
        return
    if a is None:
        v[i], v[j] = b, None
        return
    v[i], v[j] = jnp.maximum(a, b), jnp.minimum(a, b)


def _sort16_desc(v):
    v = list(v)
    for i, j in _SORT16:
        _cmpx(v, i, j)
    return v


def _bitonic_merge_desc(v):
    v = list(v)
    d = PEER_TOPK // 2
    while d >= 1:
        for i in range(PEER_TOPK):
            if i & d == 0:
                _cmpx(v, i, i + d)
        d //= 2
    return v


def _half_clean(a, b):
    out = []
    for g in range(PEER_TOPK):
        x, y = a[g], b[PEER_TOPK - 1 - g]
        out.append(x if y is None else (y if x is None else jnp.maximum(x, y)))
    return out


def _top16_all_sublanes(scores):
    rows = [_sort16_desc([s[g * 8:(g + 1) * 8, :] for g in range(PEER_TOPK)]) for s in scores]
    for shift in (4, 2, 1):
        partner = [[pltpu.roll(r, shift, 0) for r in rr] for rr in rows]
        rows = [_bitonic_merge_desc(_half_clean(rr, pp)) for rr, pp in zip(rows, partner)]
    return rows


def _route_kernel(h_ref, wqt_ref, keys_ref, rk2_ref, e2_ref, c1_ref, e1_ref, q_scr, s_scr, a_scr, f_scr):
    half = PEER_DKEY // 2
    K = PEER_TOPK
    q_scr[...] = _dot(wqt_ref[...], h_ref[...]).astype(BF16)

    def score_matmuls(h_src, slot):
        base = pl.multiple_of(h_src * PEER_DKEY, PEER_DKEY)
        for p in range(2):
            s_scr[p, slot] = _dot(keys_ref[p], q_scr[pl.ds(base + p * half, half), :])

    score_matmuls(0, 0)

    def stage_a(h, carry):
        score_matmuls(jnp.minimum(h + 1, PEER_HEADS - 1), h + 1)
        tops = _top16_all_sublanes([s_scr[p, h] for p in range(2)])
        for p in range(2):
            for k in range(K):
                a_scr[p, k, pl.ds(h, 1), :] = tops[p][k][0:1, :]
        return carry

    lax.fori_loop(0, PEER_HEADS, stage_a, 0)

    a1 = [a_scr[0, k] for k in range(K)]
    a2 = [a_scr[1, k] for k in range(K)]
    cand = [[a1[k1] + a2[k2] for k1 in col] for k2, col in enumerate(_CAND_COLS)]
    g0 = [cand[k2][0] for k2 in range(K)]
    rest = [cand[k2][i] for k2 in range(K) for i in range(1, len(_CAND_COLS[k2]))]
    rest += [None] * (-len(rest) % K)
    groups = [g0] + [_sort16_desc(rest[i:i + K]) for i in range(0, len(rest), K)]
    while len(groups) > 2:
        merged = [_bitonic_merge_desc(_half_clean(groups[i], groups[i + 1])) for i in range(0, len(groups) - 1, 2)]
        groups = merged + ([groups[-1]] if len(groups) % 2 else [])
    last = [x for x in _half_clean(groups[0], groups[1]) if x is not None]
    tau = functools.reduce(jnp.minimum, last)
    cmax = a1[0] + a2[0]
    zsum = None
    for k2, col in enumerate(_CAND_COLS):
        phi = None
        for i, k1 in enumerate(col):
            c = cand[k2][i]
            hit = c >= tau
            term = jnp.where(hit, jnp.exp(c - cmax), 0.0)
            zsum = term if zsum is None else zsum + term
            lo = jnp.where(hit, a1[k1], jnp.inf)
            phi = lo if phi is None else jnp.minimum(phi, lo)
        f_scr[k2] = phi
    f_scr[K] = 1.0 / zsum

    def prefix_count(pred, thr):
        b8 = pred(thr[7])
        b4 = pred(jnp.where(b8, thr[11], thr[3]))
        b2 = pred(jnp.where(b8, jnp.where(b4, thr[13], thr[9]), jnp.where(b4, thr[5], thr[1])))
        b1 = pred(jnp.where(b8,
                            jnp.where(b4, jnp.where(b2, thr[14], thr[12]), jnp.where(b2, thr[10], thr[8])),
                            jnp.where(b4, jnp.where(b2, thr[6], thr[4]), jnp.where(b2, thr[2], thr[0]))))
        b0 = pred(thr[15])
        bit = lambda b, v: jnp.where(b, v, 0.0)
        return bit(b8, 8.0) + bit(b4, 4.0) + bit(b2, 2.0) + bit(b1, 1.0) + bit(b0, 1.0)

    def stage_c(h, carry):
        s1 = s_scr[0, h]
        s2 = s_scr[1, h]
        shape = s2.shape
        top2 = [jnp.broadcast_to(a_scr[1, k, pl.ds(h, 1), :], shape) for k in range(K)]
        phi = [jnp.broadcast_to(f_scr[k, pl.ds(h, 1), :], shape) for k in range(K)]
        rk2_ref[h] = prefix_count(lambda t: t > s2, top2).astype(BF16)
        e2_ref[h] = jnp.exp(s2 - a_scr[1, 0, pl.ds(h, 1), :]).astype(BF16)
        c1_ref[h] = prefix_count(lambda t: s1 >= t, phi)
        e1_ref[h] = jnp.exp(s1 - a_scr[0, 0, pl.ds(h, 1), :]) * f_scr[K, pl.ds(h, 1), :]
        return carry

    lax.fori_loop(0, PEER_HEADS, stage_c, 0)


def _peer_route(h2t, wqt, keys):
    D, T = h2t.shape
    tb = PEER_TB
    PH = PEER_HEADS
    blk = pl.BlockSpec((PH, N_KEYS, tb), lambda i: (0, 0, i))
    return pl.pallas_call(
        _route_kernel,
        grid=(T // tb,),
        in_specs=[pl.BlockSpec((D, tb), lambda i: (0, i)),
                  pl.BlockSpec((PH * PEER_DKEY, D), lambda i: (0, 0)),
                  pl.BlockSpec((2, N_KEYS, PEER_DKEY // 2), lambda i: (0, 0, 0))],
        out_specs=[blk, blk, blk, blk],
        out_shape=[jax.ShapeDtypeStruct((PH, N_KEYS, T), BF16),
                   jax.ShapeDtypeStruct((PH, N_KEYS, T), BF16),
                   jax.ShapeDtypeStruct((PH, N_KEYS, T), F32),
                   jax.ShapeDtypeStruct((PH, N_KEYS, T), F32)],
        scratch_shapes=[pltpu.VMEM((PH * PEER_DKEY, tb), BF16),
                        pltpu.VMEM((2, PH + 1, N_KEYS, tb), F32),
                        pltpu.VMEM((2, PEER_TOPK, PH, tb), F32),
                        pltpu.VMEM((PEER_TOPK + 1, PH, tb), F32)],
        compiler_params=_cparams(("parallel",)),
        name="peer_route",
    )(h2t, wqt, keys)


EXP_TB = 512
EXP_EB = 1024
BF16_ROWS = 16
LANES = 256


def _expert_step_order(nsub, ngroup):
    order = []
    per = nsub // ngroup
    for g in range(ngroup):
        order.append(("v", g))
        for j in range(g * per, (g + 1) * per):
            order += [("k", j), ("g", j)]
    return tuple(order)


def _experts_kernel(h_ref, u_ref, vt_ref, rk2_ref, e2_ref, c1_ref, e1_ref, x1_ref, mod_ref, g2_ref, b2_ref,
                    o_ref, acc_scr, at_scr, w0_scr, w1_scr):
    e = pl.program_id(1)
    ne = N_EXPERTS // EXP_EB
    nsub = EXP_EB // N_KEYS
    ngrp = N_KEYS // BF16_ROWS
    tb = h_ref.shape[1]
    nhalf = tb // LANES
    assert nsub % nhalf == 0

    def value_matmul(w_read, half):
        cols = slice(half * LANES, (half + 1) * LANES)
        acc_scr[:, cols] += _dot(vt_ref[...], w_read[:, cols])

    def key_matmul(j):
        krows = slice(j * N_KEYS, (j + 1) * N_KEYS)
        at_scr[krows, :] = _dot(u_ref[krows, :], h_ref[...])

    def gated_activation(j, w_write):
        for lt in range(nhalf):
            cols = slice(lt * LANES, (lt + 1) * LANES)
            gates = [None] * ngrp
            for h in range(PEER_HEADS):
                cnt = jnp.broadcast_to(c1_ref[h, j:j + 1, cols], (BF16_ROWS, LANES)).astype(BF16)
                e1 = jnp.broadcast_to(e1_ref[h, j:j + 1, cols], (BF16_ROWS, LANES)).astype(BF16)
                for r in range(ngrp):
                    rows = slice(r * BF16_ROWS, (r + 1) * BF16_ROWS)
                    term = jnp.where(rk2_ref[h, rows, cols] < cnt, e2_ref[h, rows, cols],
                                     jnp.zeros((), BF16)) * e1
                    gates[r] = term if gates[r] is None else gates[r] + term
            for r in range(ngrp):
                rows = slice(j * N_KEYS + r * BF16_ROWS, j * N_KEYS + (r + 1) * BF16_ROWS)
                a = at_scr[rows, cols].astype(BF16)
                act = (0.5 * a) * (1.0 + lax.erf(a * (2.0 ** -0.5)))
                w_write[rows, cols] = gates[r] * act

    def step(w_write, w_read):
        for kind, idx in _expert_step_order(nsub, nhalf):
            if kind == "v":
                value_matmul(w_read, idx)
            elif kind == "k":
                key_matmul(idx)
            else:
                gated_activation(idx, w_write)

    parity = lax.rem(e, 2)

    @pl.when(e == 0)
    def _():
        acc_scr[...] = jnp.zeros_like(acc_scr)
        w1_scr[...] = jnp.zeros_like(w1_scr)

    @pl.when((e < ne) & (parity == 0))
    def _():
        step(w0_scr, w1_scr)

    @pl.when((e < ne) & (parity == 1))
    def _():
        step(w1_scr, w0_scr)

    @pl.when(e == ne)
    def _():
        w_last = w1_scr if ne % 2 == 0 else w0_scr
        for half in range(nhalf):
            value_matmul(w_last, half)
        yf = acc_scr[...].T
        gt2 = mod_ref[0, 5:6, :]
        r = ALPHA * x1_ref[...] + gt2 * yf
        o_ref[...] = _layer_norm_rows(r) * g2_ref[...] + b2_ref[...]


def _peer_experts(h2t, u_b, vt_b, rk2, e2, c1, e1, x1, mod3, ln_g, ln_b, S):
    D, T = h2t.shape
    tb, eb = EXP_TB, EXP_EB
    PH = PEER_HEADS
    route = pl.BlockSpec((PH, N_KEYS, tb), lambda i, e: (0, 0, i))
    ne = N_EXPERTS // eb
    key1 = pl.BlockSpec((PH, eb // N_KEYS, tb), lambda i, e: (0, jnp.minimum(e, ne - 1), i))
    return pl.pallas_call(
        _experts_kernel,
        grid=(T // tb, ne + 1),
        in_specs=[pl.BlockSpec((D, tb), lambda i, e: (0, i)),
                  pl.BlockSpec((eb, D), lambda i, e: (jnp.minimum(e, ne - 1), 0)),
                  pl.BlockSpec((D, eb), lambda i, e: (0, jnp.maximum(e - 1, 0))),
                  route, route, key1, key1,
                  pl.BlockSpec((tb, D), lambda i, e: (i, 0)),
                  pl.BlockSpec((1, 6, D), lambda i, e: ((i * tb) // S, 0, 0)),
                  pl.BlockSpec((1, D), lambda i, e: (0, 0)),
                  pl.BlockSpec((1, D), lambda i, e: (0, 0))],
        out_specs=pl.BlockSpec((tb, D), lambda i, e: (i, 0)),
        out_shape=jax.ShapeDtypeStruct((T, D), F32),
        scratch_shapes=[pltpu.VMEM((D, tb), F32), pltpu.VMEM((eb, tb), F32),
                        pltpu.VMEM((eb, tb), BF16), pltpu.VMEM((eb, tb), BF16)],
        compiler_params=_cparams(("parallel", "arbitrary")),
        name="peer_experts",
    )(h2t, u_b, vt_b, rk2, e2, c1, e1, x1, mod3, ln_g, ln_b)


def kernel(x, c, w_ada, b_ada, w_in, b_if, conv_w, conv_b, da_lambda, da_subln_g, ml_norm_g, w_br_attn,
           w_br_mlstm, w_out, ln1_g, ln1_b, peer_wq, peer_keys, peer_u, peer_v, ln2_g, ln2_b):
    B, S, D = x.shape
    T = B * S
    assert D == D_MODEL and S % DA_TQ == 0 and S % 1024 == 0
    l = 0
    lambda_init = 0.8 - 0.6 * math.exp(-0.3 * l)

    mod3 = _modulation(c, w_ada[l], b_ada[l]).reshape(B, 6, D)

    w = w_in[l]
    o_mq = 3 * D
    o_mv = o_mq + 2 * ML_HEADS * ML_DK
    o_mo = o_mv + D
    o_if = o_mo + D
    o_ga = o_if + 2 * ML_HEADS
    o_gm = o_ga + D
    starts = (0, D, 2 * D, o_mq, o_mv, o_mo, o_ga, o_gm)
    q_fold = (DA_DK ** -0.5) * math.log2(math.e)
    pieces = [w[:, s0:s0 + D] for s0 in starts]
    pieces[0] = pieces[0] * q_fold
    w8 = jnp.stack(pieces).astype(BF16)
    w_if = w[:, o_if:o_if + 2 * ML_HEADS]
    wg = jnp.pad(w_if, ((0, 0), (0, 128 - 2 * ML_HEADS))).astype(BF16)
    wgt = w_if.T.astype(BF16)
    bias8 = b_if[l].reshape(2 * ML_HEADS)
    bcol = jnp.pad(bias8, (0, 128 - 2 * ML_HEADS)).reshape(1, 128)
    brow = bias8.reshape(2 * ML_HEADS, 1)

    x2 = x.reshape(T, D)
    z, gcol, grow = _in_proj(x2, mod3, w8, wg, wgt, bcol, brow, S)
    z4 = z.reshape(8, B, S, D)

    ya = _diff_attention(z4, da_lambda[l], da_subln_g[l], B, S, lambda_init)
    ym = _mlstm(z4, gcol.reshape(B, S, 128), grow, conv_w[l], conv_b[l].reshape(1, -1),
                ml_norm_g[l].reshape(1, -1), B, S)

    x1, h2 = _merge(ya.reshape(T, D), ym.reshape(T, D), z, x2, mod3,
                    w_br_attn[l].astype(BF16), w_br_mlstm[l].astype(BF16), w_out[l].astype(BF16),
                    ln1_g[l].reshape(1, D), ln1_b[l].reshape(1, D), S)

    rk2, e2, c1, e1 = _peer_route(h2, peer_wq[l].T.astype(BF16), peer_keys[l].astype(BF16))
    out = _peer_experts(h2, peer_u[l].astype(BF16), peer_v[l].T.astype(BF16), rk2, e2, c1, e1, x1, mod3,
                        ln2_g[l].reshape(1, D), ln2_b[l].reshape(1, D), S)
    return out.reshape(B, S, D)
```

```python
import functools
import math

import jax
import jax.numpy as jnp
from jax import lax
from jax.experimental import pallas as pl
from jax.experimental.pallas import tpu as pltpu

D_MODEL = 1024
DA_HEADS = 8
DA_DK = 64
DA_DV = 2 * DA_DK
ML_HEADS = 4
ML_DK = 128
ML_DV = 256
ML_CHUNK = 128
CONV_K = 4
PEER_HEADS = 8
PEER_TOPK = 16
N_KEYS = 128
N_EXPERTS = N_KEYS * N_KEYS
PEER_DKEY = 128
DEPTH = 1
ALPHA = (2 * DEPTH) ** 0.25
LN_EPS = 1e-5

F32 = jnp.float32
BF16 = jnp.bfloat16
NEG_INF = float("-inf")

VMEM_LIMIT_BYTES = 56 * 1024 * 1024


def _cparams(sem):
    return pltpu.CompilerParams(dimension_semantics=sem, vmem_limit_bytes=VMEM_LIMIT_BYTES)


def _layer_norm_rows(x):
    mu = jnp.mean(x, axis=-1, keepdims=True)
    xc = x - mu
    var = jnp.mean(xc * xc, axis=-1, keepdims=True)
    return xc * lax.rsqrt(var + LN_EPS)


def _dot(a, b):
    return jnp.dot(a, b, preferred_element_type=F32)


def _dot_nt(a, b):
    return lax.dot_general(a, b, (((1,), (1,)), ((), ())), preferred_element_type=F32)


def _dot_tn(a, b):
    return lax.dot_general(a, b, (((0,), (0,)), ((), ())), preferred_element_type=F32)


def _mod_kernel(c_ref, w_ref, b_ref, o_ref):
    c = c_ref[...]
    a = c * jax.nn.sigmoid(c)
    o_ref[...] = jnp.dot(a, w_ref[...], preferred_element_type=F32,
                         precision=lax.Precision.HIGHEST) + b_ref[...]


def _modulation(c, w_ada, b_ada):
    B, D = c.shape
    N = w_ada.shape[1]
    tn = 1024
    return pl.pallas_call(
        _mod_kernel,
        grid=(N // tn,),
        in_specs=[pl.BlockSpec((B, D), lambda n: (0, 0)),
                  pl.BlockSpec((D, tn), lambda n: (0, n)),
                  pl.BlockSpec((1, tn), lambda n: (0, n))],
        out_specs=pl.BlockSpec((B, tn), lambda n: (0, n)),
        out_shape=jax.ShapeDtypeStruct((B, N), F32),
        compiler_params=_cparams(("arbitrary",)),
        name="modulation",
    )(c, w_ada, b_ada.reshape(1, N))


INPROJ_ROWS = 256


def _inproj_kernel(x_ref, mod_ref, w_ref, wg_ref, wgt_ref, bcol_ref, brow_ref,
                   z_ref, gcol_ref, grow_ref, h_scr):
    n = pl.program_id(1)

    @pl.when(n == 0)
    def _():
        sh1 = mod_ref[0, 0:1, :]
        sc1 = mod_ref[0, 1:2, :]
        tm = x_ref.shape[0]
        for c in range(tm // INPROJ_ROWS):
            rows = slice(c * INPROJ_ROWS, (c + 1) * INPROJ_ROWS)
            hb = (_layer_norm_rows(x_ref[rows, :]) * (1.0 + sc1) + sh1).astype(BF16)
            h_scr[rows, :] = hb
            z_ref[0, rows, :] = _dot(hb, w_ref[0]).astype(BF16)
            gcol_ref[rows, :] = _dot(hb, wg_ref[...]) + bcol_ref[...]
            grow_ref[:, rows] = _dot_nt(wgt_ref[...], hb) + brow_ref[...]

    @pl.when(n > 0)
    def _():
        z_ref[0] = _dot(h_scr[...], w_ref[0]).astype(BF16)


def _in_proj(x2, mod3, w8, wg, wgt, bcol, brow, S):
    T, D = x2.shape
    tm = 1024
    npiece = w8.shape[0]
    return pl.pallas_call(
        _inproj_kernel,
        grid=(T // tm, npiece),
        in_specs=[pl.BlockSpec((tm, D), lambda i, n: (i, 0)),
                  pl.BlockSpec((1, 6, D), lambda i, n: ((i * tm) // S, 0, 0)),
                  pl.BlockSpec((1, D, D), lambda i, n: (n, 0, 0)),
                  pl.BlockSpec((D, 128), lambda i, n: (0, 0)),
                  pl.BlockSpec((8, D), lambda i, n: (0, 0)),
                  pl.BlockSpec((1, 128), lambda i, n: (0, 0)),
                  pl.BlockSpec((8, 1), lambda i, n: (0, 0))],
        out_specs=[pl.BlockSpec((1, tm, D), lambda i, n: (n, i, 0)),
                   pl.BlockSpec((tm, 128), lambda i, n: (i, 0)),
                   pl.BlockSpec((8, tm), lambda i, n: (0, i))],
        out_shape=[jax.ShapeDtypeStruct((npiece, T, D), BF16),
                   jax.ShapeDtypeStruct((T, 128), F32),
                   jax.ShapeDtypeStruct((8, T), F32)],
        scratch_shapes=[pltpu.VMEM((tm, D), BF16)],
        compiler_params=_cparams(("parallel", "arbitrary")),
        name="in_proj",
    )(x2, mod3, w8, wg, wgt, bcol, brow)


DA_TQ = 256


def _diffattn_kernel(lam_ref, g_ref, q_ref, k_ref, v_ref, o_ref, s_scr, *, S, lambda_init):
    tq = DA_TQ
    nq = S // tq
    lam = lam_ref[...]
    t1 = jnp.sum(lam[0:1] * lam[1:2], axis=-1, keepdims=True)
    t2 = jnp.sum(lam[2:3] * lam[3:4], axis=-1, keepdims=True)
    lam_val = jnp.exp(t1) - jnp.exp(t2) + lambda_init
    first_map = lax.broadcasted_iota(jnp.int32, (1, DA_DV), 1) < DA_DK
    gain = g_ref[...] * (1.0 - lambda_init)
    row = lax.broadcasted_iota(jnp.int32, (tq, tq), 0)
    col = lax.broadcasted_iota(jnp.int32, (tq, tq), 1)
    causal = col <= row

    def scores(qi):
        q0 = qi * tq
        qs = q_ref[0, 0, q0:q0 + tq, :]
        zero = jnp.zeros_like(qs)
        for mp, qm in enumerate((jnp.where(first_map, qs, zero), jnp.where(first_map, zero, qs))):
            s_scr[qi % 2, mp, :, 0:q0 + tq] = _dot_nt(qm, k_ref[0, 0, 0:q0 + tq, :])

    def finish(qi):
        q0 = qi * tq
        slot = qi % 2
        probs = []
        for mp in range(2):
            s_diag = jnp.where(causal, s_scr[slot, mp, :, q0:q0 + tq], NEG_INF)
            m = jnp.max(s_diag, axis=-1, keepdims=True)
            if qi > 0:
                s_off = s_scr[slot, mp, :, 0:q0]
                m = jnp.maximum(m, jnp.max(s_off, axis=-1, keepdims=True))
            p_diag = jnp.exp2(s_diag - m)
            l = jnp.sum(p_diag, axis=-1, keepdims=True)
            p_off = None
            if qi > 0:
                p_off = jnp.exp2(s_off - m)
                l = l + jnp.sum(p_off, axis=-1, keepdims=True)
            probs.append((p_diag, p_off, l))
        (p1d, p1o, l1), (p2d, p2o, l2) = probs
        ratio = lam_val * l1 / l2
        o = _dot((p1d - p2d * ratio).astype(BF16), v_ref[0, 0, q0:q0 + tq, :])
        if qi > 0:
            o = o + _dot((p1o - p2o * ratio).astype(BF16), v_ref[0, 0, 0:q0, :])
        o = o / l1
        o = o * lax.rsqrt(jnp.mean(o * o, axis=-1, keepdims=True) + LN_EPS) * gain
        o_ref[0, q0:q0 + tq, :] = o.astype(BF16)

    scores(0)
    for qi in range(nq):
        if qi + 1 < nq:
            scores(qi + 1)
        finish(qi)


def _diff_attention(z4, da_lambda, subln_g, B, S, lambda_init):
    kern = functools.partial(_diffattn_kernel, S=S, lambda_init=lambda_init)
    return pl.pallas_call(
        kern,
        grid=(B, DA_HEADS),
        in_specs=[pl.BlockSpec((4, DA_DK), lambda b, h: (0, 0)),
                  pl.BlockSpec((1, DA_DV), lambda b, h: (0, 0)),
                  pl.BlockSpec((1, 1, S, DA_DV), lambda b, h: (0, b, 0, h)),
                  pl.BlockSpec((1, 1, S, DA_DV), lambda b, h: (1, b, 0, h)),
                  pl.BlockSpec((1, 1, S, DA_DV), lambda b, h: (2, b, 0, h))],
        out_specs=pl.BlockSpec((1, S, DA_DV), lambda b, h: (b, 0, h)),
        out_shape=jax.ShapeDtypeStruct((B, S, DA_HEADS * DA_DV), BF16),
        scratch_shapes=[pltpu.VMEM((2, 2, DA_TQ, S), F32)],
        compiler_params=_cparams(("parallel", "parallel")),
        name="diff_attention",
    )(da_lambda, subln_g.reshape(1, DA_DV), z4, z4, z4)


def _mlstm_kernel(qk_ref, v_ref, og_ref, gcol_ref, grow_ref, cw_ref, cb_ref, ng_ref, o_ref,
                  qc_scr, kc_scr, c_scr, n_scr, m_scr, *, S):
    L = ML_CHUNK
    H = ML_HEADS
    srow = lax.broadcasted_iota(jnp.int32, (S, ML_DK), 0)

    for cb in range(2 * H):
        cols = slice(cb * ML_DK, (cb + 1) * ML_DK)
        x = qk_ref[0, 0, :, cols].astype(F32)
        y = x * cw_ref[CONV_K - 1:CONV_K, cols] + cb_ref[:, cols]
        for j in range(1, CONV_K):
            xs = jnp.where(srow >= j, pltpu.roll(x, j, 0), 0.0)
            y = y + xs * cw_ref[CONV_K - 1 - j:CONV_K - j, cols]
        y = y * jax.nn.sigmoid(y)
        if cb < H:
            qc_scr[:, cols] = y
        else:
            kc_scr[:, (cb - H) * ML_DK:(cb - H + 1) * ML_DK] = y * (ML_DK ** -0.5)

    r_i = lax.broadcasted_iota(jnp.int32, (L, L), 0)
    c_i = lax.broadcasted_iota(jnp.int32, (L, L), 1)
    causal = c_i <= r_i
    tril = causal.astype(F32)
    triu = (r_i <= c_i).astype(F32)
    c_scr[...] = jnp.zeros_like(c_scr)
    n_scr[...] = jnp.zeros_like(n_scr)
    m_scr[...] = jnp.zeros_like(m_scr)

    def head_chunk(hh, t0, gc, gr, b_cols, b_rows):
        Ct = c_scr[hh]
        n_row = n_scr[hh]
        m = m_scr[hh]
        q = qc_scr[pl.ds(t0, L), hh * ML_DK:(hh + 1) * ML_DK]
        k = kc_scr[pl.ds(t0, L), hh * ML_DK:(hh + 1) * ML_DK]
        v = v_ref[0, 0, pl.ds(t0, L), hh * ML_DV:(hh + 1) * ML_DV]
        ngain = ng_ref[:, hh * ML_DV:(hh + 1) * ML_DV]
        ig_col = gc[:, hh:hh + 1]
        ig_row = gr[hh:hh + 1, :]
        b_col = b_cols[:, H + hh:H + hh + 1]
        b_row = b_rows[H + hh:H + hh + 1, :]
        dm = jnp.where(causal, b_col - b_row + ig_row, NEG_INF)
        m_inter = b_col + m
        m_t = jnp.maximum(m_inter, jnp.max(dm, axis=-1, keepdims=True))
        w = jnp.exp(dm - m_t)
        qb = q.astype(BF16)
        kb = k.astype(BF16)
        p = w * _dot_nt(qb, kb)
        inter = jnp.exp(m_inter - m_t)
        num = _dot(p.astype(BF16), v) + inter * _dot(qb, Ct.astype(BF16))
        nq = jnp.sum(p, axis=-1, keepdims=True) + inter * jnp.sum(q * n_row, axis=-1, keepdims=True)
        hout = num / jnp.maximum(jnp.abs(nq), jnp.exp(-m_t))
        hout = hout * lax.rsqrt(jnp.mean(hout * hout, axis=-1, keepdims=True) + LN_EPS) * ngain
        og = og_ref[0, 0, pl.ds(t0, L), hh * ML_DV:(hh + 1) * ML_DV].astype(F32)
        o_ref[0, pl.ds(t0, L), hh * ML_DV:(hh + 1) * ML_DV] = (hout * jax.nn.sigmoid(og)).astype(BF16)
        m_new = m_t[L - 1:L, :]
        b_last = b_col[L - 1:L, :]
        decay = jnp.exp(b_last + m - m_new)
        w_s = jnp.exp(b_last - b_col + ig_col - m_new)
        c_scr[hh] = decay * Ct + _dot_tn(kb, (v.astype(F32) * w_s).astype(BF16))
        n_scr[hh] = decay * n_row + jnp.sum(k * w_s, axis=0, keepdims=True)
        m_scr[hh] = m_new

    def chunk(ci, carry):
        t0 = pl.multiple_of(ci * L, L)
        gc = gcol_ref[0, pl.ds(t0, L), :]
        gr = grow_ref[:, pl.ds(t0, L)]
        b_cols = jnp.dot(tril, jax.nn.log_sigmoid(gc), preferred_element_type=F32,
                         precision=lax.Precision.HIGHEST)
        b_rows = jnp.dot(jax.nn.log_sigmoid(gr), triu, preferred_element_type=F32,
                         precision=lax.Precision.HIGHEST)
        for hh in range(H):
            head_chunk(hh, t0, gc, gr, b_cols, b_rows)
        return carry

    lax.fori_loop(0, S // L, chunk, 0)


def _mlstm(z4, gcol3, grow, conv_w, conv_b, norm_g, B, S):
    kern = functools.partial(_mlstm_kernel, S=S)
    H = ML_HEADS
    D = H * ML_DV
    piece = lambda n: pl.BlockSpec((1, 1, S, D), lambda b: (n, b, 0, 0))
    return pl.pallas_call(
        kern,
        grid=(B,),
        in_specs=[piece(3), piece(4), piece(5),
                  pl.BlockSpec((1, S, 128), lambda b: (b, 0, 0)),
                  pl.BlockSpec((8, S), lambda b: (0, b)),
                  pl.BlockSpec((CONV_K, 2 * H * ML_DK), lambda b: (0, 0)),
                  pl.BlockSpec((1, 2 * H * ML_DK), lambda b: (0, 0)),
                  pl.BlockSpec((1, D), lambda b: (0, 0))],
        out_specs=pl.BlockSpec((1, S, D), lambda b: (b, 0, 0)),
        out_shape=jax.ShapeDtypeStruct((B, S, D), BF16),
        scratch_shapes=[pltpu.VMEM((S, H * ML_DK), F32), pltpu.VMEM((S, H * ML_DK), F32),
                        pltpu.VMEM((H, ML_DK, ML_DV), F32), pltpu.VMEM((H, 1, ML_DK), F32),
                        pltpu.VMEM((H, 1, 1), F32)],
        compiler_params=_cparams(("parallel",)),
        name="mlstm",
    )(z4, z4, z4, gcol3, grow, conv_w, conv_b, norm_g)


MERGE_ROWS = 256


def _merge_kernel(ya_ref, ym_ref, ga_ref, gm_ref, x_ref, mod_ref, wa_ref, wm_ref, wo_ref, g1_ref, b1_ref,
                  x1_ref, h2_ref):
    gt1 = mod_ref[0, 2:3, :]
    sh2 = mod_ref[0, 3:4, :]
    sc2 = mod_ref[0, 4:5, :]
    for c in range(ya_ref.shape[0] // MERGE_ROWS):
        rows = slice(c * MERGE_ROWS, (c + 1) * MERGE_ROWS)
        ya = _dot(ya_ref[rows, :], wa_ref[...])
        ym = _dot(ym_ref[rows, :], wm_ref[...])
        y = (jax.nn.sigmoid(ga_ref[0, rows, :].astype(F32)) * ya
             + jax.nn.sigmoid(gm_ref[0, rows, :].astype(F32)) * ym)
        y2 = _dot(y.astype(BF16), wo_ref[...])
        x1 = _layer_norm_rows(ALPHA * x_ref[rows, :] + gt1 * y2) * g1_ref[...] + b1_ref[...]
        x1_ref[rows, :] = x1
        h2_ref[:, rows] = (_layer_norm_rows(x1) * (1.0 + sc2) + sh2).T.astype(BF16)


def _merge(ya2, ym2, z3, x2, mod3, wa, wm, wo, ln_g, ln_b, S):
    T, D = x2.shape
    tm = 512
    tok = lambda i: (i, 0)
    const = lambda i: (0, 0)
    return pl.pallas_call(
        _merge_kernel,
        grid=(T // tm,),
        in_specs=[pl.BlockSpec((tm, D), tok), pl.BlockSpec((tm, D), tok),
                  pl.BlockSpec((1, tm, D), lambda i: (6, i, 0)),
                  pl.BlockSpec((1, tm, D), lambda i: (7, i, 0)),
                  pl.BlockSpec((tm, D), tok),
                  pl.BlockSpec((1, 6, D), lambda i: ((i * tm) // S, 0, 0)),
                  pl.BlockSpec((D, D), const), pl.BlockSpec((D, D), const), pl.BlockSpec((D, D), const),
                  pl.BlockSpec((1, D), const), pl.BlockSpec((1, D), const)],
        out_specs=[pl.BlockSpec((tm, D), tok), pl.BlockSpec((D, tm), lambda i: (0, i))],
        out_shape=[jax.ShapeDtypeStruct((T, D), F32), jax.ShapeDtypeStruct((D, T), BF16)],
        compiler_params=_cparams(("parallel",)),
        name="merge",
    )(ya2, ym2, z3, z3, x2, mod3, wa, wm, wo, ln_g, ln_b)


PEER_TB = 256
_CAND_COLS = [[k1 for k1 in range(PEER_TOPK) if (k1 + 1) * (k2 + 1) <= PEER_TOPK] for k2 in range(PEER_TOPK)]


def _batcher_pairs(n):
    pairs = []
    p = 1
    while p < n:
        k = p
        while k >= 1:
            for j in range(k % p, n - k, 2 * k):
                for i in range(min(k, n - j - k)):
                    if (i + j) // (2 * p) == (i + j + k) // (2 * p):
                        pairs.append((i + j, i + j + k))
            k //= 2
        p *= 2
    return pairs


_SORT16 = _batcher_pairs(PEER_TOPK)


def _cmpx(v, i, j):
    a, b = v[i], v[j]
    if b is None:
        return
    if a is None:
        v[i], v[j] = b, None
        return
    v[i], v[j] = jnp.maximum(a, b), jnp.minimum(a, b)


def _sort16_desc(v):
    v = list(v)
    for i, j in _SORT16:
        _cmpx(v, i, j)
    return v


def _bitonic_merge_desc(v):
    v = list(v)
    d = PEER_TOPK // 2
    while d >= 1:
        for i in range(PEER_TOPK):
            if i & d == 0:
                _cmpx(v, i, i + d)
        d //= 2
    return v


def _half_clean(a, b):
    out = []
    for g in range(PEER_TOPK):
        x, y = a[g], b[PEER_TOPK - 1 - g]
        out.append(x if y is None else (y if x is None else jnp.maximum(x, y)))
    return out


def _top16_all_sublanes(scores):
    rows = [_sort16_desc([s[g * 8:(g + 1) * 8, :] for g in range(PEER_TOPK)]) for s in scores]
    for shift in (4, 2, 1):
        partner = [[pltpu.roll(r, shift, 0) for r in rr] for rr in rows]
        rows = [_bitonic_merge_desc(_half_clean(rr, pp)) for rr, pp in zip(rows, partner)]
    return rows


def _route_kernel(h_ref, wqt_ref, keys_ref, rk2_ref, e2_ref, c1_ref, e1_ref, q_scr, s_scr, a_scr, f_scr):
    half = PEER_DKEY // 2
    K = PEER_TOPK
    q_scr[...] = _dot(wqt_ref[...], h_ref[...]).astype(BF16)

    def score_matmuls(h_src, slot):
        base = pl.multiple_of(h_src * PEER_DKEY, PEER_DKEY)
        for p in range(2):
            s_scr[p, slot] = _dot(keys_ref[p], q_scr[pl.ds(base + p * half, half), :])

    score_matmuls(0, 0)

    def stage_a(h, carry):
        score_matmuls(jnp.minimum(h + 1, PEER_HEADS - 1), h + 1)
        tops = _top16_all_sublanes([s_scr[p, h] for p in range(2)])
        for p in range(2):
            for k in range(K):
                a_scr[p, k, pl.ds(h, 1), :] = tops[p][k][0:1, :]
        return carry

    lax.fori_loop(0, PEER_HEADS, stage_a, 0)

    a1 = [a_scr[0, k] for k in range(K)]
    a2 = [a_scr[1, k] for k in range(K)]
    cand = [[a1[k1] + a2[k2] for k1 in col] for k2, col in enumerate(_CAND_COLS)]
    g0 = [cand[k2][0] for k2 in range(K)]
    rest = [cand[k2][i] for k2 in range(K) for i in range(1, len(_CAND_COLS[k2]))]
    rest += [None] * (-len(rest) % K)
    groups = [g0] + [_sort16_desc(rest[i:i + K]) for i in range(0, len(rest), K)]
    while len(groups) > 2:
        merged = [_bitonic_merge_desc(_half_clean(groups[i], groups[i + 1])) for i in range(0, len(groups) - 1, 2)]
        groups = merged + ([groups[-1]] if len(groups) % 2 else [])
    last = [x for x in _half_clean(groups[0], groups[1]) if x is not None]
    tau = functools.reduce(jnp.minimum, last)
    cmax = a1[0] + a2[0]
    zsum = None
    for k2, col in enumerate(_CAND_COLS):
        phi = None
        for i, k1 in enumerate(col):
            c = cand[k2][i]
            hit = c >= tau
            term = jnp.where(hit, jnp.exp(c - cmax), 0.0)
            zsum = term if zsum is None else zsum + term
            lo = jnp.where(hit, a1[k1], jnp.inf)
            phi = lo if phi is None else jnp.minimum(phi, lo)
        f_scr[k2] = phi
    f_scr[K] = 1.0 / zsum

    def prefix_count(pred, thr):
        b8 = pred(thr[7])
        b4 = pred(jnp.where(b8, thr[11], thr[3]))
        b2 = pred(jnp.where(b8, jnp.where(b4, thr[13], thr[9]), jnp.where(b4, thr[5], thr[1])))
        b1 = pred(jnp.where(b8,
                            jnp.where(b4, jnp.where(b2, thr[14], thr[12]), jnp.where(b2, thr[10], thr[8])),
                            jnp.where(b4, jnp.where(b2, thr[6], thr[4]), jnp.where(b2, thr[2], thr[0]))))
        b0 = pred(thr[15])
        bit = lambda b, v: jnp.where(b, v, 0.0)
        return bit(b8, 8.0) + bit(b4, 4.0) + bit(b2, 2.0) + bit(b1, 1.0) + bit(b0, 1.0)

    def stage_c(h, carry):
        s1 = s_scr[0, h]
        s2 = s_scr[1, h]
        shape = s2.shape
        top2 = [jnp.broadcast_to(a_scr[1, k, pl.ds(h, 1), :], shape) for k in range(K)]
        phi = [jnp.broadcast_to(f_scr[k, pl.ds(h, 1), :], shape) for k in range(K)]
        rk2_ref[h] = prefix_count(lambda t: t > s2, top2).astype(BF16)
        e2_ref[h] = jnp.exp(s2 - a_scr[1, 0, pl.ds(h, 1), :]).astype(BF16)
        c1_ref[h] = prefix_count(lambda t: s1 >= t, phi)
        e1_ref[h] = jnp.exp(s1 - a_scr[0, 0, pl.ds(h, 1), :]) * f_scr[K, pl.ds(h, 1), :]
        return carry

    lax.fori_loop(0, PEER_HEADS, stage_c, 0)


def _peer_route(h2t, wqt, keys):
    D, T = h2t.shape
    tb = PEER_TB
    PH = PEER_HEADS
    blk = pl.BlockSpec((PH, N_KEYS, tb), lambda i: (0, 0, i))
    return pl.pallas_call(
        _route_kernel,
        grid=(T // tb,),
        in_specs=[pl.BlockSpec((D, tb), lambda i: (0, i)),
                  pl.BlockSpec((PH * PEER_DKEY, D), lambda i: (0, 0)),
                  pl.BlockSpec((2, N_KEYS, PEER_DKEY // 2), lambda i: (0, 0, 0))],
        out_specs=[blk, blk, blk, blk],
        out_shape=[jax.ShapeDtypeStruct((PH, N_KEYS, T), BF16),
                   jax.ShapeDtypeStruct((PH, N_KEYS, T), BF16),
                   jax.ShapeDtypeStruct((PH, N_KEYS, T), F32),
                   jax.ShapeDtypeStruct((PH, N_KEYS, T), F32)],
        scratch_shapes=[pltpu.VMEM((PH * PEER_DKEY, tb), BF16),
                        pltpu.VMEM((2, PH + 1, N_KEYS, tb), F32),
                        pltpu.VMEM((2, PEER_TOPK, PH, tb), F32),
                        pltpu.VMEM((PEER_TOPK + 1, PH, tb), F32)],
        compiler_params=_cparams(("parallel",)),
        name="peer_route",
    )(h2t, wqt, keys)


EXP_TB = 512
EXP_EB = 1024
BF16_ROWS = 16
LANES = 256


def _expert_step_order(nsub, ngroup):
    order = []
    per = nsub // ngroup
    for g in range(ngroup):
        order.append(("v", g))
        for j in range(g * per, (g + 1) * per):
            order += [("k", j), ("g", j)]
    return tuple(order)


def _experts_kernel(h_ref, u_ref, vt_ref, rk2_ref, e2_ref, c1_ref, e1_ref, x1_ref, mod_ref, g2_ref, b2_ref,
                    o_ref, acc_scr, at_scr, w0_scr, w1_scr):
    e = pl.program_id(1)
    ne = N_EXPERTS // EXP_EB
    nsub = EXP_EB // N_KEYS
    ngrp = N_KEYS // BF16_ROWS
    tb = h_ref.shape[1]
    nhalf = tb // LANES
    assert nsub % nhalf == 0

    def value_matmul(w_read, half):
        cols = slice(half * LANES, (half + 1) * LANES)
        acc_scr[:, cols] += _dot(vt_ref[0], w_read[:, cols])

    def key_matmul(j):
        krows = slice(j * N_KEYS, (j + 1) * N_KEYS)
        at_scr[krows, :] = _dot(u_ref[krows, :], h_ref[...])

    def gated_activation(j, w_write):
        for lt in range(nhalf):
            cols = slice(lt * LANES, (lt + 1) * LANES)
            gates = [None] * ngrp
            for h in range(PEER_HEADS):
                cnt = jnp.broadcast_to(c1_ref[h, j:j + 1, cols], (BF16_ROWS, LANES)).astype(BF16)
                e1 = jnp.broadcast_to(e1_ref[h, j:j + 1, cols], (BF16_ROWS, LANES)).astype(BF16)
                for r in range(ngrp):
                    rows = slice(r * BF16_ROWS, (r + 1) * BF16_ROWS)
                    term = jnp.where(rk2_ref[h, rows, cols] < cnt, e2_ref[h, rows, cols],
                                     jnp.zeros((), BF16)) * e1
                    gates[r] = term if gates[r] is None else gates[r] + term
            for r in range(ngrp):
                rows = slice(j * N_KEYS + r * BF16_ROWS, j * N_KEYS + (r + 1) * BF16_ROWS)
                a = at_scr[rows, cols].astype(BF16)
                act = (0.5 * a) * (1.0 + lax.erf(a * (2.0 ** -0.5)))
                w_write[rows, cols] = gates[r] * act

    def step(w_write, w_read):
        for kind, idx in _expert_step_order(nsub, nhalf):
            if kind == "v":
                value_matmul(w_read, idx)
            elif kind == "k":
                key_matmul(idx)
            else:
                gated_activation(idx, w_write)

    parity = lax.rem(e, 2)

    @pl.when(e == 0)
    def _():
        acc_scr[...] = jnp.zeros_like(acc_scr)
        w1_scr[...] = jnp.zeros_like(w1_scr)

    @pl.when((e < ne) & (parity == 0))
    def _():
        step(w0_scr, w1_scr)

    @pl.when((e < ne) & (parity == 1))
    def _():
        step(w1_scr, w0_scr)

    @pl.when(e == ne)
    def _():
        w_last = w1_scr if ne % 2 == 0 else w0_scr
        for half in range(nhalf):
            value_matmul(w_last, half)
        yf = acc_scr[...].T
        gt2 = mod_ref[0, 5:6, :]
        r = ALPHA * x1_ref[...] + gt2 * yf
        o_ref[...] = _layer_norm_rows(r) * g2_ref[...] + b2_ref[...]


def _peer_experts(h2t, u_b, vt_b, rk2, e2, c1, e1, x1, mod3, ln_g, ln_b, S):
    D, T = h2t.shape
    tb, eb = EXP_TB, EXP_EB
    PH = PEER_HEADS
    route = pl.BlockSpec((PH, N_KEYS, tb), lambda i, e: (0, 0, i))
    ne = N_EXPERTS // eb
    key1 = pl.BlockSpec((PH, eb // N_KEYS, tb), lambda i, e: (0, jnp.minimum(e, ne - 1), i))
    return pl.pallas_call(
        _experts_kernel,
        grid=(T // tb, ne + 1),
        in_specs=[pl.BlockSpec((D, tb), lambda i, e: (0, i)),
                  pl.BlockSpec((eb, D), lambda i, e: (jnp.minimum(e, ne - 1), 0)),
                  pl.BlockSpec((1, D, eb), lambda i, e: (jnp.maximum(e - 1, 0), 0, 0)),
                  route, route, key1, key1,
                  pl.BlockSpec((tb, D), lambda i, e: (i, 0)),
                  pl.BlockSpec((1, 6, D), lambda i, e: ((i * tb) // S, 0, 0)),
                  pl.BlockSpec((1, D), lambda i, e: (0, 0)),
                  pl.BlockSpec((1, D), lambda i, e: (0, 0))],
        out_specs=pl.BlockSpec((tb, D), lambda i, e: (i, 0)),
        out_shape=jax.ShapeDtypeStruct((T, D), F32),
        scratch_shapes=[pltpu.VMEM((D, tb), F32), pltpu.VMEM((eb, tb), F32),
                        pltpu.VMEM((eb, tb), BF16), pltpu.VMEM((eb, tb), BF16)],
        compiler_params=_cparams(("parallel", "arbitrary")),
        name="peer_experts",
    )(h2t, u_b, vt_b, rk2, e2, c1, e1, x1, mod3, ln_g, ln_b)


def kernel(x, c, w_ada, b_ada, w_in, b_if, conv_w, conv_b, da_lambda, da_subln_g, ml_norm_g, w_br_attn,
           w_br_mlstm, w_out, ln1_g, ln1_b, peer_wq, peer_keys, peer_u, peer_v, ln2_g, ln2_b):
    B, S, D = x.shape
    T = B * S
    assert D == D_MODEL and S % DA_TQ == 0 and S % 1024 == 0
    l = 0
    lambda_init = 0.8 - 0.6 * math.exp(-0.3 * l)

    mod3 = _modulation(c, w_ada[l], b_ada[l]).reshape(B, 6, D)

    w = w_in[l]
    o_mq = 3 * D
    o_mv = o_mq + 2 * ML_HEADS * ML_DK
    o_mo = o_mv + D
    o_if = o_mo + D
    o_ga = o_if + 2 * ML_HEADS
    o_gm = o_ga + D
    starts = (0, D, 2 * D, o_mq, o_mv, o_mo, o_ga, o_gm)
    q_fold = (DA_DK ** -0.5) * math.log2(math.e)
    pieces = [w[:, s0:s0 + D] for s0 in starts]
    pieces[0] = pieces[0] * q_fold
    w8 = jnp.stack(pieces).astype(BF16)
    w_if = w[:, o_if:o_if + 2 * ML_HEADS]
    wg = jnp.pad(w_if, ((0, 0), (0, 128 - 2 * ML_HEADS))).astype(BF16)
    wgt = w_if.T.astype(BF16)
    bias8 = b_if[l].reshape(2 * ML_HEADS)
    bcol = jnp.pad(bias8, (0, 128 - 2 * ML_HEADS)).reshape(1, 128)
    brow = bias8.reshape(2 * ML_HEADS, 1)

    x2 = x.reshape(T, D)
    z, gcol, grow = _in_proj(x2, mod3, w8, wg, wgt, bcol, brow, S)
    z4 = z.reshape(8, B, S, D)

    ya = _diff_attention(z4, da_lambda[l], da_subln_g[l], B, S, lambda_init)
    ym = _mlstm(z4, gcol.reshape(B, S, 128), grow, conv_w[l], conv_b[l].reshape(1, -1),
                ml_norm_g[l].reshape(1, -1), B, S)

    x1, h2 = _merge(ya.reshape(T, D), ym.reshape(T, D), z, x2, mod3,
                    w_br_attn[l].astype(BF16), w_br_mlstm[l].astype(BF16), w_out[l].astype(BF16),
                    ln1_g[l].reshape(1, D), ln1_b[l].reshape(1, D), S)

    rk2, e2, c1, e1 = _peer_route(h2, peer_wq[l].T.astype(BF16), peer_keys[l].astype(BF16))
    vt_blocks = peer_v[l].reshape(N_EXPERTS // EXP_EB, EXP_EB, D).transpose(0, 2, 1).astype(BF16)
    out = _peer_experts(h2, peer_u[l].astype(BF16), vt_blocks, rk2, e2, c1, e1, x1, mod3,
                        ln2_g[l].reshape(1, D), ln2_b[l].reshape(1, D), S)
    return out.reshape(B, S, D)
```

```python
import functools
import math

import jax
import jax.numpy as jnp
from jax import lax
from jax.experimental import pallas as pl
from jax.experimental.pallas import tpu as pltpu

D_MODEL = 1024
DA_HEADS = 8
DA_DK = 64
DA_DV = 2 * DA_DK
ML_HEADS = 4
ML_DK = 128
ML_DV = 256
ML_CHUNK = 128
CONV_K = 4
PEER_HEADS = 8
PEER_TOPK = 16
N_KEYS = 128
N_EXPERTS = N_KEYS * N_KEYS
PEER_DKEY = 128
DEPTH = 1
ALPHA = (2 * DEPTH) ** 0.25
LN_EPS = 1e-5

F32 = jnp.float32
BF16 = jnp.bfloat16
NEG_INF = float("-inf")

VMEM_LIMIT_BYTES = 56 * 1024 * 1024


def _cparams(sem):
    return pltpu.CompilerParams(dimension_semantics=sem, vmem_limit_bytes=VMEM_LIMIT_BYTES)


def _layer_norm_rows(x):
    mu = jnp.mean(x, axis=-1, keepdims=True)
    xc = x - mu
    var = jnp.mean(xc * xc, axis=-1, keepdims=True)
    return xc * lax.rsqrt(var + LN_EPS)


def _dot(a, b):
    return jnp.dot(a, b, preferred_element_type=F32)


def _dot_nt(a, b):
    return lax.dot_general(a, b, (((1,), (1,)), ((), ())), preferred_element_type=F32)


def _dot_tn(a, b):
    return lax.dot_general(a, b, (((0,), (0,)), ((), ())), preferred_element_type=F32)


def _mod_kernel(c_ref, w_ref, b_ref, o_ref):
    c = c_ref[...]
    a = c * jax.nn.sigmoid(c)
    o_ref[...] = jnp.dot(a, w_ref[...], preferred_element_type=F32,
                         precision=lax.Precision.HIGHEST) + b_ref[...]


def _modulation(c, w_ada, b_ada):
    B, D = c.shape
    N = w_ada.shape[1]
    tn = 1024
    return pl.pallas_call(
        _mod_kernel,
        grid=(N // tn,),
        in_specs=[pl.BlockSpec((B, D), lambda n: (0, 0)),
                  pl.BlockSpec((D, tn), lambda n: (0, n)),
                  pl.BlockSpec((1, tn), lambda n: (0, n))],
        out_specs=pl.BlockSpec((B, tn), lambda n: (0, n)),
        out_shape=jax.ShapeDtypeStruct((B, N), F32),
        compiler_params=_cparams(("arbitrary",)),
        name="modulation",
    )(c, w_ada, b_ada.reshape(1, N))


INPROJ_TM = 1024
INPROJ_ROWS = 256
GATE_ROWS = 2 * ML_HEADS
GATE_LANES = 128
SUBLANES = 8


def _inproj_kernel(x_ref, mod_ref, w_ref, wg_ref, wgt_ref, bcol_ref, brow_ref,
                   z_ref, gcol_ref, grow_ref, h_scr):
    n = pl.program_id(1)

    @pl.when(n == 0)
    def _():
        sh1 = mod_ref[0, 0:1, :]
        sc1 = mod_ref[0, 1:2, :]
        tm = x_ref.shape[0]
        for c in range(tm // INPROJ_ROWS):
            rows = slice(c * INPROJ_ROWS, (c + 1) * INPROJ_ROWS)
            hb = (_layer_norm_rows(x_ref[rows, :]) * (1.0 + sc1) + sh1).astype(BF16)
            h_scr[rows, :] = hb
            z_ref[0, rows, :] = _dot(hb, w_ref[0]).astype(BF16)
            gcol_ref[rows, :] = _dot(hb, wg_ref[...]) + bcol_ref[...]
            grow_ref[:, rows] = _dot_nt(wgt_ref[...], hb) + brow_ref[...]

    @pl.when(n > 0)
    def _():
        z_ref[0] = _dot(h_scr[...], w_ref[0]).astype(BF16)


def _in_proj(x2, mod3, w8, wg, wgt, bcol, brow, S):
    T, D = x2.shape
    tm = INPROJ_TM
    npiece = w8.shape[0]
    return pl.pallas_call(
        _inproj_kernel,
        grid=(T // tm, npiece),
        in_specs=[pl.BlockSpec((tm, D), lambda i, n: (i, 0)),
                  pl.BlockSpec((1, 6, D), lambda i, n: ((i * tm) // S, 0, 0)),
                  pl.BlockSpec((1, D, D), lambda i, n: (n, 0, 0)),
                  pl.BlockSpec((D, GATE_LANES), lambda i, n: (0, 0)),
                  pl.BlockSpec((GATE_ROWS, D), lambda i, n: (0, 0)),
                  pl.BlockSpec((1, GATE_LANES), lambda i, n: (0, 0)),
                  pl.BlockSpec((GATE_ROWS, 1), lambda i, n: (0, 0))],
        out_specs=[pl.BlockSpec((1, tm, D), lambda i, n: (n, i, 0)),
                   pl.BlockSpec((tm, GATE_LANES), lambda i, n: (i, 0)),
                   pl.BlockSpec((GATE_ROWS, tm), lambda i, n: (0, i))],
        out_shape=[jax.ShapeDtypeStruct((npiece, T, D), BF16),
                   jax.ShapeDtypeStruct((T, GATE_LANES), F32),
                   jax.ShapeDtypeStruct((GATE_ROWS, T), F32)],
        scratch_shapes=[pltpu.VMEM((tm, D), BF16)],
        compiler_params=_cparams(("parallel", "arbitrary")),
        name="in_proj",
    )(x2, mod3, w8, wg, wgt, bcol, brow)


DA_TQ = 256


def _diffattn_kernel(lam_ref, g_ref, q_ref, k_ref, v_ref, o_ref, s_scr, *, S, lambda_init):
    tq = DA_TQ
    nq = S // tq
    lam = lam_ref[...]
    t1 = jnp.sum(lam[0:1] * lam[1:2], axis=-1, keepdims=True)
    t2 = jnp.sum(lam[2:3] * lam[3:4], axis=-1, keepdims=True)
    lam_val = jnp.exp(t1) - jnp.exp(t2) + lambda_init
    first_map = lax.broadcasted_iota(jnp.int32, (1, DA_DV), 1) < DA_DK
    gain = g_ref[...] * (1.0 - lambda_init)
    row = lax.broadcasted_iota(jnp.int32, (tq, tq), 0)
    col = lax.broadcasted_iota(jnp.int32, (tq, tq), 1)
    causal = col <= row

    def scores(qi):
        q0 = qi * tq
        qs = q_ref[0, 0, q0:q0 + tq, :]
        zero = jnp.zeros_like(qs)
        for mp, qm in enumerate((jnp.where(first_map, qs, zero), jnp.where(first_map, zero, qs))):
            s_scr[qi % 2, mp, :, 0:q0 + tq] = _dot_nt(qm, k_ref[0, 0, 0:q0 + tq, :])

    def finish(qi):
        q0 = qi * tq
        slot = qi % 2
        probs = []
        for mp in range(2):
            s_diag = jnp.where(causal, s_scr[slot, mp, :, q0:q0 + tq], NEG_INF)
            m = jnp.max(s_diag, axis=-1, keepdims=True)
            if qi > 0:
                s_off = s_scr[slot, mp, :, 0:q0]
                m = jnp.maximum(m, jnp.max(s_off, axis=-1, keepdims=True))
            p_diag = jnp.exp2(s_diag - m)
            l = jnp.sum(p_diag, axis=-1, keepdims=True)
            p_off = None
            if qi > 0:
                p_off = jnp.exp2(s_off - m)
                l = l + jnp.sum(p_off, axis=-1, keepdims=True)
            probs.append((p_diag, p_off, l))
        (p1d, p1o, l1), (p2d, p2o, l2) = probs
        ratio = lam_val * l1 / l2
        o = _dot((p1d - p2d * ratio).astype(BF16), v_ref[0, 0, q0:q0 + tq, :])
        if qi > 0:
            o = o + _dot((p1o - p2o * ratio).astype(BF16), v_ref[0, 0, 0:q0, :])
        o = o / l1
        o = o * lax.rsqrt(jnp.mean(o * o, axis=-1, keepdims=True) + LN_EPS) * gain
        o_ref[0, q0:q0 + tq, :] = o.astype(BF16)

    scores(0)
    for qi in range(nq):
        if qi + 1 < nq:
            scores(qi + 1)
        finish(qi)


def _diff_attention(z4, da_lambda, subln_g, B, S, lambda_init):
    kern = functools.partial(_diffattn_kernel, S=S, lambda_init=lambda_init)
    return pl.pallas_call(
        kern,
        grid=(B, DA_HEADS),
        in_specs=[pl.BlockSpec((4, DA_DK), lambda b, h: (0, 0)),
                  pl.BlockSpec((1, DA_DV), lambda b, h: (0, 0)),
                  pl.BlockSpec((1, 1, S, DA_DV), lambda b, h: (0, b, 0, h)),
                  pl.BlockSpec((1, 1, S, DA_DV), lambda b, h: (1, b, 0, h)),
                  pl.BlockSpec((1, 1, S, DA_DV), lambda b, h: (2, b, 0, h))],
        out_specs=pl.BlockSpec((1, S, DA_DV), lambda b, h: (b, 0, h)),
        out_shape=jax.ShapeDtypeStruct((B, S, DA_HEADS * DA_DV), BF16),
        scratch_shapes=[pltpu.VMEM((2, 2, DA_TQ, S), F32)],
        compiler_params=_cparams(("parallel", "parallel")),
        name="diff_attention",
    )(da_lambda, subln_g.reshape(1, DA_DV), z4, z4, z4)


def _mlstm_kernel(qk_ref, v_ref, og_ref, gcol_ref, grow_ref, cw_ref, cb_ref, ng_ref, o_ref,
                  qc_scr, kc_scr, c_scr, n_scr, m_scr, *, S):
    L = ML_CHUNK
    H = ML_HEADS
    srow = lax.broadcasted_iota(jnp.int32, (S, ML_DK), 0)

    for cb in range(2 * H):
        cols = slice(cb * ML_DK, (cb + 1) * ML_DK)
        x = qk_ref[0, 0, :, cols].astype(F32)
        y = x * cw_ref[CONV_K - 1:CONV_K, cols] + cb_ref[:, cols]
        for j in range(1, CONV_K):
            xs = jnp.where(srow >= j, pltpu.roll(x, j, 0), 0.0)
            y = y + xs * cw_ref[CONV_K - 1 - j:CONV_K - j, cols]
        y = y * jax.nn.sigmoid(y)
        if cb < H:
            qc_scr[:, cols] = y
        else:
            kc_scr[:, (cb - H) * ML_DK:(cb - H + 1) * ML_DK] = y * (ML_DK ** -0.5)

    r_i = lax.broadcasted_iota(jnp.int32, (L, L), 0)
    c_i = lax.broadcasted_iota(jnp.int32, (L, L), 1)
    causal = c_i <= r_i
    tril = causal.astype(F32)
    triu = (r_i <= c_i).astype(F32)
    c_scr[...] = jnp.zeros_like(c_scr)
    n_scr[...] = jnp.zeros_like(n_scr)
    m_scr[...] = jnp.zeros_like(m_scr)

    def head_chunk(hh, t0, gc, gr, b_cols, b_rows):
        Ct = c_scr[hh]
        n_row = n_scr[hh]
        m = m_scr[hh]
        q = qc_scr[pl.ds(t0, L), hh * ML_DK:(hh + 1) * ML_DK]
        k = kc_scr[pl.ds(t0, L), hh * ML_DK:(hh + 1) * ML_DK]
        v = v_ref[0, 0, pl.ds(t0, L), hh * ML_DV:(hh + 1) * ML_DV]
        ngain = ng_ref[:, hh * ML_DV:(hh + 1) * ML_DV]
        ig_col = gc[:, hh:hh + 1]
        ig_row = gr[hh:hh + 1, :]
        b_col = b_cols[:, H + hh:H + hh + 1]
        b_row = b_rows[H + hh:H + hh + 1, :]
        dm = jnp.where(causal, b_col - b_row + ig_row, NEG_INF)
        m_inter = b_col + m
        m_t = jnp.maximum(m_inter, jnp.max(dm, axis=-1, keepdims=True))
        w = jnp.exp(dm - m_t)
        qb = q.astype(BF16)
        kb = k.astype(BF16)
        p = w * _dot_nt(qb, kb)
        inter = jnp.exp(m_inter - m_t)
        num = _dot(p.astype(BF16), v) + inter * _dot(qb, Ct.astype(BF16))
        nq = jnp.sum(p, axis=-1, keepdims=True) + inter * jnp.sum(q * n_row, axis=-1, keepdims=True)
        hout = num / jnp.maximum(jnp.abs(nq), jnp.exp(-m_t))
        hout = hout * lax.rsqrt(jnp.mean(hout * hout, axis=-1, keepdims=True) + LN_EPS) * ngain
        og = og_ref[0, 0, pl.ds(t0, L), hh * ML_DV:(hh + 1) * ML_DV].astype(F32)
        o_ref[0, pl.ds(t0, L), hh * ML_DV:(hh + 1) * ML_DV] = (hout * jax.nn.sigmoid(og)).astype(BF16)
        m_new = m_t[L - 1:L, :]
        b_last = b_col[L - 1:L, :]
        decay = jnp.exp(b_last + m - m_new)
        w_s = jnp.exp(b_last - b_col + ig_col - m_new)
        c_scr[hh] = decay * Ct + _dot_tn(kb, (v.astype(F32) * w_s).astype(BF16))
        n_scr[hh] = decay * n_row + jnp.sum(k * w_s, axis=0, keepdims=True)
        m_scr[hh] = m_new

    def chunk(ci, carry):
        t0 = pl.multiple_of(ci * L, L)
        gc = gcol_ref[0, pl.ds(t0, L), :]
        gr = grow_ref[:, pl.ds(t0, L)]
        b_cols = jnp.dot(tril, jax.nn.log_sigmoid(gc), preferred_element_type=F32,
                         precision=lax.Precision.HIGHEST)
        b_rows = jnp.dot(jax.nn.log_sigmoid(gr), triu, preferred_element_type=F32,
                         precision=lax.Precision.HIGHEST)
        for hh in range(H):
            head_chunk(hh, t0, gc, gr, b_cols, b_rows)
        return carry

    lax.fori_loop(0, S // L, chunk, 0)


def _mlstm(z4, gcol3, grow, conv_w, conv_b, norm_g, B, S):
    kern = functools.partial(_mlstm_kernel, S=S)
    H = ML_HEADS
    D = H * ML_DV
    piece = lambda n: pl.BlockSpec((1, 1, S, D), lambda b: (n, b, 0, 0))
    return pl.pallas_call(
        kern,
        grid=(B,),
        in_specs=[piece(3), piece(4), piece(5),
                  pl.BlockSpec((1, S, GATE_LANES), lambda b: (b, 0, 0)),
                  pl.BlockSpec((GATE_ROWS, S), lambda b: (0, b)),
                  pl.BlockSpec((CONV_K, 2 * H * ML_DK), lambda b: (0, 0)),
                  pl.BlockSpec((1, 2 * H * ML_DK), lambda b: (0, 0)),
                  pl.BlockSpec((1, D), lambda b: (0, 0))],
        out_specs=pl.BlockSpec((1, S, D), lambda b: (b, 0, 0)),
        out_shape=jax.ShapeDtypeStruct((B, S, D), BF16),
        scratch_shapes=[pltpu.VMEM((S, H * ML_DK), F32), pltpu.VMEM((S, H * ML_DK), F32),
                        pltpu.VMEM((H, ML_DK, ML_DV), F32), pltpu.VMEM((H, 1, ML_DK), F32),
                        pltpu.VMEM((H, 1, 1), F32)],
        compiler_params=_cparams(("parallel",)),
        name="mlstm",
    )(z4, z4, z4, gcol3, grow, conv_w, conv_b, norm_g)


MERGE_ROWS = 256


def _merge_kernel(ya_ref, ym_ref, ga_ref, gm_ref, x_ref, mod_ref, wa_ref, wm_ref, wo_ref, g1_ref, b1_ref,
                  x1_ref, h2_ref):
    gt1 = mod_ref[0, 2:3, :]
    sh2 = mod_ref[0, 3:4, :]
    sc2 = mod_ref[0, 4:5, :]
    for c in range(ya_ref.shape[0] // MERGE_ROWS):
        rows = slice(c * MERGE_ROWS, (c + 1) * MERGE_ROWS)
        ya = _dot(ya_ref[rows, :], wa_ref[...])
        ym = _dot(ym_ref[rows, :], wm_ref[...])
        y = (jax.nn.sigmoid(ga_ref[0, rows, :].astype(F32)) * ya
             + jax.nn.sigmoid(gm_ref[0, rows, :].astype(F32)) * ym)
        y2 = _dot(y.astype(BF16), wo_ref[...])
        x1 = _layer_norm_rows(ALPHA * x_ref[rows, :] + gt1 * y2) * g1_ref[...] + b1_ref[...]
        x1_ref[rows, :] = x1
        h2_ref[:, rows] = (_layer_norm_rows(x1) * (1.0 + sc2) + sh2).T.astype(BF16)


def _merge(ya2, ym2, z3, x2, mod3, wa, wm, wo, ln_g, ln_b, S):
    T, D = x2.shape
    tm = 512
    tok = lambda i: (i, 0)
    const = lambda i: (0, 0)
    return pl.pallas_call(
        _merge_kernel,
        grid=(T // tm,),
        in_specs=[pl.BlockSpec((tm, D), tok), pl.BlockSpec((tm, D), tok),
                  pl.BlockSpec((1, tm, D), lambda i: (6, i, 0)),
                  pl.BlockSpec((1, tm, D), lambda i: (7, i, 0)),
                  pl.BlockSpec((tm, D), tok),
                  pl.BlockSpec((1, 6, D), lambda i: ((i * tm) // S, 0, 0)),
                  pl.BlockSpec((D, D), const), pl.BlockSpec((D, D), const), pl.BlockSpec((D, D), const),
                  pl.BlockSpec((1, D), const), pl.BlockSpec((1, D), const)],
        out_specs=[pl.BlockSpec((tm, D), tok), pl.BlockSpec((D, tm), lambda i: (0, i))],
        out_shape=[jax.ShapeDtypeStruct((T, D), F32), jax.ShapeDtypeStruct((D, T), BF16)],
        compiler_params=_cparams(("parallel",)),
        name="merge",
    )(ya2, ym2, z3, z3, x2, mod3, wa, wm, wo, ln_g, ln_b)


PEER_TB = 256
_CAND_COLS = [[k1 for k1 in range(PEER_TOPK) if (k1 + 1) * (k2 + 1) <= PEER_TOPK] for k2 in range(PEER_TOPK)]


def _batcher_pairs(n):
    pairs = []
    p = 1
    while p < n:
        k = p
        while k >= 1:
            for j in range(k % p, n - k, 2 * k):
                for i in range(min(k, n - j - k)):
                    if (i + j) // (2 * p) == (i + j + k) // (2 * p):
                        pairs.append((i + j, i + j + k))
            k //= 2
        p *= 2
    return pairs


_SORT16 = _batcher_pairs(PEER_TOPK)


def _cmpx(v, i, j):
    a, b = v[i], v[j]
    if b is None:
        return
    if a is None:
        v[i], v[j] = b, None
        return
    v[i], v[j] = jnp.maximum(a, b), jnp.minimum(a, b)


def _sort16_desc(v):
    v = list(v)
    for i, j in _SORT16:
        _cmpx(v, i, j)
    return v


def _bitonic_merge_desc(v):
    v = list(v)
    d = PEER_TOPK // 2
    while d >= 1:
        for i in range(PEER_TOPK):
            if i & d == 0:
                _cmpx(v, i, i + d)
        d //= 2
    return v


def _half_clean(a, b):
    out = []
    for g in range(PEER_TOPK):
        x, y = a[g], b[PEER_TOPK - 1 - g]
        out.append(x if y is None else (y if x is None else jnp.maximum(x, y)))
    return out


def _top16_all_sublanes(scores):
    assert N_KEYS == PEER_TOPK * SUBLANES
    rows = [_sort16_desc([s[g * SUBLANES:(g + 1) * SUBLANES, :] for g in range(PEER_TOPK)]) for s in scores]
    for shift in (4, 2, 1):
        partner = [[pltpu.roll(r, shift, 0) for r in rr] for rr in rows]
        rows = [_bitonic_merge_desc(_half_clean(rr, pp)) for rr, pp in zip(rows, partner)]
    return rows


def _route_kernel(h_ref, wqt_ref, keys_ref, rk2_ref, e2_ref, c1_ref, e1_ref, q_scr, s_scr, a_scr, f_scr):
    half = PEER_DKEY // 2
    K = PEER_TOPK
    q_scr[...] = _dot(wqt_ref[...], h_ref[...]).astype(BF16)

    def score_matmuls(h_src, slot):
        base = pl.multiple_of(h_src * PEER_DKEY, PEER_DKEY)
        for p in range(2):
            s_scr[p, slot] = _dot(keys_ref[p], q_scr[pl.ds(base + p * half, half), :])

    score_matmuls(0, 0)

    def stage_a(h, carry):
        score_matmuls(jnp.minimum(h + 1, PEER_HEADS - 1), h + 1)
        tops = _top16_all_sublanes([s_scr[p, h] for p in range(2)])
        for p in range(2):
            for k in range(K):
                a_scr[p, k, pl.ds(h, 1), :] = tops[p][k][0:1, :]
        return carry

    lax.fori_loop(0, PEER_HEADS, stage_a, 0)

    a1 = [a_scr[0, k] for k in range(K)]
    a2 = [a_scr[1, k] for k in range(K)]
    cand = [[a1[k1] + a2[k2] for k1 in col] for k2, col in enumerate(_CAND_COLS)]
    g0 = [cand[k2][0] for k2 in range(K)]
    rest = [cand[k2][i] for k2 in range(K) for i in range(1, len(_CAND_COLS[k2]))]
    rest += [None] * (-len(rest) % K)
    groups = [g0] + [_sort16_desc(rest[i:i + K]) for i in range(0, len(rest), K)]
    while len(groups) > 2:
        merged = [_bitonic_merge_desc(_half_clean(groups[i], groups[i + 1])) for i in range(0, len(groups) - 1, 2)]
        groups = merged + ([groups[-1]] if len(groups) % 2 else [])
    last = [x for x in _half_clean(groups[0], groups[1]) if x is not None]
    tau = functools.reduce(jnp.minimum, last)
    cmax = a1[0] + a2[0]
    zsum = None
    for k2, col in enumerate(_CAND_COLS):
        phi = None
        for i, k1 in enumerate(col):
            c = cand[k2][i]
            hit = c >= tau
            term = jnp.where(hit, jnp.exp(c - cmax), 0.0)
            zsum = term if zsum is None else zsum + term
            lo = jnp.where(hit, a1[k1], jnp.inf)
            phi = lo if phi is None else jnp.minimum(phi, lo)
        f_scr[k2] = phi
    f_scr[K] = 1.0 / zsum

    def prefix_count(pred, thr):
        b8 = pred(thr[7])
        b4 = pred(jnp.where(b8, thr[11], thr[3]))
        b2 = pred(jnp.where(b8, jnp.where(b4, thr[13], thr[9]), jnp.where(b4, thr[5], thr[1])))
        b1 = pred(jnp.where(b8,
                            jnp.where(b4, jnp.where(b2, thr[14], thr[12]), jnp.where(b2, thr[10], thr[8])),
                            jnp.where(b4, jnp.where(b2, thr[6], thr[4]), jnp.where(b2, thr[2], thr[0]))))
        b0 = pred(thr[15])
        bit = lambda b, v: jnp.where(b, v, 0.0)
        return bit(b8, 8.0) + bit(b4, 4.0) + bit(b2, 2.0) + bit(b1, 1.0) + bit(b0, 1.0)

    def stage_c(h, carry):
        s1 = s_scr[0, h]
        s2 = s_scr[1, h]
        shape = s2.shape
        top2 = [jnp.broadcast_to(a_scr[1, k, pl.ds(h, 1), :], shape) for k in range(K)]
        phi = [jnp.broadcast_to(f_scr[k, pl.ds(h, 1), :], shape) for k in range(K)]
        rk2_ref[h] = prefix_count(lambda t: t > s2, top2).astype(BF16)
        e2_ref[h] = jnp.exp(s2 - a_scr[1, 0, pl.ds(h, 1), :]).astype(BF16)
        c1_ref[h] = prefix_count(lambda t: s1 >= t, phi)
        e1_ref[h] = jnp.exp(s1 - a_scr[0, 0, pl.ds(h, 1), :]) * f_scr[K, pl.ds(h, 1), :]
        return carry

    lax.fori_loop(0, PEER_HEADS, stage_c, 0)


def _peer_route(h2t, wqt, keys):
    D, T = h2t.shape
    tb = PEER_TB
    PH = PEER_HEADS
    blk = pl.BlockSpec((PH, N_KEYS, tb), lambda i: (0, 0, i))
    return pl.pallas_call(
        _route_kernel,
        grid=(T // tb,),
        in_specs=[pl.BlockSpec((D, tb), lambda i: (0, i)),
                  pl.BlockSpec((PH * PEER_DKEY, D), lambda i: (0, 0)),
                  pl.BlockSpec((2, N_KEYS, PEER_DKEY // 2), lambda i: (0, 0, 0))],
        out_specs=[blk, blk, blk, blk],
        out_shape=[jax.ShapeDtypeStruct((PH, N_KEYS, T), BF16),
                   jax.ShapeDtypeStruct((PH, N_KEYS, T), BF16),
                   jax.ShapeDtypeStruct((PH, N_KEYS, T), F32),
                   jax.ShapeDtypeStruct((PH, N_KEYS, T), F32)],
        scratch_shapes=[pltpu.VMEM((PH * PEER_DKEY, tb), BF16),
                        pltpu.VMEM((2, PH + 1, N_KEYS, tb), F32),
                        pltpu.VMEM((2, PEER_TOPK, PH, tb), F32),
                        pltpu.VMEM((PEER_TOPK + 1, PH, tb), F32)],
        compiler_params=_cparams(("parallel",)),
        name="peer_route",
    )(h2t, wqt, keys)


EXP_TB = 512
EXP_EB = 1024
BF16_ROWS = 16
LANES = 256


def _expert_step_order(nsub, ngroup):
    order = []
    per = nsub // ngroup
    for g in range(ngroup):
        order.append(("v", g))
        for j in range(g * per, (g + 1) * per):
            order += [("k", j), ("g", j)]
    return tuple(order)


def _experts_kernel(h_ref, u_ref, vt_ref, rk2_ref, e2_ref, c1_ref, e1_ref, x1_ref, mod_ref, g2_ref, b2_ref,
                    o_ref, acc_scr, at_scr, w0_scr, w1_scr):
    e = pl.program_id(1)
    ne = N_EXPERTS // EXP_EB
    nsub = EXP_EB // N_KEYS
    ngrp = N_KEYS // BF16_ROWS
    tb = h_ref.shape[1]
    nhalf = tb // LANES
    assert nsub % nhalf == 0

    def value_matmul(w_read, half):
        cols = slice(half * LANES, (half + 1) * LANES)
        acc_scr[:, cols] += _dot(vt_ref[0], w_read[:, cols])

    def key_matmul(j):
        krows = slice(j * N_KEYS, (j + 1) * N_KEYS)
        at_scr[krows, :] = _dot(u_ref[krows, :], h_ref[...])

    def gated_activation(j, w_write):
        for lt in range(nhalf):
            cols = slice(lt * LANES, (lt + 1) * LANES)
            gates = [None] * ngrp
            for h in range(PEER_HEADS):
                cnt = jnp.broadcast_to(c1_ref[h, j:j + 1, cols], (BF16_ROWS, LANES)).astype(BF16)
                e1 = jnp.broadcast_to(e1_ref[h, j:j + 1, cols], (BF16_ROWS, LANES)).astype(BF16)
                for r in range(ngrp):
                    rows = slice(r * BF16_ROWS, (r + 1) * BF16_ROWS)
                    term = jnp.where(rk2_ref[h, rows, cols] < cnt, e2_ref[h, rows, cols],
                                     jnp.zeros((), BF16)) * e1
                    gates[r] = term if gates[r] is None else gates[r] + term
            for r in range(ngrp):
                rows = slice(j * N_KEYS + r * BF16_ROWS, j * N_KEYS + (r + 1) * BF16_ROWS)
                a = at_scr[rows, cols].astype(BF16)
                act = (0.5 * a) * (1.0 + lax.erf(a * (2.0 ** -0.5)))
                w_write[rows, cols] = gates[r] * act

    def step(w_write, w_read):
        for kind, idx in _expert_step_order(nsub, nhalf):
            if kind == "v":
                value_matmul(w_read, idx)
            elif kind == "k":
                key_matmul(idx)
            else:
                gated_activation(idx, w_write)

    parity = lax.rem(e, 2)

    @pl.when(e == 0)
    def _():
        acc_scr[...] = jnp.zeros_like(acc_scr)
        w1_scr[...] = jnp.zeros_like(w1_scr)

    @pl.when((e < ne) & (parity == 0))
    def _():
        step(w0_scr, w1_scr)

    @pl.when((e < ne) & (parity == 1))
    def _():
        step(w1_scr, w0_scr)

    @pl.when(e == ne)
    def _():
        w_last = w1_scr if ne % 2 == 0 else w0_scr
        for half in range(nhalf):
            value_matmul(w_last, half)
        yf = acc_scr[...].T
        gt2 = mod_ref[0, 5:6, :]
        r = ALPHA * x1_ref[...] + gt2 * yf
        o_ref[...] = _layer_norm_rows(r) * g2_ref[...] + b2_ref[...]


def _peer_experts(h2t, u_b, vt_b, rk2, e2, c1, e1, x1, mod3, ln_g, ln_b, S):
    D, T = h2t.shape
    tb, eb = EXP_TB, EXP_EB
    PH = PEER_HEADS
    route = pl.BlockSpec((PH, N_KEYS, tb), lambda i, e: (0, 0, i))
    ne = N_EXPERTS // eb
    key1 = pl.BlockSpec((PH, eb // N_KEYS, tb), lambda i, e: (0, jnp.minimum(e, ne - 1), i))
    return pl.pallas_call(
        _experts_kernel,
        grid=(T // tb, ne + 1),
        in_specs=[pl.BlockSpec((D, tb), lambda i, e: (0, i)),
                  pl.BlockSpec((eb, D), lambda i, e: (jnp.minimum(e, ne - 1), 0)),
                  pl.BlockSpec((1, D, eb), lambda i, e: (jnp.maximum(e - 1, 0), 0, 0)),
                  route, route, key1, key1,
                  pl.BlockSpec((tb, D), lambda i, e: (i, 0)),
                  pl.BlockSpec((1, 6, D), lambda i, e: ((i * tb) // S, 0, 0)),
                  pl.BlockSpec((1, D), lambda i, e: (0, 0)),
                  pl.BlockSpec((1, D), lambda i, e: (0, 0))],
        out_specs=pl.BlockSpec((tb, D), lambda i, e: (i, 0)),
        out_shape=jax.ShapeDtypeStruct((T, D), F32),
        scratch_shapes=[pltpu.VMEM((D, tb), F32), pltpu.VMEM((eb, tb), F32),
                        pltpu.VMEM((eb, tb), BF16), pltpu.VMEM((eb, tb), BF16)],
        compiler_params=_cparams(("parallel", "arbitrary")),
        name="peer_experts",
    )(h2t, u_b, vt_b, rk2, e2, c1, e1, x1, mod3, ln_g, ln_b)


def kernel(x, c, w_ada, b_ada, w_in, b_if, conv_w, conv_b, da_lambda, da_subln_g, ml_norm_g, w_br_attn,
           w_br_mlstm, w_out, ln1_g, ln1_b, peer_wq, peer_keys, peer_u, peer_v, ln2_g, ln2_b):
    B, S, D = x.shape
    T = B * S
    assert D == D_MODEL and S % DA_TQ == 0 and S % INPROJ_TM == 0 and S % EXP_TB == 0 and S % ML_CHUNK == 0
    l = 0
    lambda_init = 0.8 - 0.6 * math.exp(-0.3 * l)

    mod3 = _modulation(c, w_ada[l], b_ada[l]).reshape(B, 6, D)

    w = w_in[l]
    o_mq = 3 * D
    o_mv = o_mq + 2 * ML_HEADS * ML_DK
    o_mo = o_mv + D
    o_if = o_mo + D
    o_ga = o_if + GATE_ROWS
    o_gm = o_ga + D
    starts = (0, D, 2 * D, o_mq, o_mv, o_mo, o_ga, o_gm)
    q_fold = (DA_DK ** -0.5) * math.log2(math.e)
    pieces = [w[:, s0:s0 + D] for s0 in starts]
    pieces[0] = pieces[0] * q_fold
    w8 = jnp.stack(pieces).astype(BF16)
    w_if = w[:, o_if:o_if + GATE_ROWS]
    wg = jnp.pad(w_if, ((0, 0), (0, GATE_LANES - GATE_ROWS))).astype(BF16)
    wgt = w_if.T.astype(BF16)
    gate_bias = b_if[l].reshape(GATE_ROWS)
    bcol = jnp.pad(gate_bias, (0, GATE_LANES - GATE_ROWS)).reshape(1, GATE_LANES)
    brow = gate_bias.reshape(GATE_ROWS, 1)

    x2 = x.reshape(T, D)
    z, gcol, grow = _in_proj(x2, mod3, w8, wg, wgt, bcol, brow, S)
    z4 = z.reshape(len(pieces), B, S, D)

    ya = _diff_attention(z4, da_lambda[l], da_subln_g[l], B, S, lambda_init)
    ym = _mlstm(z4, gcol.reshape(B, S, GATE_LANES), grow, conv_w[l], conv_b[l].reshape(1, -1),
                ml_norm_g[l].reshape(1, -1), B, S)

    x1, h2 = _merge(ya.reshape(T, D), ym.reshape(T, D), z, x2, mod3,
                    w_br_attn[l].astype(BF16), w_br_mlstm[l].astype(BF16), w_out[l].astype(BF16),
                    ln1_g[l].reshape(1, D), ln1_b[l].reshape(1, D), S)

    rk2, e2, c1, e1 = _peer_route(h2, peer_wq[l].T.astype(BF16), peer_keys[l].astype(BF16))
    vt_blocks = peer_v[l].reshape(N_EXPERTS // EXP_EB, EXP_EB, D).transpose(0, 2, 1).astype(BF16)
    out = _peer_experts(h2, peer_u[l].astype(BF16), vt_blocks, rk2, e2, c1, e1, x1, mod3,
                        ln2_g[l].reshape(1, D), ln2_b[l].reshape(1, D), S)
    return out.reshape(B, S, D)
```

```python
import functools
import math

import jax
import jax.numpy as jnp
from jax import lax
from jax.experimental import pallas as pl
from jax.experimental.pallas import tpu as pltpu

D_MODEL = 1024
DA_HEADS = 8
DA_DK = 64
DA_DV = 2 * DA_DK
ML_HEADS = 4
ML_DK = 128
ML_DV = 256
ML_CHUNK = 128
CONV_K = 4
PEER_HEADS = 8
PEER_TOPK = 16
N_KEYS = 128
N_EXPERTS = N_KEYS * N_KEYS
PEER_DKEY = 128
DEPTH = 1
ALPHA = (2 * DEPTH) ** 0.25
LN_EPS = 1e-5

F32 = jnp.float32
BF16 = jnp.bfloat16
NEG_INF = float("-inf")

VMEM_LIMIT_BYTES = 56 * 1024 * 1024


def _cparams(sem):
    return pltpu.CompilerParams(dimension_semantics=sem, vmem_limit_bytes=VMEM_LIMIT_BYTES)


def _layer_norm_rows(x):
    mu = jnp.mean(x, axis=-1, keepdims=True)
    xc = x - mu
    var = jnp.mean(xc * xc, axis=-1, keepdims=True)
    return xc * lax.rsqrt(var + LN_EPS)


def _dot(a, b):
    return jnp.dot(a, b, preferred_element_type=F32)


def _dot_nt(a, b):
    return lax.dot_general(a, b, (((1,), (1,)), ((), ())), preferred_element_type=F32)


def _dot_tn(a, b):
    return lax.dot_general(a, b, (((0,), (0,)), ((), ())), preferred_element_type=F32)


def _mod_kernel(c_ref, w_ref, b_ref, o_ref):
    c = c_ref[...]
    a = c * jax.nn.sigmoid(c)
    o_ref[...] = jnp.dot(a, w_ref[...], preferred_element_type=F32,
                         precision=lax.Precision.HIGHEST) + b_ref[...]


def _modulation(c, w_ada, b_ada):
    B, D = c.shape
    N = w_ada.shape[1]
    tn = 1024
    return pl.pallas_call(
        _mod_kernel,
        grid=(N // tn,),
        in_specs=[pl.BlockSpec((B, D), lambda n: (0, 0)),
                  pl.BlockSpec((D, tn), lambda n: (0, n)),
                  pl.BlockSpec((1, tn), lambda n: (0, n))],
        out_specs=pl.BlockSpec((B, tn), lambda n: (0, n)),
        out_shape=jax.ShapeDtypeStruct((B, N), F32),
        compiler_params=_cparams(("arbitrary",)),
        name="modulation",
    )(c, w_ada, b_ada.reshape(1, N))


INPROJ_TM = 1024
INPROJ_ROWS = 256
GATE_ROWS = 2 * ML_HEADS
GATE_LANES = 128
SUBLANES = 8


def _inproj_kernel(x_ref, mod_ref, w_ref, wg_ref, wgt_ref, bcol_ref, brow_ref,
                   z_ref, gcol_ref, grow_ref, h_scr):
    n = pl.program_id(1)

    @pl.when(n == 0)
    def _():
        sh1 = mod_ref[0, 0:1, :]
        sc1 = mod_ref[0, 1:2, :]
        tm = x_ref.shape[0]
        for c in range(tm // INPROJ_ROWS):
            rows = slice(c * INPROJ_ROWS, (c + 1) * INPROJ_ROWS)
            hb = (_layer_norm_rows(x_ref[rows, :]) * (1.0 + sc1) + sh1).astype(BF16)
            h_scr[rows, :] = hb
            z_ref[0, rows, :] = _dot(hb, w_ref[0]).astype(BF16)
            gcol_ref[rows, :] = _dot(hb, wg_ref[...]) + bcol_ref[...]
            grow_ref[:, rows] = _dot_nt(wgt_ref[...], hb) + brow_ref[...]

    @pl.when(n > 0)
    def _():
        z_ref[0] = _dot(h_scr[...], w_ref[0]).astype(BF16)


def _in_proj(x2, mod3, w8, wg, wgt, bcol, brow, S):
    T, D = x2.shape
    tm = INPROJ_TM
    npiece = w8.shape[0]
    return pl.pallas_call(
        _inproj_kernel,
        grid=(T // tm, npiece),
        in_specs=[pl.BlockSpec((tm, D), lambda i, n: (i, 0)),
                  pl.BlockSpec((1, 6, D), lambda i, n: ((i * tm) // S, 0, 0)),
                  pl.BlockSpec((1, D, D), lambda i, n: (n, 0, 0)),
                  pl.BlockSpec((D, GATE_LANES), lambda i, n: (0, 0)),
                  pl.BlockSpec((GATE_ROWS, D), lambda i, n: (0, 0)),
                  pl.BlockSpec((1, GATE_LANES), lambda i, n: (0, 0)),
                  pl.BlockSpec((GATE_ROWS, 1), lambda i, n: (0, 0))],
        out_specs=[pl.BlockSpec((1, tm, D), lambda i, n: (n, i, 0)),
                   pl.BlockSpec((tm, GATE_LANES), lambda i, n: (i, 0)),
                   pl.BlockSpec((GATE_ROWS, tm), lambda i, n: (0, i))],
        out_shape=[jax.ShapeDtypeStruct((npiece, T, D), BF16),
                   jax.ShapeDtypeStruct((T, GATE_LANES), F32),
                   jax.ShapeDtypeStruct((GATE_ROWS, T), F32)],
        scratch_shapes=[pltpu.VMEM((tm, D), BF16)],
        compiler_params=_cparams(("parallel", "arbitrary")),
        name="in_proj",
    )(x2, mod3, w8, wg, wgt, bcol, brow)


DA_TQ = 256


def _diffattn_kernel(lam_ref, g_ref, q_ref, k_ref, v_ref, o_ref, s_scr, vext_scr, *, S, lambda_init):
    tq = DA_TQ
    nq = S // tq
    lam = lam_ref[...]
    t1 = jnp.sum(lam[0:1] * lam[1:2], axis=-1, keepdims=True)
    t2 = jnp.sum(lam[2:3] * lam[3:4], axis=-1, keepdims=True)
    lam_val = jnp.exp(t1) - jnp.exp(t2) + lambda_init
    first_map = lax.broadcasted_iota(jnp.int32, (1, DA_DV), 1) < DA_DK
    gain = g_ref[...] * (1.0 - lambda_init)
    row = lax.broadcasted_iota(jnp.int32, (tq, tq), 0)
    col = lax.broadcasted_iota(jnp.int32, (tq, tq), 1)
    causal = col <= row
    ones_col = (lax.broadcasted_iota(jnp.int32, (S, DA_DV), 1) == 0).astype(BF16)
    vext_scr[:, 0:DA_DV] = v_ref[0, 0]
    vext_scr[:, DA_DV:2 * DA_DV] = ones_col

    def scores(qi):
        q0 = qi * tq
        qs = q_ref[0, 0, q0:q0 + tq, :]
        zero = jnp.zeros_like(qs)
        for mp, qm in enumerate((jnp.where(first_map, qs, zero), jnp.where(first_map, zero, qs))):
            s_scr[qi % 2, mp, :, 0:q0 + tq] = _dot_nt(qm, k_ref[0, 0, 0:q0 + tq, :])

    def finish(qi):
        q0 = qi * tq
        slot = qi % 2
        outs = []
        for mp in range(2):
            s_diag = jnp.where(causal, s_scr[slot, mp, :, q0:q0 + tq], NEG_INF)
            m = jnp.max(s_diag, axis=-1, keepdims=True)
            if qi > 0:
                s_off = s_scr[slot, mp, :, 0:q0]
                m = jnp.maximum(m, jnp.max(s_off, axis=-1, keepdims=True))
            acc = _dot(jnp.exp2(s_diag - m).astype(BF16), vext_scr[q0:q0 + tq, :])
            if qi > 0:
                acc = acc + _dot(jnp.exp2(s_off - m).astype(BF16), vext_scr[0:q0, :])
            outs.append(acc[:, 0:DA_DV] / acc[:, DA_DV:DA_DV + 1])
        o = outs[0] - lam_val * outs[1]
        o = o * lax.rsqrt(jnp.mean(o * o, axis=-1, keepdims=True) + LN_EPS) * gain
        o_ref[0, q0:q0 + tq, :] = o.astype(BF16)

    scores(0)
    for qi in range(nq):
        if qi + 1 < nq:
            scores(qi + 1)
        finish(qi)


def _diff_attention(z4, da_lambda, subln_g, B, S, lambda_init):
    kern = functools.partial(_diffattn_kernel, S=S, lambda_init=lambda_init)
    return pl.pallas_call(
        kern,
        grid=(B, DA_HEADS),
        in_specs=[pl.BlockSpec((4, DA_DK), lambda b, h: (0, 0)),
                  pl.BlockSpec((1, DA_DV), lambda b, h: (0, 0)),
                  pl.BlockSpec((1, 1, S, DA_DV), lambda b, h: (0, b, 0, h)),
                  pl.BlockSpec((1, 1, S, DA_DV), lambda b, h: (1, b, 0, h)),
                  pl.BlockSpec((1, 1, S, DA_DV), lambda b, h: (2, b, 0, h))],
        out_specs=pl.BlockSpec((1, S, DA_DV), lambda b, h: (b, 0, h)),
        out_shape=jax.ShapeDtypeStruct((B, S, DA_HEADS * DA_DV), BF16),
        scratch_shapes=[pltpu.VMEM((2, 2, DA_TQ, S), F32), pltpu.VMEM((S, 2 * DA_DV), BF16)],
        compiler_params=_cparams(("parallel", "parallel")),
        name="diff_attention",
    )(da_lambda, subln_g.reshape(1, DA_DV), z4, z4, z4)


def _mlstm_kernel(qk_ref, v_ref, og_ref, gcol_ref, grow_ref, cw_ref, cb_ref, ng_ref, o_ref,
                  qc_scr, kc_scr, c_scr, n_scr, m_scr, *, S):
    L = ML_CHUNK
    H = ML_HEADS
    srow = lax.broadcasted_iota(jnp.int32, (S, ML_DK), 0)

    for cb in range(2 * H):
        cols = slice(cb * ML_DK, (cb + 1) * ML_DK)
        x = qk_ref[0, 0, :, cols].astype(F32)
        y = x * cw_ref[CONV_K - 1:CONV_K, cols] + cb_ref[:, cols]
        for j in range(1, CONV_K):
            xs = jnp.where(srow >= j, pltpu.roll(x, j, 0), 0.0)
            y = y + xs * cw_ref[CONV_K - 1 - j:CONV_K - j, cols]
        y = y * jax.nn.sigmoid(y)
        if cb < H:
            qc_scr[:, cols] = y
        else:
            kc_scr[:, (cb - H) * ML_DK:(cb - H + 1) * ML_DK] = y * (ML_DK ** -0.5)

    r_i = lax.broadcasted_iota(jnp.int32, (L, L), 0)
    c_i = lax.broadcasted_iota(jnp.int32, (L, L), 1)
    causal = c_i <= r_i
    tril = causal.astype(F32)
    triu = (r_i <= c_i).astype(F32)
    c_scr[...] = jnp.zeros_like(c_scr)
    n_scr[...] = jnp.zeros_like(n_scr)
    m_scr[...] = jnp.zeros_like(m_scr)

    def head_chunk(hh, t0, gc, gr, b_cols, b_rows):
        Ct = c_scr[hh]
        n_row = n_scr[hh]
        m = m_scr[hh]
        q = qc_scr[pl.ds(t0, L), hh * ML_DK:(hh + 1) * ML_DK]
        k = kc_scr[pl.ds(t0, L), hh * ML_DK:(hh + 1) * ML_DK]
        v = v_ref[0, 0, pl.ds(t0, L), hh * ML_DV:(hh + 1) * ML_DV]
        ngain = ng_ref[:, hh * ML_DV:(hh + 1) * ML_DV]
        ig_col = gc[:, hh:hh + 1]
        ig_row = gr[hh:hh + 1, :]
        b_col = b_cols[:, H + hh:H + hh + 1]
        b_row = b_rows[H + hh:H + hh + 1, :]
        dm = jnp.where(causal, b_col - b_row + ig_row, NEG_INF)
        m_inter = b_col + m
        m_t = jnp.maximum(m_inter, jnp.max(dm, axis=-1, keepdims=True))
        w = jnp.exp(dm - m_t)
        qb = q.astype(BF16)
        kb = k.astype(BF16)
        p = w * _dot_nt(qb, kb)
        inter = jnp.exp(m_inter - m_t)
        num = _dot(p.astype(BF16), v) + inter * _dot(qb, Ct.astype(BF16))
        nq = jnp.sum(p, axis=-1, keepdims=True) + inter * jnp.sum(q * n_row, axis=-1, keepdims=True)
        hout = num / jnp.maximum(jnp.abs(nq), jnp.exp(-m_t))
        hout = hout * lax.rsqrt(jnp.mean(hout * hout, axis=-1, keepdims=True) + LN_EPS) * ngain
        og = og_ref[0, 0, pl.ds(t0, L), hh * ML_DV:(hh + 1) * ML_DV].astype(F32)
        o_ref[0, pl.ds(t0, L), hh * ML_DV:(hh + 1) * ML_DV] = (hout * jax.nn.sigmoid(og)).astype(BF16)
        m_new = m_t[L - 1:L, :]
        b_last = b_col[L - 1:L, :]
        decay = jnp.exp(b_last + m - m_new)
        w_s = jnp.exp(b_last - b_col + ig_col - m_new)
        c_scr[hh] = decay * Ct + _dot_tn(kb, (v.astype(F32) * w_s).astype(BF16))
        n_scr[hh] = decay * n_row + jnp.sum(k * w_s, axis=0, keepdims=True)
        m_scr[hh] = m_new

    def chunk(ci, carry):
        t0 = pl.multiple_of(ci * L, L)
        gc = gcol_ref[0, pl.ds(t0, L), :]
        gr = grow_ref[:, pl.ds(t0, L)]
        b_cols = jnp.dot(tril, jax.nn.log_sigmoid(gc), preferred_element_type=F32,
                         precision=lax.Precision.HIGHEST)
        b_rows = jnp.dot(jax.nn.log_sigmoid(gr), triu, preferred_element_type=F32,
                         precision=lax.Precision.HIGHEST)
        for hh in range(H):
            head_chunk(hh, t0, gc, gr, b_cols, b_rows)
        return carry

    lax.fori_loop(0, S // L, chunk, 0)


def _mlstm(z4, gcol3, grow, conv_w, conv_b, norm_g, B, S):
    kern = functools.partial(_mlstm_kernel, S=S)
    H = ML_HEADS
    D = H * ML_DV
    piece = lambda n: pl.BlockSpec((1, 1, S, D), lambda b: (n, b, 0, 0))
    return pl.pallas_call(
        kern,
        grid=(B,),
        in_specs=[piece(3), piece(4), piece(5),
                  pl.BlockSpec((1, S, GATE_LANES), lambda b: (b, 0, 0)),
                  pl.BlockSpec((GATE_ROWS, S), lambda b: (0, b)),
                  pl.BlockSpec((CONV_K, 2 * H * ML_DK), lambda b: (0, 0)),
                  pl.BlockSpec((1, 2 * H * ML_DK), lambda b: (0, 0)),
                  pl.BlockSpec((1, D), lambda b: (0, 0))],
        out_specs=pl.BlockSpec((1, S, D), lambda b: (b, 0, 0)),
        out_shape=jax.ShapeDtypeStruct((B, S, D), BF16),
        scratch_shapes=[pltpu.VMEM((S, H * ML_DK), F32), pltpu.VMEM((S, H * ML_DK), F32),
                        pltpu.VMEM((H, ML_DK, ML_DV), F32), pltpu.VMEM((H, 1, ML_DK), F32),
                        pltpu.VMEM((H, 1, 1), F32)],
        compiler_params=_cparams(("parallel",)),
        name="mlstm",
    )(z4, z4, z4, gcol3, grow, conv_w, conv_b, norm_g)


MERGE_ROWS = 256


def _merge_kernel(ya_ref, ym_ref, ga_ref, gm_ref, x_ref, mod_ref, wa_ref, wm_ref, wo_ref, g1_ref, b1_ref,
                  x1_ref, h2_ref):
    gt1 = mod_ref[0, 2:3, :]
    sh2 = mod_ref[0, 3:4, :]
    sc2 = mod_ref[0, 4:5, :]
    for c in range(ya_ref.shape[0] // MERGE_ROWS):
        rows = slice(c * MERGE_ROWS, (c + 1) * MERGE_ROWS)
        ya = _dot(ya_ref[rows, :], wa_ref[...])
        ym = _dot(ym_ref[rows, :], wm_ref[...])
        y = (jax.nn.sigmoid(ga_ref[0, rows, :].astype(F32)) * ya
             + jax.nn.sigmoid(gm_ref[0, rows, :].astype(F32)) * ym)
        y2 = _dot(y.astype(BF16), wo_ref[...])
        x1 = _layer_norm_rows(ALPHA * x_ref[rows, :] + gt1 * y2) * g1_ref[...] + b1_ref[...]
        x1_ref[rows, :] = x1
        h2_ref[:, rows] = (_layer_norm_rows(x1) * (1.0 + sc2) + sh2).T.astype(BF16)


def _merge(ya2, ym2, z3, x2, mod3, wa, wm, wo, ln_g, ln_b, S):
    T, D = x2.shape
    tm = 512
    tok = lambda i: (i, 0)
    const = lambda i: (0, 0)
    return pl.pallas_call(
        _merge_kernel,
        grid=(T // tm,),
        in_specs=[pl.BlockSpec((tm, D), tok), pl.BlockSpec((tm, D), tok),
                  pl.BlockSpec((1, tm, D), lambda i: (6, i, 0)),
                  pl.BlockSpec((1, tm, D), lambda i: (7, i, 0)),
                  pl.BlockSpec((tm, D), tok),
                  pl.BlockSpec((1, 6, D), lambda i: ((i * tm) // S, 0, 0)),
                  pl.BlockSpec((D, D), const), pl.BlockSpec((D, D), const), pl.BlockSpec((D, D), const),
                  pl.BlockSpec((1, D), const), pl.BlockSpec((1, D), const)],
        out_specs=[pl.BlockSpec((tm, D), tok), pl.BlockSpec((D, tm), lambda i: (0, i))],
        out_shape=[jax.ShapeDtypeStruct((T, D), F32), jax.ShapeDtypeStruct((D, T), BF16)],
        compiler_params=_cparams(("parallel",)),
        name="merge",
    )(ya2, ym2, z3, z3, x2, mod3, wa, wm, wo, ln_g, ln_b)


PEER_TB = 256
_CAND_COLS = [[k1 for k1 in range(PEER_TOPK) if (k1 + 1) * (k2 + 1) <= PEER_TOPK] for k2 in range(PEER_TOPK)]


def _batcher_pairs(n):
    pairs = []
    p = 1
    while p < n:
        k = p
        while k >= 1:
            for j in range(k % p, n - k, 2 * k):
                for i in range(min(k, n - j - k)):
                    if (i + j) // (2 * p) == (i + j + k) // (2 * p):
                        pairs.append((i + j, i + j + k))
            k //= 2
        p *= 2
    return pairs


_SORT16 = _batcher_pairs(PEER_TOPK)


def _cmpx(v, i, j):
    a, b = v[i], v[j]
    if b is None:
        return
    if a is None:
        v[i], v[j] = b, None
        return
    v[i], v[j] = jnp.maximum(a, b), jnp.minimum(a, b)


def _sort16_desc(v):
    v = list(v)
    for i, j in _SORT16:
        _cmpx(v, i, j)
    return v


def _bitonic_merge_desc(v):
    v = list(v)
    d = PEER_TOPK // 2
    while d >= 1:
        for i in range(PEER_TOPK):
            if i & d == 0:
                _cmpx(v, i, i + d)
        d //= 2
    return v


def _half_clean(a, b):
    out = []
    for g in range(PEER_TOPK):
        x, y = a[g], b[PEER_TOPK - 1 - g]
        out.append(x if y is None else (y if x is None else jnp.maximum(x, y)))
    return out


def _top16_all_sublanes(scores):
    assert N_KEYS == PEER_TOPK * SUBLANES
    rows = [_sort16_desc([s[g * SUBLANES:(g + 1) * SUBLANES, :] for g in range(PEER_TOPK)]) for s in scores]
    for shift in (4, 2, 1):
        partner = [[pltpu.roll(r, shift, 0) for r in rr] for rr in rows]
        rows = [_bitonic_merge_desc(_half_clean(rr, pp)) for rr, pp in zip(rows, partner)]
    return rows


def _route_kernel(h_ref, wqt_ref, keys_ref, rk2_ref, e2_ref, c1_ref, e1_ref, q_scr, s_scr, a_scr, f_scr):
    half = PEER_DKEY // 2
    K = PEER_TOPK
    q_scr[...] = _dot(wqt_ref[...], h_ref[...]).astype(BF16)

    def score_matmuls(h_src, slot):
        base = pl.multiple_of(h_src * PEER_DKEY, PEER_DKEY)
        for p in range(2):
            s_scr[p, slot] = _dot(keys_ref[p], q_scr[pl.ds(base + p * half, half), :])

    score_matmuls(0, 0)

    def stage_a(h, carry):
        score_matmuls(jnp.minimum(h + 1, PEER_HEADS - 1), h + 1)
        tops = _top16_all_sublanes([s_scr[p, h] for p in range(2)])
        for p in range(2):
            for k in range(K):
                a_scr[p, k, pl.ds(h, 1), :] = tops[p][k][0:1, :]
        return carry

    lax.fori_loop(0, PEER_HEADS, stage_a, 0)

    a1 = [a_scr[0, k] for k in range(K)]
    a2 = [a_scr[1, k] for k in range(K)]
    cand = [[a1[k1] + a2[k2] for k1 in col] for k2, col in enumerate(_CAND_COLS)]
    g0 = [cand[k2][0] for k2 in range(K)]
    rest = [cand[k2][i] for k2 in range(K) for i in range(1, len(_CAND_COLS[k2]))]
    rest += [None] * (-len(rest) % K)
    groups = [g0] + [_sort16_desc(rest[i:i + K]) for i in range(0, len(rest), K)]
    while len(groups) > 2:
        merged = [_bitonic_merge_desc(_half_clean(groups[i], groups[i + 1])) for i in range(0, len(groups) - 1, 2)]
        groups = merged + ([groups[-1]] if len(groups) % 2 else [])
    last = [x for x in _half_clean(groups[0], groups[1]) if x is not None]
    tau = functools.reduce(jnp.minimum, last)
    cmax = a1[0] + a2[0]
    zsum = None
    for k2, col in enumerate(_CAND_COLS):
        phi = None
        for i, k1 in enumerate(col):
            c = cand[k2][i]
            hit = c >= tau
            term = jnp.where(hit, jnp.exp(c - cmax), 0.0)
            zsum = term if zsum is None else zsum + term
            lo = jnp.where(hit, a1[k1], jnp.inf)
            phi = lo if phi is None else jnp.minimum(phi, lo)
        f_scr[k2] = phi
    f_scr[K] = 1.0 / zsum

    def prefix_count(pred, thr):
        b8 = pred(thr[7])
        b4 = pred(jnp.where(b8, thr[11], thr[3]))
        b2 = pred(jnp.where(b8, jnp.where(b4, thr[13], thr[9]), jnp.where(b4, thr[5], thr[1])))
        b1 = pred(jnp.where(b8,
                            jnp.where(b4, jnp.where(b2, thr[14], thr[12]), jnp.where(b2, thr[10], thr[8])),
                            jnp.where(b4, jnp.where(b2, thr[6], thr[4]), jnp.where(b2, thr[2], thr[0]))))
        b0 = pred(thr[15])
        bit = lambda b, v: jnp.where(b, v, 0.0)
        return bit(b8, 8.0) + bit(b4, 4.0) + bit(b2, 2.0) + bit(b1, 1.0) + bit(b0, 1.0)

    def stage_c(h, carry):
        s1 = s_scr[0, h]
        s2 = s_scr[1, h]
        shape = s2.shape
        top2 = [jnp.broadcast_to(a_scr[1, k, pl.ds(h, 1), :], shape) for k in range(K)]
        phi = [jnp.broadcast_to(f_scr[k, pl.ds(h, 1), :], shape) for k in range(K)]
        rk2_ref[h] = prefix_count(lambda t: t > s2, top2).astype(BF16)
        e2_ref[h] = jnp.exp(s2 - a_scr[1, 0, pl.ds(h, 1), :]).astype(BF16)
        c1_ref[h] = prefix_count(lambda t: s1 >= t, phi)
        e1_ref[h] = jnp.exp(s1 - a_scr[0, 0, pl.ds(h, 1), :]) * f_scr[K, pl.ds(h, 1), :]
        return carry

    lax.fori_loop(0, PEER_HEADS, stage_c, 0)


def _peer_route(h2t, wqt, keys):
    D, T = h2t.shape
    tb = PEER_TB
    PH = PEER_HEADS
    blk = pl.BlockSpec((PH, N_KEYS, tb), lambda i: (0, 0, i))
    return pl.pallas_call(
        _route_kernel,
        grid=(T // tb,),
        in_specs=[pl.BlockSpec((D, tb), lambda i: (0, i)),
                  pl.BlockSpec((PH * PEER_DKEY, D), lambda i: (0, 0)),
                  pl.BlockSpec((2, N_KEYS, PEER_DKEY // 2), lambda i: (0, 0, 0))],
        out_specs=[blk, blk, blk, blk],
        out_shape=[jax.ShapeDtypeStruct((PH, N_KEYS, T), BF16),
                   jax.ShapeDtypeStruct((PH, N_KEYS, T), BF16),
                   jax.ShapeDtypeStruct((PH, N_KEYS, T), F32),
                   jax.ShapeDtypeStruct((PH, N_KEYS, T), F32)],
        scratch_shapes=[pltpu.VMEM((PH * PEER_DKEY, tb), BF16),
                        pltpu.VMEM((2, PH + 1, N_KEYS, tb), F32),
                        pltpu.VMEM((2, PEER_TOPK, PH, tb), F32),
                        pltpu.VMEM((PEER_TOPK + 1, PH, tb), F32)],
        compiler_params=_cparams(("parallel",)),
        name="peer_route",
    )(h2t, wqt, keys)


EXP_TB = 512
EXP_EB = 1024
BF16_ROWS = 16
LANES = 256


def _expert_step_order(nsub, ngroup):
    order = []
    per = nsub // ngroup
    for g in range(ngroup):
        order.append(("v", g))
        for j in range(g * per, (g + 1) * per):
            order += [("k", j), ("g", j)]
    return tuple(order)


def _experts_kernel(h_ref, u_ref, vt_ref, rk2_ref, e2_ref, c1_ref, e1_ref, x1_ref, mod_ref, g2_ref, b2_ref,
                    o_ref, acc_scr, at_scr, w0_scr, w1_scr):
    e = pl.program_id(1)
    ne = N_EXPERTS // EXP_EB
    nsub = EXP_EB // N_KEYS
    ngrp = N_KEYS // BF16_ROWS
    tb = h_ref.shape[1]
    nhalf = tb // LANES
    assert nsub % nhalf == 0

    def value_matmul(w_read, half):
        cols = slice(half * LANES, (half + 1) * LANES)
        acc_scr[:, cols] += _dot(vt_ref[0], w_read[:, cols])

    def key_matmul(j):
        krows = slice(j * N_KEYS, (j + 1) * N_KEYS)
        at_scr[krows, :] = _dot(u_ref[krows, :], h_ref[...])

    def gated_activation(j, w_write):
        for lt in range(nhalf):
            cols = slice(lt * LANES, (lt + 1) * LANES)
            gates = [None] * ngrp
            for h in range(PEER_HEADS):
                cnt = jnp.broadcast_to(c1_ref[h, j:j + 1, cols], (BF16_ROWS, LANES)).astype(BF16)
                e1 = jnp.broadcast_to(e1_ref[h, j:j + 1, cols], (BF16_ROWS, LANES)).astype(BF16)
                for r in range(ngrp):
                    rows = slice(r * BF16_ROWS, (r + 1) * BF16_ROWS)
                    term = jnp.where(rk2_ref[h, rows, cols] < cnt, e2_ref[h, rows, cols],
                                     jnp.zeros((), BF16)) * e1
                    gates[r] = term if gates[r] is None else gates[r] + term
            for r in range(ngrp):
                rows = slice(j * N_KEYS + r * BF16_ROWS, j * N_KEYS + (r + 1) * BF16_ROWS)
                a = at_scr[rows, cols].astype(BF16)
                act = (0.5 * a) * (1.0 + lax.erf(a * (2.0 ** -0.5)))
                w_write[rows, cols] = gates[r] * act

    def step(w_write, w_read):
        for kind, idx in _expert_step_order(nsub, nhalf):
            if kind == "v":
                value_matmul(w_read, idx)
            elif kind == "k":
                key_matmul(idx)
            else:
                gated_activation(idx, w_write)

    parity = lax.rem(e, 2)

    @pl.when(e == 0)
    def _():
        acc_scr[...] = jnp.zeros_like(acc_scr)
        w1_scr[...] = jnp.zeros_like(w1_scr)

    @pl.when((e < ne) & (parity == 0))
    def _():
        step(w0_scr, w1_scr)

    @pl.when((e < ne) & (parity == 1))
    def _():
        step(w1_scr, w0_scr)

    @pl.when(e == ne)
    def _():
        w_last = w1_scr if ne % 2 == 0 else w0_scr
        for half in range(nhalf):
            value_matmul(w_last, half)
        yf = acc_scr[...].T
        gt2 = mod_ref[0, 5:6, :]
        r = ALPHA * x1_ref[...] + gt2 * yf
        o_ref[...] = _layer_norm_rows(r) * g2_ref[...] + b2_ref[...]


def _peer_experts(h2t, u_b, vt_b, rk2, e2, c1, e1, x1, mod3, ln_g, ln_b, S):
    D, T = h2t.shape
    tb, eb = EXP_TB, EXP_EB
    PH = PEER_HEADS
    route = pl.BlockSpec((PH, N_KEYS, tb), lambda i, e: (0, 0, i))
    ne = N_EXPERTS // eb
    key1 = pl.BlockSpec((PH, eb // N_KEYS, tb), lambda i, e: (0, jnp.minimum(e, ne - 1), i))
    return pl.pallas_call(
        _experts_kernel,
        grid=(T // tb, ne + 1),
        in_specs=[pl.BlockSpec((D, tb), lambda i, e: (0, i)),
                  pl.BlockSpec((eb, D), lambda i, e: (jnp.minimum(e, ne - 1), 0)),
                  pl.BlockSpec((1, D, eb), lambda i, e: (jnp.maximum(e - 1, 0), 0, 0)),
                  route, route, key1, key1,
                  pl.BlockSpec((tb, D), lambda i, e: (i, 0)),
                  pl.BlockSpec((1, 6, D), lambda i, e: ((i * tb) // S, 0, 0)),
                  pl.BlockSpec((1, D), lambda i, e: (0, 0)),
                  pl.BlockSpec((1, D), lambda i, e: (0, 0))],
        out_specs=pl.BlockSpec((tb, D), lambda i, e: (i, 0)),
        out_shape=jax.ShapeDtypeStruct((T, D), F32),
        scratch_shapes=[pltpu.VMEM((D, tb), F32), pltpu.VMEM((eb, tb), F32),
                        pltpu.VMEM((eb, tb), BF16), pltpu.VMEM((eb, tb), BF16)],
        compiler_params=_cparams(("parallel", "arbitrary")),
        name="peer_experts",
    )(h2t, u_b, vt_b, rk2, e2, c1, e1, x1, mod3, ln_g, ln_b)


def kernel(x, c, w_ada, b_ada, w_in, b_if, conv_w, conv_b, da_lambda, da_subln_g, ml_norm_g, w_br_attn,
           w_br_mlstm, w_out, ln1_g, ln1_b, peer_wq, peer_keys, peer_u, peer_v, ln2_g, ln2_b):
    B, S, D = x.shape
    T = B * S
    assert D == D_MODEL and S % DA_TQ == 0 and S % INPROJ_TM == 0 and S % EXP_TB == 0 and S % ML_CHUNK == 0
    l = 0
    lambda_init = 0.8 - 0.6 * math.exp(-0.3 * l)

    mod3 = _modulation(c, w_ada[l], b_ada[l]).reshape(B, 6, D)

    w = w_in[l]
    o_mq = 3 * D
    o_mv = o_mq + 2 * ML_HEADS * ML_DK
    o_mo = o_mv + D
    o_if = o_mo + D
    o_ga = o_if + GATE_ROWS
    o_gm = o_ga + D
    starts = (0, D, 2 * D, o_mq, o_mv, o_mo, o_ga, o_gm)
    q_fold = (DA_DK ** -0.5) * math.log2(math.e)
    pieces = [w[:, s0:s0 + D] for s0 in starts]
    pieces[0] = pieces[0] * q_fold
    w8 = jnp.stack(pieces).astype(BF16)
    w_if = w[:, o_if:o_if + GATE_ROWS]
    wg = jnp.pad(w_if, ((0, 0), (0, GATE_LANES - GATE_ROWS))).astype(BF16)
    wgt = w_if.T.astype(BF16)
    gate_bias = b_if[l].reshape(GATE_ROWS)
    bcol = jnp.pad(gate_bias, (0, GATE_LANES - GATE_ROWS)).reshape(1, GATE_LANES)
    brow = gate_bias.reshape(GATE_ROWS, 1)

    x2 = x.reshape(T, D)
    z, gcol, grow = _in_proj(x2, mod3, w8, wg, wgt, bcol, brow, S)
    z4 = z.reshape(len(pieces), B, S, D)

    ya = _diff_attention(z4, da_lambda[l], da_subln_g[l], B, S, lambda_init)
    ym = _mlstm(z4, gcol.reshape(B, S, GATE_LANES), grow, conv_w[l], conv_b[l].reshape(1, -1),
                ml_norm_g[l].reshape(1, -1), B, S)

    x1, h2 = _merge(ya.reshape(T, D), ym.reshape(T, D), z, x2, mod3,
                    w_br_attn[l].astype(BF16), w_br_mlstm[l].astype(BF16), w_out[l].astype(BF16),
                    ln1_g[l].reshape(1, D), ln1_b[l].reshape(1, D), S)

    rk2, e2, c1, e1 = _peer_route(h2, peer_wq[l].T.astype(BF16), peer_keys[l].astype(BF16))
    vt_blocks = peer_v[l].reshape(N_EXPERTS // EXP_EB, EXP_EB, D).transpose(0, 2, 1).astype(BF16)
    out = _peer_experts(h2, peer_u[l].astype(BF16), vt_blocks, rk2, e2, c1, e1, x1, mod3,
                        ln2_g[l].reshape(1, D), ln2_b[l].reshape(1, D), S)
    return out.reshape(B, S, D)
```
